```python
import jax, jax.numpy as jnp
from jax import lax
import numpy as np

D_MODEL = 1024
BATCH = 4
SEQ = 8192
DEPTH = 2

N_MEM = 256
RET_HEADS = 4
RET_HEAD_DIM = 128
RET_WIDTH = RET_HEADS * RET_HEAD_DIM
RET_CHUNK = 128
CONV_CH = 512
CONV_WIDTH = 31
XATTN_HEADS = 4
XATTN_HEAD_DIM = 128
XATTN_WIDTH = XATTN_HEADS * XATTN_HEAD_DIM
N_EXPERTS = 16
N_GROUPS = 4
EXPERTS_PER_GROUP = N_EXPERTS // N_GROUPS
TOP_K = 2
D_EXPERT = 512
ROPE_BASE = 10000.0
EPS = 1e-6
IN_COLS = 4 * RET_WIDTH + 2 * CONV_CH + 2 * D_MODEL

kernel_name = 'hybrid_retention_conformer_xattn_grouped_moe'


def _rmsnorm(x, g):
    xf = x.astype(jnp.float32)
    y = xf * lax.rsqrt(jnp.mean(xf * xf, axis=-1, keepdims=True) + EPS)
    return (y * g.astype(jnp.float32)).astype(x.dtype)


def _rotary(t, positions):
    half = t.shape[-1] // 2
    inv_freq = ROPE_BASE ** (-jnp.arange(half, dtype=jnp.float32) / half)
    ang = positions.astype(jnp.float32)[:, None] * inv_freq[None, :]
    cos = jnp.cos(ang)[None, :, None, :]
    sin = jnp.sin(ang)[None, :, None, :]
    tf = t.astype(jnp.float32)
    t1, t2 = tf[..., :half], tf[..., half:]
    return jnp.concatenate([t1 * cos - t2 * sin, t2 * cos + t1 * sin], axis=-1).astype(t.dtype)


def _retention(q, k, v):
    b, s, h, dh = q.shape
    c = RET_CHUNK
    n = s // c
    log_gamma = jnp.log(1.0 - 2.0 ** (-5.0 - jnp.arange(h, dtype=jnp.float32)))
    qf = q.astype(jnp.float32).reshape(b, n, c, h, dh)
    kf = k.astype(jnp.float32).reshape(b, n, c, h, dh) * (dh ** -0.5)
    vf = v.astype(jnp.float32).reshape(b, n, c, h, dh)
    idx = jnp.arange(c, dtype=jnp.float32)
    diff = idx[:, None] - idx[None, :]
    decay = jnp.where(diff[None] >= 0.0,
                      jnp.exp(jnp.maximum(diff, 0.0)[None] * log_gamma[:, None, None]), 0.0)
    scores = jnp.einsum('bnihd,bnjhd->bnhij', qf, kf) * decay[None, None]
    inner = jnp.einsum('bnhij,bnjhe->bnihe', scores, vf)
    zeta = jnp.exp((c - 1.0 - idx)[None, :] * log_gamma[:, None])
    kv = jnp.einsum('bnjhd,hj,bnjhe->nbhde', kf, zeta, vf)
    chunk_decay = jnp.exp(c * log_gamma)[None, :, None, None]

    def step(state, kv_n):
        return chunk_decay * state + kv_n, state

    _, prev = lax.scan(step, jnp.zeros((b, h, dh, dh), jnp.float32), kv)
    xi = jnp.exp((idx + 1.0)[None, :] * log_gamma[:, None])
    cross = jnp.einsum('bnihd,nbhde,hi->bnihe', qf, prev, xi)
    return (inner + cross).reshape(b, s, h, dh)


def _conv_module(u, conv_w, conv_b, ln_g, ln_b, w_proj):
    a, g = jnp.split(u, 2, axis=-1)
    z = a * jax.nn.sigmoid(g)
    z = jnp.pad(z, ((0, 0), (CONV_WIDTH - 1, 0), (0, 0)))
    z = lax.conv_general_dilated(z, conv_w[:, None, :], window_strides=(1,), padding='VALID',
                                 dimension_numbers=('NWC', 'WIO', 'NWC'),
                                 feature_group_count=CONV_CH) + conv_b
    zf = z.astype(jnp.float32)
    mu = jnp.mean(zf, axis=-1, keepdims=True)
    var = jnp.mean(jnp.square(zf - mu), axis=-1, keepdims=True)
    zf = (zf - mu) * lax.rsqrt(var + EPS) * ln_g.astype(jnp.float32) + ln_b.astype(jnp.float32)
    z = jax.nn.silu(zf).astype(u.dtype)
    return z @ w_proj


def _cross_attention(h, mem_n, w_xq, w_xkv, w_xo):
    b, s, _ = h.shape
    m = mem_n.shape[1]
    q = (h @ w_xq).reshape(b, s, XATTN_HEADS, XATTN_HEAD_DIM)
    kv = (mem_n @ w_xkv).reshape(b, m, 2, XATTN_HEADS, XATTN_HEAD_DIM)
    k, v = kv[:, :, 0], kv[:, :, 1]
    sc = jnp.einsum('bshd,bmhd->bhsm', q.astype(jnp.float32), k.astype(jnp.float32)) * (XATTN_HEAD_DIM ** -0.5)
    p = jax.nn.softmax(sc, axis=-1)
    o = jnp.einsum('bhsm,bmhd->bshd', p, v.astype(jnp.float32)).astype(h.dtype)
    return o.reshape(b, s, XATTN_WIDTH) @ w_xo


def _grouped_moe(h, w_router, b_router, w_gate, w_up, w_down):
    b, s, d = h.shape
    t = h.reshape(b * s, d)
    scores = jax.nn.sigmoid(t.astype(jnp.float32) @ w_router.astype(jnp.float32))
    sel = scores + b_router.astype(jnp.float32)
    grp = sel.reshape(-1, N_GROUPS, EXPERTS_PER_GROUP)
    grp_score = jnp.sum(lax.top_k(grp, TOP_K)[0], axis=-1)
    best_group = jnp.argmax(grp_score, axis=-1)
    in_group = (jnp.arange(N_EXPERTS) // EXPERTS_PER_GROUP)[None, :] == best_group[:, None]
    _, e_idx = lax.top_k(jnp.where(in_group, sel, -jnp.inf), TOP_K)
    w_sel = jnp.take_along_axis(scores, e_idx, axis=-1)
    w_sel = w_sel / jnp.sum(w_sel, axis=-1, keepdims=True)
    gates = jnp.sum(jax.nn.one_hot(e_idx, N_EXPERTS, dtype=jnp.float32) * w_sel[..., None], axis=1)
    y = jnp.zeros((b * s, d), jnp.float32)
    for e in range(N_EXPERTS):
        he = jax.nn.silu(t @ w_gate[e]) * (t @ w_up[e])
        y = y + gates[:, e:e + 1] * (he @ w_down[e]).astype(jnp.float32)
    return y.astype(h.dtype).reshape(b, s, d)


def setup_inputs(seed: int = 0) -> dict:
    key = jax.random.key(seed)
    ks = jax.random.split(key, 32)
    f32 = jnp.float32

    def nrm(k, shape, fan_in, scale=1.0):
        return jax.random.normal(k, shape, f32) * (scale * fan_in ** -0.5)

    def gain(k, shape):
        return 1.0 + 0.05 * jax.random.normal(k, shape, f32)

    return {
        'x': jax.random.normal(ks[0], (BATCH, SEQ, D_MODEL), f32),
        'mem': jax.random.normal(ks[1], (BATCH, N_MEM, D_MODEL), f32),
        'positions': jnp.arange(SEQ, dtype=jnp.int32),
        'norm_mix_g': gain(ks[2], (DEPTH, D_MODEL)),
        'w_in': nrm(ks[3], (DEPTH, D_MODEL, IN_COLS), D_MODEL),
        'b_branch_gate': 0.02 * jax.random.normal(ks[4], (DEPTH, 2 * D_MODEL), f32),
        'ret_norm_g': gain(ks[5], (DEPTH, RET_WIDTH)),
        'w_ret_out': nrm(ks[6], (DEPTH, RET_WIDTH, D_MODEL), RET_WIDTH),
        'conv_w': nrm(ks[7], (DEPTH, CONV_WIDTH, CONV_CH), CONV_WIDTH),
        'conv_b': 0.02 * jax.random.normal(ks[8], (DEPTH, CONV_CH), f32),
        'conv_ln_g': gain(ks[9], (DEPTH, CONV_CH)),
        'conv_ln_b': 0.02 * jax.random.normal(ks[10], (DEPTH, CONV_CH), f32),
        'w_conv_out': nrm(ks[11], (DEPTH, CONV_CH, D_MODEL), CONV_CH),
        'w_mix_out': nrm(ks[12], (DEPTH, D_MODEL, D_MODEL), D_MODEL, 0.5),
        'norm_xattn_g': gain(ks[13], (DEPTH, D_MODEL)),
        'norm_mem_g': gain(ks[14], (DEPTH, D_MODEL)),
        'w_xq': nrm(ks[15], (DEPTH, D_MODEL, XATTN_WIDTH), D_MODEL),
        'w_xkv': nrm(ks[16], (DEPTH, D_MODEL, 2 * XATTN_WIDTH), D_MODEL),
        'w_xo': nrm(ks[17], (DEPTH, XATTN_WIDTH, D_MODEL), XATTN_WIDTH, 0.5),
        'norm_ffn_g': gain(ks[18], (DEPTH, D_MODEL)),
        'w_router': nrm(ks[19], (D_MODEL, N_EXPERTS), D_MODEL),
        'b_router': 0.01 * jax.random.normal(ks[20], (N_EXPERTS,), f32),
        'w_exp_gate': nrm(ks[21], (DEPTH, N_EXPERTS, D_MODEL, D_EXPERT), D_MODEL),
        'w_exp_up': nrm(ks[22], (DEPTH, N_EXPERTS, D_MODEL, D_EXPERT), D_MODEL),
        'w_exp_down': nrm(ks[23], (DEPTH, N_EXPERTS, D_EXPERT, D_MODEL), D_EXPERT, 0.5),
        'norm_final_g': gain(ks[24], (D_MODEL,)),
    }


def reference(x, mem, positions, norm_mix_g, w_in, b_branch_gate, ret_norm_g, w_ret_out,
              conv_w, conv_b, conv_ln_g, conv_ln_b, w_conv_out, w_mix_out,
              norm_xattn_g, norm_mem_g, w_xq, w_xkv, w_xo, norm_ffn_g,
              w_router, b_router, w_exp_gate, w_exp_up, w_exp_down, norm_final_g):
    b, s, _ = x.shape
    o_q, o_k, o_v, o_g = RET_WIDTH, 2 * RET_WIDTH, 3 * RET_WIDTH, 4 * RET_WIDTH
    o_c = o_g + 2 * CONV_CH
    for l in range(DEPTH):
        h = _rmsnorm(x, norm_mix_g[l])
        u = h @ w_in[l]
        q = _rotary(u[..., :o_q].reshape(b, s, RET_HEADS, RET_HEAD_DIM), positions)
        k = _rotary(u[..., o_q:o_k].reshape(b, s, RET_HEADS, RET_HEAD_DIM), positions)
        v = u[..., o_k:o_v].reshape(b, s, RET_HEADS, RET_HEAD_DIM)
        r = _retention(q, k, v)
        r = r * lax.rsqrt(jnp.mean(r * r, axis=-1, keepdims=True) + EPS)
        r = r.reshape(b, s, RET_WIDTH) * ret_norm_g[l].astype(jnp.float32)
        r = (jax.nn.silu(u[..., o_v:o_g].astype(jnp.float32)) * r).astype(x.dtype)
        y_ret = r @ w_ret_out[l]
        y_conv = _conv_module(u[..., o_g:o_c], conv_w[l], conv_b[l], conv_ln_g[l], conv_ln_b[l],
                              w_conv_out[l])
        gate = jax.nn.sigmoid(u[..., o_c:] + b_branch_gate[l])
        g_ret, g_conv = gate[..., :D_MODEL], gate[..., D_MODEL:]
        x = x + (g_ret * y_ret + g_conv * y_conv) @ w_mix_out[l]
        x = x + _cross_attention(_rmsnorm(x, norm_xattn_g[l]), _rmsnorm(mem, norm_mem_g[l]),
                                 w_xq[l], w_xkv[l], w_xo[l])
        x = x + _grouped_moe(_rmsnorm(x, norm_ffn_g[l]), w_router, b_router,
                             w_exp_gate[l], w_exp_up[l], w_exp_down[l])
    return _rmsnorm(x, norm_final_g)
```

```python
import functools

import jax
import jax.numpy as jnp
import numpy as np
from jax import lax
from jax.experimental import pallas as pl
from jax.experimental.pallas import tpu as pltpu

F32 = jnp.float32
BF16 = jnp.bfloat16

RET_HEADS = 4
HEAD_DIM = 128
RET_WIDTH = RET_HEADS * HEAD_DIM
RET_CHUNK = 128
CONV_CH = 512
CONV_WIDTH = 31
CONV_HALO = 32
XATTN_HEADS = 4
XATTN_WIDTH = XATTN_HEADS * HEAD_DIM
N_EXPERTS = 16
N_GROUPS = 4
EXPERTS_PER_GROUP = 4
ROPE_BASE = 10000.0
EPS = 1e-6
QK_SCALE = HEAD_DIM ** -0.5

PAIRS = ((0, 1), (0, 2), (0, 3), (1, 2), (1, 3), (2, 3))
N_CLASSES = N_GROUPS * len(PAIRS)
CLASS_ROWS = 32
META_LANES = 128

SEQ_TILE = 256
MOE_TILE = 256
COMBINE_TILE = 256
VMEM_LIMIT = 56 * 1024 * 1024


def _rmsnorm(x, g):
    return x * lax.rsqrt(jnp.mean(x * x, axis=-1, keepdims=True) + EPS) * g


def _sigmoid(x):
    return 1.0 / (1.0 + jnp.exp(-x))


def _dot(a, b):
    return jnp.dot(a, b, preferred_element_type=F32)


def _dot_nt(a, b):
    return lax.dot_general(a, b, (((1,), (1,)), ((), ())), preferred_element_type=F32)


def _dot_tn(a, b):
    return lax.dot_general(a, b, (((0,), (0,)), ((), ())), preferred_element_type=F32)


def _const_spec(shape):
    n = len(shape)
    return pl.BlockSpec(shape, lambda *_: (0,) * n)


def _rope_kernel(pos_ref, invf_ref, cos_ref, sin_ref):
    ang = pos_ref[...] * invf_ref[...]
    lane = lax.broadcasted_iota(jnp.int32, ang.shape, 1)
    s = jnp.sin(ang)
    cos_ref[...] = jnp.cos(ang)
    sin_ref[...] = jnp.where(lane < HEAD_DIM // 2, -s, s)


def _rope_tables(positions):
    s = positions.shape[0]
    half = HEAD_DIM // 2
    inv_freq = ROPE_BASE ** (-jnp.arange(half, dtype=F32) / half)
    invf = jnp.concatenate([inv_freq, inv_freq])[None, :]
    pos = positions.astype(F32)[:, None]
    ts = min(s, 1024)
    return pl.pallas_call(
        _rope_kernel,
        name="rope_tables",
        grid=(s // ts,),
        in_specs=[pl.BlockSpec((ts, 1), lambda i: (i, 0)), _const_spec((1, HEAD_DIM))],
        out_specs=[pl.BlockSpec((ts, HEAD_DIM), lambda i: (i, 0))] * 2,
        out_shape=[jax.ShapeDtypeStruct((s, HEAD_DIM), F32)] * 2,
    )(pos, invf)


def _memkv_kernel(mem_ref, g_ref, w_ref, kv_ref):
    m = _rmsnorm(mem_ref[0], g_ref[...]).astype(BF16)
    kv_ref[0] = _dot(m, w_ref[...]).astype(BF16)


def _mem_kv(mem, g, w_xkv):
    b, m, d = mem.shape
    return pl.pallas_call(
        _memkv_kernel,
        name="mem_kv",
        grid=(b,),
        in_specs=[pl.BlockSpec((1, m, d), lambda i: (i, 0, 0)), _const_spec((1, d)),
                  _const_spec((d, 2 * XATTN_WIDTH))],
        out_specs=pl.BlockSpec((1, m, 2 * XATTN_WIDTH), lambda i: (i, 0, 0)),
        out_shape=jax.ShapeDtypeStruct((b, m, 2 * XATTN_WIDTH), BF16),
        compiler_params=pltpu.CompilerParams(vmem_limit_bytes=VMEM_LIMIT),
    )(mem, g[None, :], w_xkv)


def _retention_tables():
    h = np.arange(RET_HEADS, dtype=np.float64)
    log_gamma = np.log(1.0 - 2.0 ** (-5.0 - h))
    idx = np.arange(RET_CHUNK, dtype=np.float64)
    diff = idx[:, None] - idx[None, :]
    decay = np.where(diff[None] >= 0.0, np.exp(np.maximum(diff, 0.0)[None] * log_gamma[:, None, None]), 0.0)
    zeta = np.exp((RET_CHUNK - 1.0 - idx)[None, :] * log_gamma[:, None])
    xi = np.exp((idx + 1.0)[None, :] * log_gamma[:, None])
    chunk_decay = np.exp(RET_CHUNK * log_gamma)
    bc = lambda t: np.broadcast_to(t[:, :, None], (RET_HEADS, RET_CHUNK, HEAD_DIM))
    return (jnp.asarray(decay, F32), jnp.asarray(bc(zeta), F32), jnp.asarray(bc(xi), F32),
            tuple(float(c) for c in chunk_decay))


def _mixer_kernel(chunk_decay, x_ref, cos_ref, sin_ref, g_ref, win_ref, bg_ref, rg_ref, wro_ref,
                  cw_ref, cb_ref, lng_ref, lnb_ref, wco_ref, wmo_ref, decay_ref, zeta_ref, xi_ref,
                  o_ref, state_ref, zext_ref):
    ts = x_ref.shape[1]
    d = x_ref.shape[2]

    @pl.when(pl.program_id(1) == 0)
    def _():
        state_ref[...] = jnp.zeros_like(state_ref)
        zext_ref[0:CONV_HALO, :] = jnp.zeros((CONV_HALO, CONV_CH), F32)

    x = x_ref[0]
    h = _rmsnorm(x, g_ref[...]).astype(BF16)

    def proj(a, b):
        return _dot(h, win_ref[:, a:b])

    o_q, o_k, o_v, o_g = RET_WIDTH, 2 * RET_WIDTH, 3 * RET_WIDTH, 4 * RET_WIDTH
    uq, uk, uv = proj(0, o_q), proj(o_q, o_k), proj(o_k, o_v)
    cos, sin = cos_ref[...], sin_ref[...]
    heads = []
    for hd in range(RET_HEADS):
        sl = slice(hd * HEAD_DIM, (hd + 1) * HEAD_DIM)
        qh = uq[:, sl]
        qh = qh * cos + pltpu.roll(qh, HEAD_DIM // 2, 1) * sin
        kh = uk[:, sl]
        kh = (kh * cos + pltpu.roll(kh, HEAD_DIM // 2, 1) * sin) * QK_SCALE
        vh = uv[:, sl]
        rows = []
        for c in range(ts // RET_CHUNK):
            r = slice(c * RET_CHUNK, (c + 1) * RET_CHUNK)
            qc = qh[r].astype(BF16)
            kc = kh[r]
            vc = vh[r].astype(BF16)
            sc = _dot_nt(qc, kc.astype(BF16)) * decay_ref[hd]
            inner = _dot(sc.astype(BF16), vc)
            st = state_ref[hd]
            cross = _dot(qc, st.astype(BF16)) * xi_ref[hd]
            kz = (kc * zeta_ref[hd]).astype(BF16)
            state_ref[hd] = chunk_decay[hd] * st + _dot_tn(kz, vc)
            rows.append(inner + cross)
        rh = jnp.concatenate(rows, axis=0)
        heads.append(rh * lax.rsqrt(jnp.mean(rh * rh, axis=-1, keepdims=True) + EPS))
    r = jnp.concatenate(heads, axis=1) * rg_ref[...]
    sg = proj(o_v, o_g)
    r = (sg * _sigmoid(sg)) * r
    y_ret = _dot(r.astype(BF16), wro_ref[...])

    o_c = o_g + 2 * CONV_CH
    ca, cg = proj(o_g, o_g + CONV_CH), proj(o_g + CONV_CH, o_c)
    zext_ref[CONV_HALO:CONV_HALO + ts, :] = ca * _sigmoid(cg)
    acc = jnp.zeros((ts, CONV_CH), F32)
    first = CONV_HALO - (CONV_WIDTH - 1)
    for w in range(CONV_WIDTH):
        acc = acc + zext_ref[first + w:first + w + ts, :] * cw_ref[w:w + 1, :]
    acc = acc + cb_ref[...]
    zext_ref[0:CONV_HALO, :] = zext_ref[ts:ts + CONV_HALO, :]
    mu = jnp.mean(acc, axis=-1, keepdims=True)
    cen = acc - mu
    var = jnp.mean(cen * cen, axis=-1, keepdims=True)
    zf = cen * lax.rsqrt(var + EPS) * lng_ref[...] + lnb_ref[...]
    y_conv = _dot((zf * _sigmoid(zf)).astype(BF16), wco_ref[...])

    g_ret = _sigmoid(proj(o_c, o_c + d) + bg_ref[:, 0:d])
    g_conv = _sigmoid(proj(o_c + d, o_c + 2 * d) + bg_ref[:, d:2 * d])
    merged = (g_ret * y_ret + g_conv * y_conv).astype(BF16)
    o_ref[0] = x + _dot(merged, wmo_ref[...])


def _mixer(x, cos, sin, g, w_in, b_gate, ret_g, w_ret_out, conv_w, conv_b, ln_g, ln_b, w_conv_out, w_mix_out):
    b, s, d = x.shape
    ts = SEQ_TILE
    decay, zeta, xi, chunk_decay = _retention_tables()
    in_cols = w_in.shape[1]
    cw = jnp.zeros((CONV_HALO, CONV_CH), F32).at[:CONV_WIDTH].set(conv_w)
    row = lambda v: v[None, :]
    tab = (RET_HEADS, RET_CHUNK, HEAD_DIM)
    return pl.pallas_call(
        functools.partial(_mixer_kernel, chunk_decay),
        name="mixer",
        grid=(b, s // ts),
        in_specs=[
            pl.BlockSpec((1, ts, d), lambda i, j: (i, j, 0)),
            pl.BlockSpec((ts, HEAD_DIM), lambda i, j: (j, 0)),
            pl.BlockSpec((ts, HEAD_DIM), lambda i, j: (j, 0)),
            _const_spec((1, d)), _const_spec((d, in_cols)), _const_spec((1, 2 * d)),
            _const_spec((1, RET_WIDTH)), _const_spec((RET_WIDTH, d)),
            _const_spec((CONV_HALO, CONV_CH)), _const_spec((1, CONV_CH)), _const_spec((1, CONV_CH)),
            _const_spec((1, CONV_CH)), _const_spec((CONV_CH, d)), _const_spec((d, d)),
            _const_spec((RET_HEADS, RET_CHUNK, RET_CHUNK)), _const_spec(tab), _const_spec(tab),
        ],
        out_specs=pl.BlockSpec((1, ts, d), lambda i, j: (i, j, 0)),
        out_shape=jax.ShapeDtypeStruct((b, s, d), F32),
        scratch_shapes=[pltpu.VMEM((RET_HEADS, HEAD_DIM, HEAD_DIM), F32),
                        pltpu.VMEM((CONV_HALO + ts, CONV_CH), F32)],
        compiler_params=pltpu.CompilerParams(dimension_semantics=("arbitrary", "arbitrary"),
                                             vmem_limit_bytes=VMEM_LIMIT),
    )(x, cos, sin, row(g), w_in.astype(BF16), row(b_gate), row(ret_g), w_ret_out.astype(BF16), cw,
      row(conv_b), row(ln_g), row(ln_b), w_conv_out.astype(BF16), w_mix_out.astype(BF16), decay, zeta, xi)


def _route(logits, bias):
    scores = _sigmoid(logits)
    sel = scores + bias
    one, zero = jnp.float32(1.0), jnp.float32(0.0)
    top, gscore = [], []
    for g in range(N_GROUPS):
        a = [sel[EXPERTS_PER_GROUP * g + i:EXPERTS_PER_GROUP * g + i + 1, :] for i in range(EXPERTS_PER_GROUP)]
        tg, sg = [], None
        for i in range(EXPERTS_PER_GROUP):
            rank = None
            for j in range(EXPERTS_PER_GROUP):
                if j == i:
                    continue
                ahead = (a[j] >= a[i]) if j < i else (a[j] > a[i])
                ahead = jnp.where(ahead, one, zero)
                rank = ahead if rank is None else rank + ahead
            in_top = rank < 2.0
            tg.append(in_top)
            contrib = jnp.where(in_top, a[i], zero)
            sg = contrib if sg is None else sg + contrib
        top.append(tg)
        gscore.append(sg)
    cls = jnp.zeros_like(gscore[0])
    w_lo = jnp.zeros_like(cls)
    w_hi = jnp.zeros_like(cls)
    for g in range(N_GROUPS):
        behind = None
        for g2 in range(N_GROUPS):
            if g2 == g:
                continue
            ahead = (gscore[g2] >= gscore[g]) if g2 < g else (gscore[g2] > gscore[g])
            behind = ahead if behind is None else (behind | ahead)
        best = jnp.logical_not(behind)
        for p, (i, j) in enumerate(PAIRS):
            active = best & top[g][i] & top[g][j]
            e_lo, e_hi = EXPERTS_PER_GROUP * g + i, EXPERTS_PER_GROUP * g + j
            cls = jnp.where(active, jnp.float32(len(PAIRS) * g + p), cls)
            w_lo = jnp.where(active, scores[e_lo:e_lo + 1, :], w_lo)
            w_hi = jnp.where(active, scores[e_hi:e_hi + 1, :], w_hi)
    den = w_lo + w_hi
    return cls, w_lo / den, w_hi / den


def _xattn_route_kernel(x_ref, kv_ref, gx_ref, wq_ref, wo_ref, gf_ref, wr_ref, br_ref,
                        o_ref, hf_ref, meta_ref, cnt_ref, carry_ref):
    ts = x_ref.shape[1]
    d = x_ref.shape[2]
    first = (pl.program_id(0) == 0) & (pl.program_id(1) == 0)

    @pl.when(first)
    def _():
        carry_ref[...] = jnp.zeros_like(carry_ref)

    x = x_ref[0]
    h = _rmsnorm(x, gx_ref[...]).astype(BF16)
    q = _dot(h, wq_ref[...])
    kv = kv_ref[0]
    heads = []
    for hd in range(XATTN_HEADS):
        sl = slice(hd * HEAD_DIM, (hd + 1) * HEAD_DIM)
        vs = slice(XATTN_WIDTH + hd * HEAD_DIM, XATTN_WIDTH + (hd + 1) * HEAD_DIM)
        sc = _dot_nt(q[:, sl].astype(BF16), kv[:, sl]) * QK_SCALE
        sc = sc - jnp.max(sc, axis=-1, keepdims=True)
        p = jnp.exp(sc)
        p = p / jnp.sum(p, axis=-1, keepdims=True)
        heads.append(_dot(p.astype(BF16), kv[:, vs]))
    att = jnp.concatenate(heads, axis=1).astype(BF16)
    x2 = x + _dot(att, wo_ref[...])
    o_ref[0] = x2

    hf = _rmsnorm(x2, gf_ref[...])
    logits = lax.dot_general(wr_ref[...], hf, (((1,), (1,)), ((), ())), preferred_element_type=F32,
                             precision=lax.Precision.HIGHEST)
    cls, w_lo, w_hi = _route(logits, br_ref[...])

    crow = lax.broadcasted_iota(jnp.int32, (CLASS_ROWS, ts), 0).astype(F32)
    onehot = jnp.where(crow == cls, jnp.float32(1.0), jnp.float32(0.0))
    before = lax.broadcasted_iota(jnp.int32, (ts, ts), 0) < lax.broadcasted_iota(jnp.int32, (ts, ts), 1)
    tri = jnp.where(before, jnp.float32(1.0), jnp.float32(0.0)).astype(BF16)
    prefix = _dot(onehot.astype(BF16), tri)
    carry = carry_ref[...]
    rank = jnp.sum(onehot * (prefix + carry), axis=0, keepdims=True)
    carry = carry + jnp.sum(onehot, axis=1, keepdims=True)
    carry_ref[...] = carry
    cnt_ref[...] = jnp.broadcast_to(carry, cnt_ref.shape)

    meta = jnp.concatenate([cls, w_lo, w_hi, rank, jnp.zeros((META_LANES - 4, ts), F32)], axis=0)
    meta_ref[...] = meta[0:8, :]
    hf_ref[:, 0:d] = hf
    hf_ref[:, d:d + META_LANES] = meta.T


def _xattn_route(x, kv, g_x, w_xq, w_xo, g_f, w_router, b_router):
    b, s, d = x.shape
    ts = SEQ_TILE
    m = kv.shape[1]
    nt = s // ts
    row = lambda v: v[None, :]
    return pl.pallas_call(
        _xattn_route_kernel,
        name="xattn_route",
        grid=(b, nt),
        in_specs=[
            pl.BlockSpec((1, ts, d), lambda i, j: (i, j, 0)),
            pl.BlockSpec((1, m, 2 * XATTN_WIDTH), lambda i, j: (i, 0, 0)),
            _const_spec((1, d)), _const_spec((d, XATTN_WIDTH)), _const_spec((XATTN_WIDTH, d)),
            _const_spec((1, d)), _const_spec((N_EXPERTS, d)), _const_spec((N_EXPERTS, 1)),
        ],
        out_specs=[
            pl.BlockSpec((1, ts, d), lambda i, j: (i, j, 0)),
            pl.BlockSpec((ts, d + META_LANES), lambda i, j: (i * nt + j, 0)),
            pl.BlockSpec((8, ts), lambda i, j: (0, i * nt + j)),
            _const_spec((CLASS_ROWS, 128)),
        ],
        out_shape=[
            jax.ShapeDtypeStruct((b, s, d), F32),
            jax.ShapeDtypeStruct((b * s, d + META_LANES), F32),
            jax.ShapeDtypeStruct((8, b * s), F32),
            jax.ShapeDtypeStruct((CLASS_ROWS, 128), F32),
        ],
        scratch_shapes=[pltpu.VMEM((CLASS_ROWS, 1), F32)],
        compiler_params=pltpu.CompilerParams(dimension_semantics=("arbitrary", "arbitrary"),
                                             vmem_limit_bytes=VMEM_LIMIT),
    )(x, kv, row(g_x), w_xq.astype(BF16), w_xo.astype(BF16), row(g_f), w_router.T, b_router[:, None])


def _dispatch_tables(meta, counts, t, tm, n_tiles):
    cls = meta[0].astype(jnp.int32)
    rank = meta[3].astype(jnp.int32)
    cnt = counts[:N_CLASSES, 0].astype(jnp.int32)
    tiles_c = (cnt + tm - 1) // tm
    tiles_end = jnp.cumsum(tiles_c)
    pos = ((tiles_end - tiles_c) * tm)[cls] + rank
    n_used = tiles_end[-1]
    tile_id = jnp.minimum(jnp.arange(n_tiles, dtype=jnp.int32), n_used - 1)
    tile_cls = jnp.minimum(jnp.searchsorted(tiles_end, tile_id, side="right"), N_CLASSES - 1).astype(jnp.int32)
    pair = tile_cls % len(PAIRS)
    base = EXPERTS_PER_GROUP * (tile_cls // len(PAIRS))
    lo_of = jnp.asarray([p[0] for p in PAIRS], jnp.int32)
    hi_of = jnp.asarray([p[1] for p in PAIRS], jnp.int32)
    e_lo = base + lo_of[pair]
    e_hi = base + hi_of[pair]
    src = jnp.zeros((n_tiles * tm,), jnp.int32).at[pos].set(jnp.arange(t, dtype=jnp.int32))
    return pos, src, e_lo.astype(jnp.int32), e_hi.astype(jnp.int32), n_used.reshape(1).astype(jnp.int32)


def _moe_kernel(elo_ref, ehi_ref, nused_ref, src_ref, hf_hbm, wg_lo, wg_hi, wu_lo, wu_hi, wd_lo, wd_hi,
                y_ref, rows_ref, sem):
    tm, d = y_ref.shape
    i = pl.program_id(0)
    used = i < nused_ref[0]

    def row_copy(r):
        return pltpu.make_async_copy(hf_hbm.at[pl.ds(src_ref[i * tm + r], 1)], rows_ref.at[pl.ds(r, 1)], sem.at[0])

    @pl.when(used)
    def _():
        def issue(r, carry):
            row_copy(r).start()
            return carry

        lax.fori_loop(0, tm, issue, 0)

        def drain(r, carry):
            row_copy(r).wait()
            return carry

        lax.fori_loop(0, tm, drain, 0)

        hb = rows_ref[:, 0:d].astype(BF16)
        y = None
        for col, wg, wu, wd in ((d + 1, wg_lo, wu_lo, wd_lo), (d + 2, wg_hi, wu_hi, wd_hi)):
            gate = _dot(hb, wg[0])
            act = (gate * _sigmoid(gate)) * _dot(hb, wu[0])
            part = rows_ref[:, col:col + 1] * _dot(act.astype(BF16), wd[0])
            y = part if y is None else y + part
        y_ref[...] = y

    @pl.when(jnp.logical_not(used))
    def _():
        y_ref[...] = jnp.zeros_like(y_ref)


def _moe(hf_ext, src, e_lo, e_hi, n_used, w_gate, w_up, w_down, n_tiles, tm):
    d = hf_ext.shape[1] - META_LANES
    f = w_gate.shape[2]
    lo3 = lambda i, elo, ehi, nu, sr: (elo[i], 0, 0)
    hi3 = lambda i, elo, ehi, nu, sr: (ehi[i], 0, 0)
    grid_spec = pltpu.PrefetchScalarGridSpec(
        num_scalar_prefetch=4,
        grid=(n_tiles,),
        in_specs=[
            pl.BlockSpec(memory_space=pl.ANY),
            pl.BlockSpec((1, d, f), lo3), pl.BlockSpec((1, d, f), hi3),
            pl.BlockSpec((1, d, f), lo3), pl.BlockSpec((1, d, f), hi3),
            pl.BlockSpec((1, f, d), lo3), pl.BlockSpec((1, f, d), hi3),
        ],
        out_specs=pl.BlockSpec((tm, d), lambda i, elo, ehi, nu, sr: (i, 0)),
        scratch_shapes=[pltpu.VMEM((tm, d + META_LANES), F32), pltpu.SemaphoreType.DMA((1,))],
    )
    return pl.pallas_call(
        _moe_kernel,
        name="moe_experts",
        grid_spec=grid_spec,
        out_shape=jax.ShapeDtypeStruct((n_tiles * tm, d), F32),
        compiler_params=pltpu.CompilerParams(dimension_semantics=("arbitrary",), vmem_limit_bytes=VMEM_LIMIT),
    )(e_lo, e_hi, n_used, src, hf_ext, w_gate, w_gate, w_up, w_up, w_down, w_down)


def _combine_kernel(final_norm, pos_ref, x_ref, y_hbm, g_ref, o_ref, rows_ref, sem):
    tc = x_ref.shape[0]
    i = pl.program_id(0)

    def row_copy(r):
        return pltpu.make_async_copy(y_hbm.at[pl.ds(pos_ref[i * tc + r], 1)], rows_ref.at[pl.ds(r, 1)], sem.at[0])

    def issue(r, carry):
        row_copy(r).start()
        return carry

    lax.fori_loop(0, tc, issue, 0)

    def drain(r, carry):
        row_copy(r).wait()
        return carry

    lax.fori_loop(0, tc, drain, 0)
    x = x_ref[...] + rows_ref[...]
    if final_norm:
        x = _rmsnorm(x, g_ref[...])
    o_ref[...] = x


def _combine(x, y_sorted, pos, g, final_norm):
    t, d = x.shape
    tc = COMBINE_TILE
    grid_spec = pltpu.PrefetchScalarGridSpec(
        num_scalar_prefetch=1,
        grid=(t // tc,),
        in_specs=[pl.BlockSpec((tc, d), lambda i, p: (i, 0)), pl.BlockSpec(memory_space=pl.ANY),
                  pl.BlockSpec((1, d), lambda i, p: (0, 0))],
        out_specs=pl.BlockSpec((tc, d), lambda i, p: (i, 0)),
        scratch_shapes=[pltpu.VMEM((tc, d), F32), pltpu.SemaphoreType.DMA((1,))],
    )
    return pl.pallas_call(
        functools.partial(_combine_kernel, final_norm),
        name="combine",
        grid_spec=grid_spec,
        out_shape=jax.ShapeDtypeStruct((t, d), F32),
        compiler_params=pltpu.CompilerParams(dimension_semantics=("arbitrary",), vmem_limit_bytes=VMEM_LIMIT),
    )(pos, x, y_sorted, g[None, :])


def kernel(x, mem, positions, norm_mix_g, w_in, b_branch_gate, ret_norm_g, w_ret_out, conv_w, conv_b, conv_ln_g, conv_ln_b, w_conv_out, w_mix_out, norm_xattn_g, norm_mem_g, w_xq, w_xkv, w_xo, norm_ffn_g, w_router, b_router, w_exp_gate, w_exp_up, w_exp_down, norm_final_g):
    b, s, d = x.shape
    depth = w_in.shape[0]
    t = b * s
    assert s % SEQ_TILE == 0 and SEQ_TILE % RET_CHUNK == 0 and t % MOE_TILE == 0 and t % COMBINE_TILE == 0
    tm = MOE_TILE
    n_tiles = t // tm + N_CLASSES
    cos, sin = _rope_tables(positions)
    for l in range(depth):
        x = _mixer(x, cos, sin, norm_mix_g[l], w_in[l], b_branch_gate[l], ret_norm_g[l], w_ret_out[l],
                   conv_w[l], conv_b[l], conv_ln_g[l], conv_ln_b[l], w_conv_out[l], w_mix_out[l])
        kv = _mem_kv(mem, norm_mem_g[l], w_xkv[l].astype(BF16))
        x, hf_ext, meta, counts = _xattn_route(x, kv, norm_xattn_g[l], w_xq[l], w_xo[l], norm_ffn_g[l],
                                               w_router, b_router)
        pos, src, e_lo, e_hi, n_used = _dispatch_tables(meta, counts, t, tm, n_tiles)
        y_sorted = _moe(hf_ext, src, e_lo, e_hi, n_used, w_exp_gate[l].astype(BF16), w_exp_up[l].astype(BF16),
                        w_exp_down[l].astype(BF16), n_tiles, tm)
        last = l == depth - 1
        x = _combine(x.reshape(t, d), y_sorted, pos, norm_final_g, last).reshape(b, s, d)
    return x
```

```python
import functools

import jax
import jax.numpy as jnp
import numpy as np
from jax import lax
from jax.experimental import pallas as pl
from jax.experimental.pallas import tpu as pltpu

F32 = jnp.float32
BF16 = jnp.bfloat16

RET_HEADS = 4
HEAD_DIM = 128
RET_WIDTH = RET_HEADS * HEAD_DIM
RET_CHUNK = 128
CONV_CH = 512
CONV_WIDTH = 31
CONV_HALO = 32
XATTN_HEADS = 4
XATTN_WIDTH = XATTN_HEADS * HEAD_DIM
N_EXPERTS = 16
N_GROUPS = 4
EXPERTS_PER_GROUP = 4
ROPE_BASE = 10000.0
EPS = 1e-6
QK_SCALE = HEAD_DIM ** -0.5

PAIRS = ((0, 1), (0, 2), (0, 3), (1, 2), (1, 3), (2, 3))
N_CLASSES = N_GROUPS * len(PAIRS)
CLASS_ROWS = 32
META_LANES = 128

SEQ_TILE = 256
MOE_TILE = 256
COMBINE_TILE = 256
DISPATCH_TILE = 512
VMEM_LIMIT = 56 * 1024 * 1024


def _rmsnorm(x, g):
    return x * lax.rsqrt(jnp.mean(x * x, axis=-1, keepdims=True) + EPS) * g


def _sigmoid(x):
    return 1.0 / (1.0 + jnp.exp(-x))


def _dot(a, b):
    return jnp.dot(a, b, preferred_element_type=F32)


def _dot_nt(a, b):
    return lax.dot_general(a, b, (((1,), (1,)), ((), ())), preferred_element_type=F32)


def _dot_tn(a, b):
    return lax.dot_general(a, b, (((0,), (0,)), ((), ())), preferred_element_type=F32)


def _const_spec(shape):
    n = len(shape)
    return pl.BlockSpec(shape, lambda *_: (0,) * n)


def _rope_kernel(pos_ref, invf_ref, cos_ref, sin_ref):
    ang = pos_ref[...] * invf_ref[...]
    lane = lax.broadcasted_iota(jnp.int32, ang.shape, 1)
    s = jnp.sin(ang)
    cos_ref[...] = jnp.cos(ang)
    sin_ref[...] = jnp.where(lane < HEAD_DIM // 2, -s, s)


def _rope_tables(positions):
    s = positions.shape[0]
    half = HEAD_DIM // 2
    inv_freq = ROPE_BASE ** (-jnp.arange(half, dtype=F32) / half)
    invf = jnp.concatenate([inv_freq, inv_freq])[None, :]
    pos = positions.astype(F32)[:, None]
    ts = min(s, 1024)
    return pl.pallas_call(
        _rope_kernel,
        name="rope_tables",
        grid=(s // ts,),
        in_specs=[pl.BlockSpec((ts, 1), lambda i: (i, 0)), _const_spec((1, HEAD_DIM))],
        out_specs=[pl.BlockSpec((ts, HEAD_DIM), lambda i: (i, 0))] * 2,
        out_shape=[jax.ShapeDtypeStruct((s, HEAD_DIM), F32)] * 2,
    )(pos, invf)


def _memkv_kernel(mem_ref, g_ref, w_ref, kv_ref):
    m = _rmsnorm(mem_ref[0], g_ref[...]).astype(BF16)
    kv_ref[0] = _dot(m, w_ref[...]).astype(BF16)


def _mem_kv(mem, g, w_xkv):
    b, m, d = mem.shape
    return pl.pallas_call(
        _memkv_kernel,
        name="mem_kv",
        grid=(b,),
        in_specs=[pl.BlockSpec((1, m, d), lambda i: (i, 0, 0)), _const_spec((1, d)),
                  _const_spec((d, 2 * XATTN_WIDTH))],
        out_specs=pl.BlockSpec((1, m, 2 * XATTN_WIDTH), lambda i: (i, 0, 0)),
        out_shape=jax.ShapeDtypeStruct((b, m, 2 * XATTN_WIDTH), BF16),
        compiler_params=pltpu.CompilerParams(vmem_limit_bytes=VMEM_LIMIT),
    )(mem, g[None, :], w_xkv)


def _retention_tables():
    h = np.arange(RET_HEADS, dtype=np.float64)
    log_gamma = np.log(1.0 - 2.0 ** (-5.0 - h))
    idx = np.arange(RET_CHUNK, dtype=np.float64)
    diff = idx[:, None] - idx[None, :]
    decay = np.where(diff[None] >= 0.0, np.exp(np.maximum(diff, 0.0)[None] * log_gamma[:, None, None]), 0.0)
    zeta = np.exp((RET_CHUNK - 1.0 - idx)[None, :] * log_gamma[:, None])
    xi = np.exp((idx + 1.0)[None, :] * log_gamma[:, None])
    chunk_decay = np.exp(RET_CHUNK * log_gamma)
    bc = lambda t: np.broadcast_to(t[:, :, None], (RET_HEADS, RET_CHUNK, HEAD_DIM))
    return (jnp.asarray(decay, F32), jnp.asarray(bc(zeta), F32), jnp.asarray(bc(xi), F32),
            tuple(float(c) for c in chunk_decay))


def _mixer_kernel(chunk_decay, x_ref, cos_ref, sin_ref, g_ref, win_ref, bg_ref, rg_ref, wro_ref,
                  cw_ref, cb_ref, lng_ref, lnb_ref, wco_ref, wmo_ref, decay_ref, zeta_ref, xi_ref,
                  o_ref, state_ref, zext_ref):
    ts = x_ref.shape[1]
    d = x_ref.shape[2]

    @pl.when(pl.program_id(1) == 0)
    def _():
        state_ref[...] = jnp.zeros_like(state_ref)
        zext_ref[0:CONV_HALO, :] = jnp.zeros((CONV_HALO, CONV_CH), F32)

    x = x_ref[0]
    h = _rmsnorm(x, g_ref[...]).astype(BF16)

    def proj(a, b):
        return _dot(h, win_ref[:, a:b])

    o_q, o_k, o_v, o_g = RET_WIDTH, 2 * RET_WIDTH, 3 * RET_WIDTH, 4 * RET_WIDTH
    uq, uk, uv = proj(0, o_q), proj(o_q, o_k), proj(o_k, o_v)
    cos, sin = cos_ref[...], sin_ref[...]
    heads = []
    for hd in range(RET_HEADS):
        sl = slice(hd * HEAD_DIM, (hd + 1) * HEAD_DIM)
        qh = uq[:, sl]
        qh = qh * cos + pltpu.roll(qh, HEAD_DIM // 2, 1) * sin
        kh = uk[:, sl]
        kh = (kh * cos + pltpu.roll(kh, HEAD_DIM // 2, 1) * sin) * QK_SCALE
        vh = uv[:, sl]
        rows = []
        for c in range(ts // RET_CHUNK):
            r = slice(c * RET_CHUNK, (c + 1) * RET_CHUNK)
            qc = qh[r].astype(BF16)
            kc = kh[r]
            vc = vh[r].astype(BF16)
            sc = _dot_nt(qc, kc.astype(BF16)) * decay_ref[hd]
            inner = _dot(sc.astype(BF16), vc)
            st = state_ref[hd]
            cross = _dot(qc, st.astype(BF16)) * xi_ref[hd]
            kz = (kc * zeta_ref[hd]).astype(BF16)
            state_ref[hd] = chunk_decay[hd] * st + _dot_tn(kz, vc)
            rows.append(inner + cross)
        rh = jnp.concatenate(rows, axis=0)
        heads.append(rh * lax.rsqrt(jnp.mean(rh * rh, axis=-1, keepdims=True) + EPS))
    r = jnp.concatenate(heads, axis=1) * rg_ref[...]
    sg = proj(o_v, o_g)
    r = (sg * _sigmoid(sg)) * r
    y_ret = _dot(r.astype(BF16), wro_ref[...])

    o_c = o_g + 2 * CONV_CH
    ca, cg = proj(o_g, o_g + CONV_CH), proj(o_g + CONV_CH, o_c)
    zext_ref[CONV_HALO:CONV_HALO + ts, :] = ca * _sigmoid(cg)
    acc = jnp.zeros((ts, CONV_CH), F32)
    first = CONV_HALO - (CONV_WIDTH - 1)
    for w in range(CONV_WIDTH):
        acc = acc + zext_ref[first + w:first + w + ts, :] * cw_ref[w:w + 1, :]
    acc = acc + cb_ref[...]
    zext_ref[0:CONV_HALO, :] = zext_ref[ts:ts + CONV_HALO, :]
    mu = jnp.mean(acc, axis=-1, keepdims=True)
    cen = acc - mu
    var = jnp.mean(cen * cen, axis=-1, keepdims=True)
    zf = cen * lax.rsqrt(var + EPS) * lng_ref[...] + lnb_ref[...]
    y_conv = _dot((zf * _sigmoid(zf)).astype(BF16), wco_ref[...])

    g_ret = _sigmoid(proj(o_c, o_c + d) + bg_ref[:, 0:d])
    g_conv = _sigmoid(proj(o_c + d, o_c + 2 * d) + bg_ref[:, d:2 * d])
    merged = (g_ret * y_ret + g_conv * y_conv).astype(BF16)
    o_ref[0] = x + _dot(merged, wmo_ref[...])


def _mixer(x, cos, sin, g, w_in, b_gate, ret_g, w_ret_out, conv_w, conv_b, ln_g, ln_b, w_conv_out, w_mix_out):
    b, s, d = x.shape
    ts = SEQ_TILE
    decay, zeta, xi, chunk_decay = _retention_tables()
    in_cols = w_in.shape[1]
    cw = jnp.zeros((CONV_HALO, CONV_CH), F32).at[:CONV_WIDTH].set(conv_w)
    row = lambda v: v[None, :]
    tab = (RET_HEADS, RET_CHUNK, HEAD_DIM)
    return pl.pallas_call(
        functools.partial(_mixer_kernel, chunk_decay),
        name="mixer",
        grid=(b, s // ts),
        in_specs=[
            pl.BlockSpec((1, ts, d), lambda i, j: (i, j, 0)),
            pl.BlockSpec((ts, HEAD_DIM), lambda i, j: (j, 0)),
            pl.BlockSpec((ts, HEAD_DIM), lambda i, j: (j, 0)),
            _const_spec((1, d)), _const_spec((d, in_cols)), _const_spec((1, 2 * d)),
            _const_spec((1, RET_WIDTH)), _const_spec((RET_WIDTH, d)),
            _const_spec((CONV_HALO, CONV_CH)), _const_spec((1, CONV_CH)), _const_spec((1, CONV_CH)),
            _const_spec((1, CONV_CH)), _const_spec((CONV_CH, d)), _const_spec((d, d)),
            _const_spec((RET_HEADS, RET_CHUNK, RET_CHUNK)), _const_spec(tab), _const_spec(tab),
        ],
        out_specs=pl.BlockSpec((1, ts, d), lambda i, j: (i, j, 0)),
        out_shape=jax.ShapeDtypeStruct((b, s, d), F32),
        scratch_shapes=[pltpu.VMEM((RET_HEADS, HEAD_DIM, HEAD_DIM), F32),
                        pltpu.VMEM((CONV_HALO + ts, CONV_CH), F32)],
        compiler_params=pltpu.CompilerParams(dimension_semantics=("arbitrary", "arbitrary"),
                                             vmem_limit_bytes=VMEM_LIMIT),
    )(x, cos, sin, row(g), w_in.astype(BF16), row(b_gate), row(ret_g), w_ret_out.astype(BF16), cw,
      row(conv_b), row(ln_g), row(ln_b), w_conv_out.astype(BF16), w_mix_out.astype(BF16), decay, zeta, xi)


def _route(logits, bias):
    scores = _sigmoid(logits)
    sel = scores + bias
    one, zero = jnp.float32(1.0), jnp.float32(0.0)
    top, gscore = [], []
    for g in range(N_GROUPS):
        a = [sel[EXPERTS_PER_GROUP * g + i:EXPERTS_PER_GROUP * g + i + 1, :] for i in range(EXPERTS_PER_GROUP)]
        tg, sg = [], None
        for i in range(EXPERTS_PER_GROUP):
            rank = None
            for j in range(EXPERTS_PER_GROUP):
                if j == i:
                    continue
                ahead = (a[j] >= a[i]) if j < i else (a[j] > a[i])
                ahead = jnp.where(ahead, one, zero)
                rank = ahead if rank is None else rank + ahead
            in_top = rank < 2.0
            tg.append(in_top)
            contrib = jnp.where(in_top, a[i], zero)
            sg = contrib if sg is None else sg + contrib
        top.append(tg)
        gscore.append(sg)
    cls = jnp.zeros_like(gscore[0])
    w_lo = jnp.zeros_like(cls)
    w_hi = jnp.zeros_like(cls)
    for g in range(N_GROUPS):
        behind = None
        for g2 in range(N_GROUPS):
            if g2 == g:
                continue
            ahead = (gscore[g2] >= gscore[g]) if g2 < g else (gscore[g2] > gscore[g])
            behind = ahead if behind is None else (behind | ahead)
        best = jnp.logical_not(behind)
        for p, (i, j) in enumerate(PAIRS):
            active = best & top[g][i] & top[g][j]
            e_lo, e_hi = EXPERTS_PER_GROUP * g + i, EXPERTS_PER_GROUP * g + j
            cls = jnp.where(active, jnp.float32(len(PAIRS) * g + p), cls)
            w_lo = jnp.where(active, scores[e_lo:e_lo + 1, :], w_lo)
            w_hi = jnp.where(active, scores[e_hi:e_hi + 1, :], w_hi)
    den = w_lo + w_hi
    return cls, w_lo / den, w_hi / den


def _xattn_route_kernel(x_ref, kv_ref, gx_ref, wq_ref, wo_ref, gf_ref, wr_ref, br_ref,
                        o_ref, hf_ref, meta_ref, cnt_ref, carry_ref):
    ts = x_ref.shape[1]
    d = x_ref.shape[2]
    first = (pl.program_id(0) == 0) & (pl.program_id(1) == 0)

    @pl.when(first)
    def _():
        carry_ref[...] = jnp.zeros_like(carry_ref)

    x = x_ref[0]
    h = _rmsnorm(x, gx_ref[...]).astype(BF16)
    q = _dot(h, wq_ref[...])
    kv = kv_ref[0]
    heads = []
    for hd in range(XATTN_HEADS):
        sl = slice(hd * HEAD_DIM, (hd + 1) * HEAD_DIM)
        vs = slice(XATTN_WIDTH + hd * HEAD_DIM, XATTN_WIDTH + (hd + 1) * HEAD_DIM)
        sc = _dot_nt(q[:, sl].astype(BF16), kv[:, sl]) * QK_SCALE
        sc = sc - jnp.max(sc, axis=-1, keepdims=True)
        p = jnp.exp(sc)
        p = p / jnp.sum(p, axis=-1, keepdims=True)
        heads.append(_dot(p.astype(BF16), kv[:, vs]))
    att = jnp.concatenate(heads, axis=1).astype(BF16)
    x2 = x + _dot(att, wo_ref[...])
    o_ref[0] = x2

    hf = _rmsnorm(x2, gf_ref[...])
    logits = lax.dot_general(wr_ref[...], hf, (((1,), (1,)), ((), ())), preferred_element_type=F32,
                             precision=lax.Precision.HIGHEST)
    cls, w_lo, w_hi = _route(logits, br_ref[...])

    crow = lax.broadcasted_iota(jnp.int32, (CLASS_ROWS, ts), 0).astype(F32)
    onehot = jnp.where(crow == cls, jnp.float32(1.0), jnp.float32(0.0))
    before = lax.broadcasted_iota(jnp.int32, (ts, ts), 0) < lax.broadcasted_iota(jnp.int32, (ts, ts), 1)
    tri = jnp.where(before, jnp.float32(1.0), jnp.float32(0.0)).astype(BF16)
    prefix = _dot(onehot.astype(BF16), tri)
    carry = carry_ref[...]
    rank = jnp.sum(onehot * (prefix + carry), axis=0, keepdims=True)
    carry = carry + jnp.sum(onehot, axis=1, keepdims=True)
    carry_ref[...] = carry
    cnt_ref[...] = jnp.broadcast_to(carry, cnt_ref.shape)

    meta = jnp.concatenate([cls, w_lo, w_hi, rank, jnp.zeros((META_LANES - 4, ts), F32)], axis=0)
    meta_ref[...] = meta[0:8, :]
    hf_ref[:, 0:d] = hf
    hf_ref[:, d:d + META_LANES] = meta.T


def _xattn_route(x, kv, g_x, w_xq, w_xo, g_f, w_router, b_router):
    b, s, d = x.shape
    ts = SEQ_TILE
    m = kv.shape[1]
    nt = s // ts
    row = lambda v: v[None, :]
    return pl.pallas_call(
        _xattn_route_kernel,
        name="xattn_route",
        grid=(b, nt),
        in_specs=[
            pl.BlockSpec((1, ts, d), lambda i, j: (i, j, 0)),
            pl.BlockSpec((1, m, 2 * XATTN_WIDTH), lambda i, j: (i, 0, 0)),
            _const_spec((1, d)), _const_spec((d, XATTN_WIDTH)), _const_spec((XATTN_WIDTH, d)),
            _const_spec((1, d)), _const_spec((N_EXPERTS, d)), _const_spec((N_EXPERTS, 1)),
        ],
        out_specs=[
            pl.BlockSpec((1, ts, d), lambda i, j: (i, j, 0)),
            pl.BlockSpec((ts, d + META_LANES), lambda i, j: (i * nt + j, 0)),
            pl.BlockSpec((8, ts), lambda i, j: (0, i * nt + j)),
            _const_spec((CLASS_ROWS, 128)),
        ],
        out_shape=[
            jax.ShapeDtypeStruct((b, s, d), F32),
            jax.ShapeDtypeStruct((b * s, d + META_LANES), F32),
            jax.ShapeDtypeStruct((8, b * s), F32),
            jax.ShapeDtypeStruct((CLASS_ROWS, 128), F32),
        ],
        scratch_shapes=[pltpu.VMEM((CLASS_ROWS, 1), F32)],
        compiler_params=pltpu.CompilerParams(dimension_semantics=("arbitrary", "arbitrary"),
                                             vmem_limit_bytes=VMEM_LIMIT),
    )(x, kv, row(g_x), w_xq.astype(BF16), w_xo.astype(BF16), row(g_f), w_router.T, b_router[:, None])


def _dispatch_tables(meta, counts, t, tm, n_tiles):
    cls = meta[0].astype(jnp.int32)
    rank = meta[3].astype(jnp.int32)
    cnt = counts[:N_CLASSES, 0].astype(jnp.int32)
    tiles_c = (cnt + tm - 1) // tm
    tiles_end = jnp.cumsum(tiles_c)
    pos = ((tiles_end - tiles_c) * tm)[cls] + rank
    n_used = tiles_end[-1]
    tile_id = jnp.minimum(jnp.arange(n_tiles, dtype=jnp.int32), n_used - 1)
    tile_cls = jnp.sum((tile_id[:, None] >= tiles_end[None, :]).astype(jnp.int32), axis=1)
    tile_cls = jnp.minimum(tile_cls, N_CLASSES - 1)
    pair = tile_cls % len(PAIRS)
    base = EXPERTS_PER_GROUP * (tile_cls // len(PAIRS))
    lo_of = jnp.asarray([p[0] for p in PAIRS], jnp.int32)
    hi_of = jnp.asarray([p[1] for p in PAIRS], jnp.int32)
    e_lo = base + lo_of[pair]
    e_hi = base + hi_of[pair]
    return pos, e_lo.astype(jnp.int32), e_hi.astype(jnp.int32), n_used.reshape(1).astype(jnp.int32)


def _dispatch_kernel(pos_ref, hf_ref, init_hbm, hs_hbm, sem):
    del init_hbm
    td = hf_ref.shape[0]
    base = pl.program_id(0) * td
    for r in range(td):
        pltpu.make_async_copy(hf_ref.at[pl.ds(r, 1)], hs_hbm.at[pl.ds(pos_ref[base + r], 1)], sem.at[0]).start()
    pltpu.make_async_copy(hf_ref, hs_hbm.at[pl.ds(0, td)], sem.at[0]).wait()


def _dispatch(hf_ext, pos, n_rows):
    t, w = hf_ext.shape
    td = DISPATCH_TILE
    grid_spec = pltpu.PrefetchScalarGridSpec(
        num_scalar_prefetch=1,
        grid=(t // td,),
        in_specs=[pl.BlockSpec((td, w), lambda i, p: (i, 0)), pl.BlockSpec(memory_space=pl.ANY)],
        out_specs=pl.BlockSpec(memory_space=pl.ANY),
        scratch_shapes=[pltpu.SemaphoreType.DMA((1,))],
    )
    return pl.pallas_call(
        _dispatch_kernel,
        name="dispatch",
        grid_spec=grid_spec,
        out_shape=jax.ShapeDtypeStruct((n_rows, w), F32),
        input_output_aliases={2: 0},
        compiler_params=pltpu.CompilerParams(dimension_semantics=("arbitrary",), vmem_limit_bytes=VMEM_LIMIT),
    )(pos, hf_ext, jnp.zeros((n_rows, w), F32))


def _moe_kernel(elo_ref, ehi_ref, nused_ref, rows_ref, wg_lo, wg_hi, wu_lo, wu_hi, wd_lo, wd_hi, y_ref):
    tm, d = y_ref.shape
    i = pl.program_id(0)
    used = i < nused_ref[0]

    @pl.when(used)
    def _():
        hb = rows_ref[:, 0:d].astype(BF16)
        y = None
        for col, wg, wu, wd in ((d + 1, wg_lo, wu_lo, wd_lo), (d + 2, wg_hi, wu_hi, wd_hi)):
            gate = _dot(hb, wg[0])
            act = (gate * _sigmoid(gate)) * _dot(hb, wu[0])
            part = rows_ref[:, col:col + 1] * _dot(act.astype(BF16), wd[0])
            y = part if y is None else y + part
        y_ref[...] = y

    @pl.when(jnp.logical_not(used))
    def _():
        y_ref[...] = jnp.zeros_like(y_ref)


def _moe(hs, e_lo, e_hi, n_used, w_gate, w_up, w_down, n_tiles, tm):
    w = hs.shape[1]
    d = w - META_LANES
    f = w_gate.shape[2]
    lo3 = lambda i, elo, ehi, nu: (elo[i], 0, 0)
    hi3 = lambda i, elo, ehi, nu: (ehi[i], 0, 0)
    grid_spec = pltpu.PrefetchScalarGridSpec(
        num_scalar_prefetch=3,
        grid=(n_tiles,),
        in_specs=[
            pl.BlockSpec((tm, w), lambda i, elo, ehi, nu: (jnp.minimum(i, nu[0] - 1), 0)),
            pl.BlockSpec((1, d, f), lo3), pl.BlockSpec((1, d, f), hi3),
            pl.BlockSpec((1, d, f), lo3), pl.BlockSpec((1, d, f), hi3),
            pl.BlockSpec((1, f, d), lo3), pl.BlockSpec((1, f, d), hi3),
        ],
        out_specs=pl.BlockSpec((tm, d), lambda i, elo, ehi, nu: (i, 0)),
    )
    return pl.pallas_call(
        _moe_kernel,
        name="moe_experts",
        grid_spec=grid_spec,
        out_shape=jax.ShapeDtypeStruct((n_tiles * tm, d), F32),
        compiler_params=pltpu.CompilerParams(dimension_semantics=("arbitrary",), vmem_limit_bytes=VMEM_LIMIT),
    )(e_lo, e_hi, n_used, hs, w_gate, w_gate, w_up, w_up, w_down, w_down)


def _combine_kernel(final_norm, pos_ref, x_ref, y_hbm, g_ref, o_ref, rows_ref, sem):
    tc = x_ref.shape[0]
    i = pl.program_id(0)
    n = pl.num_programs(0)

    def issue(tile, slot):
        base = tile * tc
        for r in range(tc):
            pltpu.make_async_copy(y_hbm.at[pl.ds(pos_ref[base + r], 1)], rows_ref.at[slot, pl.ds(r, 1)],
                                  sem.at[slot]).start()

    @pl.when(i == 0)
    def _():
        issue(0, 0)

    for slot in range(2):
        @pl.when(i % 2 == slot)
        def _():
            @pl.when(i + 1 < n)
            def _():
                issue(i + 1, 1 - slot)

            pltpu.make_async_copy(y_hbm.at[pl.ds(0, tc)], rows_ref.at[slot], sem.at[slot]).wait()
            x = x_ref[...] + rows_ref[slot]
            if final_norm:
                x = _rmsnorm(x, g_ref[...])
            o_ref[...] = x


def _combine(x, y_sorted, pos, g, final_norm):
    t, d = x.shape
    tc = COMBINE_TILE
    grid_spec = pltpu.PrefetchScalarGridSpec(
        num_scalar_prefetch=1,
        grid=(t // tc,),
        in_specs=[pl.BlockSpec((tc, d), lambda i, p: (i, 0)), pl.BlockSpec(memory_space=pl.ANY),
                  pl.BlockSpec((1, d), lambda i, p: (0, 0))],
        out_specs=pl.BlockSpec((tc, d), lambda i, p: (i, 0)),
        scratch_shapes=[pltpu.VMEM((2, tc, d), F32), pltpu.SemaphoreType.DMA((2,))],
    )
    return pl.pallas_call(
        functools.partial(_combine_kernel, final_norm),
        name="combine",
        grid_spec=grid_spec,
        out_shape=jax.ShapeDtypeStruct((t, d), F32),
        compiler_params=pltpu.CompilerParams(dimension_semantics=("arbitrary",), vmem_limit_bytes=VMEM_LIMIT),
    )(pos, x, y_sorted, g[None, :])


def kernel(x, mem, positions, norm_mix_g, w_in, b_branch_gate, ret_norm_g, w_ret_out, conv_w, conv_b, conv_ln_g, conv_ln_b, w_conv_out, w_mix_out, norm_xattn_g, norm_mem_g, w_xq, w_xkv, w_xo, norm_ffn_g, w_router, b_router, w_exp_gate, w_exp_up, w_exp_down, norm_final_g):
    b, s, d = x.shape
    depth = w_in.shape[0]
    t = b * s
    assert s % SEQ_TILE == 0 and SEQ_TILE % RET_CHUNK == 0
    assert t % MOE_TILE == 0 and t % COMBINE_TILE == 0 and t % DISPATCH_TILE == 0
    tm = MOE_TILE
    n_tiles = t // tm + N_CLASSES
    cos, sin = _rope_tables(positions)
    for l in range(depth):
        x = _mixer(x, cos, sin, norm_mix_g[l], w_in[l], b_branch_gate[l], ret_norm_g[l], w_ret_out[l],
                   conv_w[l], conv_b[l], conv_ln_g[l], conv_ln_b[l], w_conv_out[l], w_mix_out[l])
        kv = _mem_kv(mem, norm_mem_g[l], w_xkv[l].astype(BF16))
        x, hf_ext, meta, counts = _xattn_route(x, kv, norm_xattn_g[l], w_xq[l], w_xo[l], norm_ffn_g[l],
                                               w_router, b_router)
        pos, e_lo, e_hi, n_used = _dispatch_tables(meta, counts, t, tm, n_tiles)
        hs = _dispatch(hf_ext, pos, n_tiles * tm)
        y_sorted = _moe(hs, e_lo, e_hi, n_used, w_exp_gate[l].astype(BF16), w_exp_up[l].astype(BF16),
                        w_exp_down[l].astype(BF16), n_tiles, tm)
        last = l == depth - 1
        x = _combine(x.reshape(t, d), y_sorted, pos, norm_final_g, last).reshape(b, s, d)
    return x
```

```python
import functools

import jax
import jax.numpy as jnp
import numpy as np
from jax import lax
from jax.experimental import pallas as pl
from jax.experimental.pallas import tpu as pltpu

F32 = jnp.float32
BF16 = jnp.bfloat16

RET_HEADS = 4
HEAD_DIM = 128
RET_WIDTH = RET_HEADS * HEAD_DIM
RET_CHUNK = 128
CONV_CH = 512
CONV_WIDTH = 31
CONV_HALO = 32
XATTN_HEADS = 4
XATTN_WIDTH = XATTN_HEADS * HEAD_DIM
N_EXPERTS = 16
N_GROUPS = 4
EXPERTS_PER_GROUP = 4
ROPE_BASE = 10000.0
EPS = 1e-6
QK_SCALE = HEAD_DIM ** -0.5

PAIRS = ((0, 1), (0, 2), (0, 3), (1, 2), (1, 3), (2, 3))
N_CLASSES = N_GROUPS * len(PAIRS)
CLASS_ROWS = 32
META_LANES = 128

SEQ_TILE = 256
MOE_TILE = 256
COMBINE_TILE = 256
DISPATCH_TILE = 512
VMEM_LIMIT = 56 * 1024 * 1024


def _rmsnorm(x, g):
    return x * lax.rsqrt(jnp.mean(x * x, axis=-1, keepdims=True) + EPS) * g


def _sigmoid(x):
    return 1.0 / (1.0 + jnp.exp(-x))


def _dot(a, b):
    return jnp.dot(a, b, preferred_element_type=F32)


def _dot_nt(a, b):
    return lax.dot_general(a, b, (((1,), (1,)), ((), ())), preferred_element_type=F32)


def _dot_tn(a, b):
    return lax.dot_general(a, b, (((0,), (0,)), ((), ())), preferred_element_type=F32)


def _const_spec(shape):
    n = len(shape)
    return pl.BlockSpec(shape, lambda *_: (0,) * n)


def _rope_kernel(pos_ref, invf_ref, cos_ref, sin_ref):
    ang = pos_ref[...] * invf_ref[...]
    lane = lax.broadcasted_iota(jnp.int32, ang.shape, 1)
    s = jnp.sin(ang)
    cos_ref[...] = jnp.cos(ang)
    sin_ref[...] = jnp.where(lane < HEAD_DIM // 2, -s, s)


def _rope_tables(positions):
    s = positions.shape[0]
    half = HEAD_DIM // 2
    inv_freq = ROPE_BASE ** (-jnp.arange(half, dtype=F32) / half)
    invf = jnp.concatenate([inv_freq, inv_freq])[None, :]
    pos = positions.astype(F32)[:, None]
    ts = min(s, 1024)
    return pl.pallas_call(
        _rope_kernel,
        name="rope_tables",
        grid=(s // ts,),
        in_specs=[pl.BlockSpec((ts, 1), lambda i: (i, 0)), _const_spec((1, HEAD_DIM))],
        out_specs=[pl.BlockSpec((ts, HEAD_DIM), lambda i: (i, 0))] * 2,
        out_shape=[jax.ShapeDtypeStruct((s, HEAD_DIM), F32)] * 2,
    )(pos, invf)


def _memkv_kernel(mem_ref, g_ref, w_ref, kv_ref):
    m = _rmsnorm(mem_ref[0], g_ref[...]).astype(BF16)
    kv_ref[0] = _dot(m, w_ref[...]).astype(BF16)


def _mem_kv(mem, g, w_xkv):
    b, m, d = mem.shape
    return pl.pallas_call(
        _memkv_kernel,
        name="mem_kv",
        grid=(b,),
        in_specs=[pl.BlockSpec((1, m, d), lambda i: (i, 0, 0)), _const_spec((1, d)),
                  _const_spec((d, 2 * XATTN_WIDTH))],
        out_specs=pl.BlockSpec((1, m, 2 * XATTN_WIDTH), lambda i: (i, 0, 0)),
        out_shape=jax.ShapeDtypeStruct((b, m, 2 * XATTN_WIDTH), BF16),
        compiler_params=pltpu.CompilerParams(vmem_limit_bytes=VMEM_LIMIT),
    )(mem, g[None, :], w_xkv)


def _retention_tables():
    h = np.arange(RET_HEADS, dtype=np.float64)
    log_gamma = np.log(1.0 - 2.0 ** (-5.0 - h))
    idx = np.arange(RET_CHUNK, dtype=np.float64)
    diff = idx[:, None] - idx[None, :]
    decay = np.where(diff[None] >= 0.0, np.exp(np.maximum(diff, 0.0)[None] * log_gamma[:, None, None]), 0.0)
    zeta = np.exp((RET_CHUNK - 1.0 - idx)[None, :] * log_gamma[:, None])
    xi = np.exp((idx + 1.0)[None, :] * log_gamma[:, None])
    chunk_decay = np.exp(RET_CHUNK * log_gamma)
    bc = lambda t: np.broadcast_to(t[:, :, None], (RET_HEADS, RET_CHUNK, HEAD_DIM))
    return (jnp.asarray(decay, F32), jnp.asarray(bc(zeta), F32), jnp.asarray(bc(xi), F32),
            tuple(float(c) for c in chunk_decay))


def _mixer_kernel(chunk_decay, x_ref, cos_ref, sin_ref, g_ref, win_ref, bg_ref, rg_ref, wro_ref,
                  cw_ref, cb_ref, lng_ref, lnb_ref, wco_ref, wmo_ref, decay_ref, zeta_ref, xi_ref,
                  o_ref, state_ref, zext_ref, phase_ref, u_ref):
    ts = x_ref.shape[1]
    d = x_ref.shape[2]

    @pl.when(pl.program_id(1) == 0)
    def _():
        state_ref[...] = jnp.zeros_like(state_ref)
        zext_ref[0:CONV_HALO, :] = jnp.zeros((CONV_HALO, CONV_CH), F32)

    x = x_ref[0]
    h = _rmsnorm(x, g_ref[...]).astype(BF16)

    def proj(a, b):
        return _dot(h, win_ref[:, a:b])

    o_q, o_k, o_v, o_g = RET_WIDTH, 2 * RET_WIDTH, 3 * RET_WIDTH, 4 * RET_WIDTH
    o_c = o_g + 2 * CONV_CH

    ca, cg = proj(o_g, o_g + CONV_CH), proj(o_g + CONV_CH, o_c)
    zext_ref[CONV_HALO:CONV_HALO + ts, :] = ca * _sigmoid(cg)
    first = CONV_HALO - (CONV_WIDTH - 1)
    other_cols = [(c0, c0 + 256) for c0 in range(0, o_g, 256)] + [(c0, c0 + 256) for c0 in range(o_c, o_c + 2 * d, 256)]
    per_phase = len(other_cols) // 8
    acc = None
    for b in range(8):
        for k in range(b * per_phase, (b + 1) * per_phase):
            u_ref[:, 256 * k:256 * (k + 1)] = proj(*other_cols[k])
        rows = ts if b == 0 else ts + 8
        part = None
        for a in range(CONV_HALO // 8 + 1):
            w = 8 * a + b - first
            if 0 <= w < CONV_WIDTH:
                term = zext_ref[8 * a:8 * a + rows, :] * cw_ref[w:w + 1, :]
                part = term if part is None else part + term
        if b == 0:
            acc = part
        else:
            phase_ref[b - 1] = part
            acc = acc + phase_ref[b - 1, b:b + ts, :]
    acc = acc + cb_ref[...]
    zext_ref[0:CONV_HALO, :] = zext_ref[ts:ts + CONV_HALO, :]
    mu = jnp.mean(acc, axis=-1, keepdims=True)
    cen = acc - mu
    var = jnp.mean(cen * cen, axis=-1, keepdims=True)
    zf = cen * lax.rsqrt(var + EPS) * lng_ref[...] + lnb_ref[...]
    y_conv = _dot((zf * _sigmoid(zf)).astype(BF16), wco_ref[...])
    uq, uk, uv, sg = (u_ref[:, c0:c0 + RET_WIDTH] for c0 in (0, o_q, o_k, o_v))
    gate_pre = u_ref[:, o_g:o_g + 2 * d]

    cos, sin = cos_ref[...], sin_ref[...]
    heads = []
    for hd in range(RET_HEADS):
        sl = slice(hd * HEAD_DIM, (hd + 1) * HEAD_DIM)
        qh = uq[:, sl]
        qh = qh * cos + pltpu.roll(qh, HEAD_DIM // 2, 1) * sin
        kh = uk[:, sl]
        kh = (kh * cos + pltpu.roll(kh, HEAD_DIM // 2, 1) * sin) * QK_SCALE
        vh = uv[:, sl]
        rows = []
        for c in range(ts // RET_CHUNK):
            r = slice(c * RET_CHUNK, (c + 1) * RET_CHUNK)
            qc = qh[r].astype(BF16)
            kc = kh[r]
            vc = vh[r].astype(BF16)
            sc = _dot_nt(qc, kc.astype(BF16)) * decay_ref[hd]
            inner = _dot(sc.astype(BF16), vc)
            st = state_ref[hd]
            cross = _dot(qc, st.astype(BF16)) * xi_ref[hd]
            kz = (kc * zeta_ref[hd]).astype(BF16)
            state_ref[hd] = chunk_decay[hd] * st + _dot_tn(kz, vc)
            rows.append(inner + cross)
        rh = jnp.concatenate(rows, axis=0)
        heads.append(rh * lax.rsqrt(jnp.mean(rh * rh, axis=-1, keepdims=True) + EPS))
    r = jnp.concatenate(heads, axis=1) * rg_ref[...]
    r = (sg * _sigmoid(sg)) * r
    y_ret = _dot(r.astype(BF16), wro_ref[...])

    gate = _sigmoid(gate_pre + bg_ref[...])
    merged = (gate[:, 0:d] * y_ret + gate[:, d:2 * d] * y_conv).astype(BF16)
    o_ref[0] = x + _dot(merged, wmo_ref[...])


def _mixer(x, cos, sin, g, w_in, b_gate, ret_g, w_ret_out, conv_w, conv_b, ln_g, ln_b, w_conv_out, w_mix_out):
    b, s, d = x.shape
    ts = SEQ_TILE
    decay, zeta, xi, chunk_decay = _retention_tables()
    in_cols = w_in.shape[1]
    cw = jnp.zeros((CONV_HALO, CONV_CH), F32).at[:CONV_WIDTH].set(conv_w)
    row = lambda v: v[None, :]
    tab = (RET_HEADS, RET_CHUNK, HEAD_DIM)
    return pl.pallas_call(
        functools.partial(_mixer_kernel, chunk_decay),
        name="mixer",
        grid=(b, s // ts),
        in_specs=[
            pl.BlockSpec((1, ts, d), lambda i, j: (i, j, 0)),
            pl.BlockSpec((ts, HEAD_DIM), lambda i, j: (j, 0)),
            pl.BlockSpec((ts, HEAD_DIM), lambda i, j: (j, 0)),
            _const_spec((1, d)), _const_spec((d, in_cols)), _const_spec((1, 2 * d)),
            _const_spec((1, RET_WIDTH)), _const_spec((RET_WIDTH, d)),
            _const_spec((CONV_HALO, CONV_CH)), _const_spec((1, CONV_CH)), _const_spec((1, CONV_CH)),
            _const_spec((1, CONV_CH)), _const_spec((CONV_CH, d)), _const_spec((d, d)),
            _const_spec((RET_HEADS, RET_CHUNK, RET_CHUNK)), _const_spec(tab), _const_spec(tab),
        ],
        out_specs=pl.BlockSpec((1, ts, d), lambda i, j: (i, j, 0)),
        out_shape=jax.ShapeDtypeStruct((b, s, d), F32),
        scratch_shapes=[pltpu.VMEM((RET_HEADS, HEAD_DIM, HEAD_DIM), F32),
                        pltpu.VMEM((CONV_HALO + ts, CONV_CH), F32),
                        pltpu.VMEM((7, ts + 8, CONV_CH), F32),
                        pltpu.VMEM((ts, in_cols - 2 * CONV_CH), F32)],
        compiler_params=pltpu.CompilerParams(dimension_semantics=("arbitrary", "arbitrary"),
                                             vmem_limit_bytes=VMEM_LIMIT),
    )(x, cos, sin, row(g), w_in.astype(BF16), row(b_gate), row(ret_g), w_ret_out.astype(BF16), cw,
      row(conv_b), row(ln_g), row(ln_b), w_conv_out.astype(BF16), w_mix_out.astype(BF16), decay, zeta, xi)


def _route(logits, bias):
    scores = _sigmoid(logits)
    sel = scores + bias
    one, zero = jnp.float32(1.0), jnp.float32(0.0)
    top, gscore = [], []
    for g in range(N_GROUPS):
        a = [sel[EXPERTS_PER_GROUP * g + i:EXPERTS_PER_GROUP * g + i + 1, :] for i in range(EXPERTS_PER_GROUP)]
        tg, sg = [], None
        for i in range(EXPERTS_PER_GROUP):
            rank = None
            for j in range(EXPERTS_PER_GROUP):
                if j == i:
                    continue
                ahead = (a[j] >= a[i]) if j < i else (a[j] > a[i])
                ahead = jnp.where(ahead, one, zero)
                rank = ahead if rank is None else rank + ahead
            in_top = rank < 2.0
            tg.append(in_top)
            contrib = jnp.where(in_top, a[i], zero)
            sg = contrib if sg is None else sg + contrib
        top.append(tg)
        gscore.append(sg)
    cls = jnp.zeros_like(gscore[0])
    w_lo = jnp.zeros_like(cls)
    w_hi = jnp.zeros_like(cls)
    for g in range(N_GROUPS):
        behind = None
        for g2 in range(N_GROUPS):
            if g2 == g:
                continue
            ahead = (gscore[g2] >= gscore[g]) if g2 < g else (gscore[g2] > gscore[g])
            behind = ahead if behind is None else (behind | ahead)
        best = jnp.logical_not(behind)
        for p, (i, j) in enumerate(PAIRS):
            active = best & top[g][i] & top[g][j]
            e_lo, e_hi = EXPERTS_PER_GROUP * g + i, EXPERTS_PER_GROUP * g + j
            cls = jnp.where(active, jnp.float32(len(PAIRS) * g + p), cls)
            w_lo = jnp.where(active, scores[e_lo:e_lo + 1, :], w_lo)
            w_hi = jnp.where(active, scores[e_hi:e_hi + 1, :], w_hi)
    den = w_lo + w_hi
    return cls, w_lo / den, w_hi / den


def _xattn_route_kernel(x_ref, kv_ref, gx_ref, wq_ref, wo_ref, gf_ref, wrh_ref, wrl_ref, br_ref,
                        o_ref, hf_ref, meta_ref, cnt_ref, carry_ref):
    ts = x_ref.shape[1]
    d = x_ref.shape[2]
    first = (pl.program_id(0) == 0) & (pl.program_id(1) == 0)

    @pl.when(first)
    def _():
        carry_ref[...] = jnp.zeros_like(carry_ref)

    x = x_ref[0]
    h = _rmsnorm(x, gx_ref[...]).astype(BF16)
    q = _dot(h, wq_ref[...])
    kv = kv_ref[0]
    heads = []
    for hd in range(XATTN_HEADS):
        sl = slice(hd * HEAD_DIM, (hd + 1) * HEAD_DIM)
        vs = slice(XATTN_WIDTH + hd * HEAD_DIM, XATTN_WIDTH + (hd + 1) * HEAD_DIM)
        sc = _dot_nt(q[:, sl].astype(BF16), kv[:, sl]) * QK_SCALE
        sc = sc - jnp.max(sc, axis=-1, keepdims=True)
        p = jnp.exp(sc)
        p = p / jnp.sum(p, axis=-1, keepdims=True)
        heads.append(_dot(p.astype(BF16), kv[:, vs]))
    att = jnp.concatenate(heads, axis=1).astype(BF16)
    x2 = x + _dot(att, wo_ref[...])
    o_ref[0] = x2

    hf = _rmsnorm(x2, gf_ref[...])
    h_hi = hf.astype(BF16)
    h_lo = (hf - h_hi.astype(F32)).astype(BF16)
    lg = _dot(h_hi, wrh_ref[...]) + (_dot(h_lo, wrh_ref[...]) + _dot(h_hi, wrl_ref[...]))
    logits = lg.T[0:N_EXPERTS, :]
    cls, w_lo, w_hi = _route(logits, br_ref[...])

    crow = lax.broadcasted_iota(jnp.int32, (CLASS_ROWS, ts), 0).astype(F32)
    onehot = jnp.where(crow == cls, jnp.float32(1.0), jnp.float32(0.0))
    before = lax.broadcasted_iota(jnp.int32, (ts, ts), 0) < lax.broadcasted_iota(jnp.int32, (ts, ts), 1)
    tri = jnp.where(before, jnp.float32(1.0), jnp.float32(0.0)).astype(BF16)
    prefix = _dot(onehot.astype(BF16), tri)
    carry = carry_ref[...]
    rank = jnp.sum(onehot * (prefix + carry), axis=0, keepdims=True)
    carry = carry + jnp.sum(onehot, axis=1, keepdims=True)
    carry_ref[...] = carry
    cnt_ref[...] = jnp.broadcast_to(carry, cnt_ref.shape)

    meta = jnp.concatenate([cls, w_lo, w_hi, rank, jnp.zeros((META_LANES - 4, ts), F32)], axis=0)
    meta_ref[...] = meta[0:8, :]
    hf_ref[:, 0:d] = hf
    hf_ref[:, d:d + META_LANES] = meta.T


def _xattn_route(x, kv, g_x, w_xq, w_xo, g_f, w_router, b_router):
    b, s, d = x.shape
    ts = SEQ_TILE
    m = kv.shape[1]
    nt = s // ts
    row = lambda v: v[None, :]
    wr = jnp.zeros((d, META_LANES), F32).at[:, :N_EXPERTS].set(w_router)
    wr_hi = wr.astype(BF16)
    wr_lo = (wr - wr_hi.astype(F32)).astype(BF16)
    return pl.pallas_call(
        _xattn_route_kernel,
        name="xattn_route",
        grid=(b, nt),
        in_specs=[
            pl.BlockSpec((1, ts, d), lambda i, j: (i, j, 0)),
            pl.BlockSpec((1, m, 2 * XATTN_WIDTH), lambda i, j: (i, 0, 0)),
            _const_spec((1, d)), _const_spec((d, XATTN_WIDTH)), _const_spec((XATTN_WIDTH, d)),
            _const_spec((1, d)), _const_spec((d, META_LANES)), _const_spec((d, META_LANES)),
            _const_spec((N_EXPERTS, 1)),
        ],
        out_specs=[
            pl.BlockSpec((1, ts, d), lambda i, j: (i, j, 0)),
            pl.BlockSpec((ts, d + META_LANES), lambda i, j: (i * nt + j, 0)),
            pl.BlockSpec((8, ts), lambda i, j: (0, i * nt + j)),
            _const_spec((CLASS_ROWS, 128)),
        ],
        out_shape=[
            jax.ShapeDtypeStruct((b, s, d), F32),
            jax.ShapeDtypeStruct((b * s, d + META_LANES), F32),
            jax.ShapeDtypeStruct((8, b * s), F32),
            jax.ShapeDtypeStruct((CLASS_ROWS, 128), F32),
        ],
        scratch_shapes=[pltpu.VMEM((CLASS_ROWS, 1), F32)],
        compiler_params=pltpu.CompilerParams(dimension_semantics=("arbitrary", "arbitrary"),
                                             vmem_limit_bytes=VMEM_LIMIT),
    )(x, kv, row(g_x), w_xq.astype(BF16), w_xo.astype(BF16), row(g_f), wr_hi, wr_lo, b_router[:, None])


def _dispatch_tables(meta, counts, t, tm, n_tiles):
    cls = meta[0].astype(jnp.int32)
    rank = meta[3].astype(jnp.int32)
    cnt = counts[:N_CLASSES, 0].astype(jnp.int32)
    tiles_c = (cnt + tm - 1) // tm
    tiles_end = jnp.cumsum(tiles_c)
    pos = ((tiles_end - tiles_c) * tm)[cls] + rank
    n_used = tiles_end[-1]
    tile_id = jnp.minimum(jnp.arange(n_tiles, dtype=jnp.int32), n_used - 1)
    tile_cls = jnp.sum((tile_id[:, None] >= tiles_end[None, :]).astype(jnp.int32), axis=1)
    tile_cls = jnp.minimum(tile_cls, N_CLASSES - 1)
    pair = tile_cls % len(PAIRS)
    base = EXPERTS_PER_GROUP * (tile_cls // len(PAIRS))
    lo_of = jnp.asarray([p[0] for p in PAIRS], jnp.int32)
    hi_of = jnp.asarray([p[1] for p in PAIRS], jnp.int32)
    e_lo = base + lo_of[pair]
    e_hi = base + hi_of[pair]
    return pos, e_lo.astype(jnp.int32), e_hi.astype(jnp.int32), n_used.reshape(1).astype(jnp.int32)


def _dispatch_kernel(pos_ref, hf_ref, init_hbm, hs_hbm, sem):
    del init_hbm
    td = hf_ref.shape[0]
    base = pl.program_id(0) * td
    for r in range(td):
        pltpu.make_async_copy(hf_ref.at[pl.ds(r, 1)], hs_hbm.at[pl.ds(pos_ref[base + r], 1)], sem.at[0]).start()
    pltpu.make_async_copy(hf_ref, hs_hbm.at[pl.ds(0, td)], sem.at[0]).wait()


def _dispatch(hf_ext, pos, n_rows):
    t, w = hf_ext.shape
    td = DISPATCH_TILE
    grid_spec = pltpu.PrefetchScalarGridSpec(
        num_scalar_prefetch=1,
        grid=(t // td,),
        in_specs=[pl.BlockSpec((td, w), lambda i, p: (i, 0)), pl.BlockSpec(memory_space=pl.ANY)],
        out_specs=pl.BlockSpec(memory_space=pl.ANY),
        scratch_shapes=[pltpu.SemaphoreType.DMA((1,))],
    )
    return pl.pallas_call(
        _dispatch_kernel,
        name="dispatch",
        grid_spec=grid_spec,
        out_shape=jax.ShapeDtypeStruct((n_rows, w), F32),
        input_output_aliases={2: 0},
        compiler_params=pltpu.CompilerParams(dimension_semantics=("arbitrary",), vmem_limit_bytes=VMEM_LIMIT),
    )(pos, hf_ext, jnp.zeros((n_rows, w), F32))


def _moe_kernel(elo_ref, ehi_ref, nused_ref, rows_ref, wg_lo, wg_hi, wu_lo, wu_hi, wd_lo, wd_hi, y_ref):
    tm, d = y_ref.shape
    i = pl.program_id(0)
    used = i < nused_ref[0]

    @pl.when(used)
    def _():
        hb = rows_ref[:, 0:d].astype(BF16)
        y = None
        for col, wg, wu, wd in ((d + 1, wg_lo, wu_lo, wd_lo), (d + 2, wg_hi, wu_hi, wd_hi)):
            gate = _dot(hb, wg[0])
            act = (gate * _sigmoid(gate)) * _dot(hb, wu[0])
            part = rows_ref[:, col:col + 1] * _dot(act.astype(BF16), wd[0])
            y = part if y is None else y + part
        y_ref[...] = y

    @pl.when(jnp.logical_not(used))
    def _():
        y_ref[...] = jnp.zeros_like(y_ref)


def _moe(hs, e_lo, e_hi, n_used, w_gate, w_up, w_down, n_tiles, tm):
    w = hs.shape[1]
    d = w - META_LANES
    f = w_gate.shape[2]
    lo3 = lambda i, elo, ehi, nu: (elo[i], 0, 0)
    hi3 = lambda i, elo, ehi, nu: (ehi[i], 0, 0)
    grid_spec = pltpu.PrefetchScalarGridSpec(
        num_scalar_prefetch=3,
        grid=(n_tiles,),
        in_specs=[
            pl.BlockSpec((tm, w), lambda i, elo, ehi, nu: (jnp.minimum(i, nu[0] - 1), 0)),
            pl.BlockSpec((1, d, f), lo3), pl.BlockSpec((1, d, f), hi3),
            pl.BlockSpec((1, d, f), lo3), pl.BlockSpec((1, d, f), hi3),
            pl.BlockSpec((1, f, d), lo3), pl.BlockSpec((1, f, d), hi3),
        ],
        out_specs=pl.BlockSpec((tm, d), lambda i, elo, ehi, nu: (i, 0)),
    )
    return pl.pallas_call(
        _moe_kernel,
        name="moe_experts",
        grid_spec=grid_spec,
        out_shape=jax.ShapeDtypeStruct((n_tiles * tm, d), F32),
        compiler_params=pltpu.CompilerParams(dimension_semantics=("arbitrary",), vmem_limit_bytes=VMEM_LIMIT),
    )(e_lo, e_hi, n_used, hs, w_gate, w_gate, w_up, w_up, w_down, w_down)


def _combine_kernel(final_norm, pos_ref, x_ref, y_hbm, g_ref, o_ref, rows_ref, sem):
    tc = x_ref.shape[0]
    i = pl.program_id(0)
    n = pl.num_programs(0)

    def issue(tile, slot):
        base = tile * tc
        for r in range(tc):
            pltpu.make_async_copy(y_hbm.at[pl.ds(pos_ref[base + r], 1)], rows_ref.at[slot, pl.ds(r, 1)],
                                  sem.at[slot]).start()

    @pl.when(i == 0)
    def _():
        issue(0, 0)

    for slot in range(2):
        @pl.when(i % 2 == slot)
        def _():
            @pl.when(i + 1 < n)
            def _():
                issue(i + 1, 1 - slot)

            pltpu.make_async_copy(y_hbm.at[pl.ds(0, tc)], rows_ref.at[slot], sem.at[slot]).wait()
            x = x_ref[...] + rows_ref[slot]
            if final_norm:
                x = _rmsnorm(x, g_ref[...])
            o_ref[...] = x


def _combine(x, y_sorted, pos, g, final_norm):
    t, d = x.shape
    tc = COMBINE_TILE
    grid_spec = pltpu.PrefetchScalarGridSpec(
        num_scalar_prefetch=1,
        grid=(t // tc,),
        in_specs=[pl.BlockSpec((tc, d), lambda i, p: (i, 0)), pl.BlockSpec(memory_space=pl.ANY),
                  pl.BlockSpec((1, d), lambda i, p: (0, 0))],
        out_specs=pl.BlockSpec((tc, d), lambda i, p: (i, 0)),
        scratch_shapes=[pltpu.VMEM((2, tc, d), F32), pltpu.SemaphoreType.DMA((2,))],
    )
    return pl.pallas_call(
        functools.partial(_combine_kernel, final_norm),
        name="combine",
        grid_spec=grid_spec,
        out_shape=jax.ShapeDtypeStruct((t, d), F32),
        compiler_params=pltpu.CompilerParams(dimension_semantics=("arbitrary",), vmem_limit_bytes=VMEM_LIMIT),
    )(pos, x, y_sorted, g[None, :])


def kernel(x, mem, positions, norm_mix_g, w_in, b_branch_gate, ret_norm_g, w_ret_out, conv_w, conv_b, conv_ln_g, conv_ln_b, w_conv_out, w_mix_out, norm_xattn_g, norm_mem_g, w_xq, w_xkv, w_xo, norm_ffn_g, w_router, b_router, w_exp_gate, w_exp_up, w_exp_down, norm_final_g):
    b, s, d = x.shape
    depth = w_in.shape[0]
    t = b * s
    assert s % SEQ_TILE == 0 and SEQ_TILE % RET_CHUNK == 0
    assert t % MOE_TILE == 0 and t % COMBINE_TILE == 0 and t % DISPATCH_TILE == 0
    tm = MOE_TILE
    n_tiles = t // tm + N_CLASSES
    cos, sin = _rope_tables(positions)
    n_exp, _, f = w_exp_gate.shape[1:]
    wg_all = w_exp_gate.astype(BF16).reshape(depth * n_exp, d, f)
    wu_all = w_exp_up.astype(BF16).reshape(depth * n_exp, d, f)
    wd_all = w_exp_down.astype(BF16).reshape(depth * n_exp, f, d)
    for l in range(depth):
        x = _mixer(x, cos, sin, norm_mix_g[l], w_in[l], b_branch_gate[l], ret_norm_g[l], w_ret_out[l],
                   conv_w[l], conv_b[l], conv_ln_g[l], conv_ln_b[l], w_conv_out[l], w_mix_out[l])
        kv = _mem_kv(mem, norm_mem_g[l], w_xkv[l].astype(BF16))
        x, hf_ext, meta, counts = _xattn_route(x, kv, norm_xattn_g[l], w_xq[l], w_xo[l], norm_ffn_g[l],
                                               w_router, b_router)
        pos, e_lo, e_hi, n_used = _dispatch_tables(meta, counts, t, tm, n_tiles)
        hs = _dispatch(hf_ext, pos, n_tiles * tm)
        y_sorted = _moe(hs, e_lo + l * n_exp, e_hi + l * n_exp, n_used, wg_all, wu_all, wd_all, n_tiles, tm)
        last = l == depth - 1
        x = _combine(x.reshape(t, d), y_sorted, pos, norm_final_g, last).reshape(b, s, d)
    return x
```

```python
import functools

import jax
import jax.numpy as jnp
import numpy as np
from jax import lax
from jax.experimental import pallas as pl
from jax.experimental.pallas import tpu as pltpu

F32 = jnp.float32
BF16 = jnp.bfloat16

RET_HEADS = 4
HEAD_DIM = 128
RET_WIDTH = RET_HEADS * HEAD_DIM
RET_CHUNK = 128
CONV_CH = 512
CONV_WIDTH = 31
CONV_HALO = 32
XATTN_HEADS = 4
XATTN_WIDTH = XATTN_HEADS * HEAD_DIM
N_EXPERTS = 16
N_GROUPS = 4
EXPERTS_PER_GROUP = 4
ROPE_BASE = 10000.0
EPS = 1e-6
QK_SCALE = HEAD_DIM ** -0.5

PAIRS = ((0, 1), (0, 2), (0, 3), (1, 2), (1, 3), (2, 3))
N_CLASSES = N_GROUPS * len(PAIRS)
CLASS_ROWS = 32
META_LANES = 128

SEQ_TILE = 256
MOE_TILE = 256
COMBINE_TILE = 256
VMEM_LIMIT = 56 * 1024 * 1024


def _rmsnorm(x, g):
    return x * lax.rsqrt(jnp.mean(x * x, axis=-1, keepdims=True) + EPS) * g


def _sigmoid(x):
    return 1.0 / (1.0 + jnp.exp(-x))


def _dot(a, b):
    return jnp.dot(a, b, preferred_element_type=F32)


def _dot_nt(a, b):
    return lax.dot_general(a, b, (((1,), (1,)), ((), ())), preferred_element_type=F32)


def _dot_tn(a, b):
    return lax.dot_general(a, b, (((0,), (0,)), ((), ())), preferred_element_type=F32)


def _const_spec(shape):
    n = len(shape)
    return pl.BlockSpec(shape, lambda *_: (0,) * n)


def _rope_kernel(pos_ref, invf_ref, cos_ref, sin_ref):
    ang = pos_ref[...] * invf_ref[...]
    lane = lax.broadcasted_iota(jnp.int32, ang.shape, 1)
    s = jnp.sin(ang)
    cos_ref[...] = jnp.cos(ang)
    sin_ref[...] = jnp.where(lane < HEAD_DIM // 2, -s, s)


def _rope_tables(positions):
    s = positions.shape[0]
    half = HEAD_DIM // 2
    inv_freq = ROPE_BASE ** (-jnp.arange(half, dtype=F32) / half)
    invf = jnp.concatenate([inv_freq, inv_freq])[None, :]
    pos = positions.astype(F32)[:, None]
    ts = min(s, 1024)
    return pl.pallas_call(
        _rope_kernel,
        name="rope_tables",
        grid=(s // ts,),
        in_specs=[pl.BlockSpec((ts, 1), lambda i: (i, 0)), _const_spec((1, HEAD_DIM))],
        out_specs=[pl.BlockSpec((ts, HEAD_DIM), lambda i: (i, 0))] * 2,
        out_shape=[jax.ShapeDtypeStruct((s, HEAD_DIM), F32)] * 2,
    )(pos, invf)


def _memkv_kernel(mem_ref, g_ref, w_ref, kv_ref):
    m = _rmsnorm(mem_ref[0], g_ref[...]).astype(BF16)
    kv_ref[0] = _dot(m, w_ref[...]).astype(BF16)


def _mem_kv(mem, g, w_xkv):
    b, m, d = mem.shape
    return pl.pallas_call(
        _memkv_kernel,
        name="mem_kv",
        grid=(b,),
        in_specs=[pl.BlockSpec((1, m, d), lambda i: (i, 0, 0)), _const_spec((1, d)),
                  _const_spec((d, 2 * XATTN_WIDTH))],
        out_specs=pl.BlockSpec((1, m, 2 * XATTN_WIDTH), lambda i: (i, 0, 0)),
        out_shape=jax.ShapeDtypeStruct((b, m, 2 * XATTN_WIDTH), BF16),
        compiler_params=pltpu.CompilerParams(vmem_limit_bytes=VMEM_LIMIT),
    )(mem, g[None, :], w_xkv)


def _retention_tables():
    h = np.arange(RET_HEADS, dtype=np.float64)
    log_gamma = np.log(1.0 - 2.0 ** (-5.0 - h))
    idx = np.arange(RET_CHUNK, dtype=np.float64)
    diff = idx[:, None] - idx[None, :]
    decay = np.where(diff[None] >= 0.0, np.exp(np.maximum(diff, 0.0)[None] * log_gamma[:, None, None]), 0.0)
    zeta = np.exp((RET_CHUNK - 1.0 - idx)[None, :] * log_gamma[:, None])
    xi = np.exp((idx + 1.0)[None, :] * log_gamma[:, None])
    chunk_decay = np.exp(RET_CHUNK * log_gamma)
    bc = lambda t: np.broadcast_to(t[:, :, None], (RET_HEADS, RET_CHUNK, HEAD_DIM))
    return (jnp.asarray(decay, F32), jnp.asarray(bc(zeta), F32), jnp.asarray(bc(xi), F32),
            tuple(float(c) for c in chunk_decay))


def _gather_rows(idx_ref, tile, n, src_hbm, dst_ref, slot, sem):
    base = tile * n
    for r in range(n):
        pltpu.make_async_copy(src_hbm.at[pl.ds(idx_ref[base + r], 1)], dst_ref.at[slot, pl.ds(r, 1)],
                              sem.at[slot]).start()


def _wait_gathered(n, src_hbm, dst_ref, slot, sem):
    pltpu.make_async_copy(src_hbm.at[pl.ds(0, n)], dst_ref.at[slot], sem.at[slot]).wait()


def _mixer_kernel(chunk_decay, add_moe, *refs):
    if add_moe:
        pos_ref, refs = refs[0], refs[1:]
    (x_ref, cos_ref, sin_ref, g_ref, win_ref, bg_ref, rg_ref, wro_ref, cw_ref, cb_ref, lng_ref, lnb_ref,
     wco_ref, wmo_ref, decay_ref, zeta_ref, xi_ref) = refs[:17]
    refs = refs[17:]
    if add_moe:
        y_hbm, refs = refs[0], refs[1:]
    o_ref, state_ref, zext_ref, phase_ref, u_ref = refs[:5]
    ts = x_ref.shape[1]
    d = x_ref.shape[2]

    @pl.when(pl.program_id(1) == 0)
    def _():
        state_ref[...] = jnp.zeros_like(state_ref)
        zext_ref[0:CONV_HALO, :] = jnp.zeros((CONV_HALO, CONV_CH), F32)

    x = x_ref[0]
    if add_moe:
        rows_ref, sem = refs[5:7]
        lin = pl.program_id(0) * pl.num_programs(1) + pl.program_id(1)
        total = pl.num_programs(0) * pl.num_programs(1)
        slot = lin % 2

        @pl.when(lin == 0)
        def _():
            _gather_rows(pos_ref, 0, ts, y_hbm, rows_ref, 0, sem)

        _wait_gathered(ts, y_hbm, rows_ref, slot, sem)
        x = x + rows_ref[slot]
        _gather_rows(pos_ref, jnp.minimum(lin + 1, total - 1), ts, y_hbm, rows_ref, 1 - slot, sem)
    h = _rmsnorm(x, g_ref[...]).astype(BF16)

    def proj(a, b):
        return _dot(h, win_ref[:, a:b])

    o_q, o_k, o_v, o_g = RET_WIDTH, 2 * RET_WIDTH, 3 * RET_WIDTH, 4 * RET_WIDTH
    o_c = o_g + 2 * CONV_CH

    ca, cg = proj(o_g, o_g + CONV_CH), proj(o_g + CONV_CH, o_c)
    zext_ref[CONV_HALO:CONV_HALO + ts, :] = ca * _sigmoid(cg)
    first = CONV_HALO - (CONV_WIDTH - 1)
    other_cols = [(c0, c0 + 256) for c0 in range(0, o_g, 256)] + [(c0, c0 + 256) for c0 in range(o_c, o_c + 2 * d, 256)]
    per_phase = len(other_cols) // 8
    acc = None
    for b in range(8):
        for k in range(b * per_phase, (b + 1) * per_phase):
            u_ref[:, 256 * k:256 * (k + 1)] = proj(*other_cols[k])
        rows = ts if b == 0 else ts + 8
        part = None
        for a in range(CONV_HALO // 8 + 1):
            w = 8 * a + b - first
            if 0 <= w < CONV_WIDTH:
                term = zext_ref[8 * a:8 * a + rows, :] * cw_ref[w:w + 1, :]
                part = term if part is None else part + term
        if b == 0:
            acc = part
        else:
            phase_ref[b - 1] = part
            acc = acc + phase_ref[b - 1, b:b + ts, :]
    acc = acc + cb_ref[...]
    zext_ref[0:CONV_HALO, :] = zext_ref[ts:ts + CONV_HALO, :]
    mu = jnp.mean(acc, axis=-1, keepdims=True)
    cen = acc - mu
    var = jnp.mean(cen * cen, axis=-1, keepdims=True)
    zf = cen * lax.rsqrt(var + EPS) * lng_ref[...] + lnb_ref[...]
    y_conv = _dot((zf * _sigmoid(zf)).astype(BF16), wco_ref[...])
    uq, uk, uv, sg = (u_ref[:, c0:c0 + RET_WIDTH] for c0 in (0, o_q, o_k, o_v))
    gate_pre = u_ref[:, o_g:o_g + 2 * d]

    cos, sin = cos_ref[...], sin_ref[...]
    heads = []
    for hd in range(RET_HEADS):
        sl = slice(hd * HEAD_DIM, (hd + 1) * HEAD_DIM)
        qh = uq[:, sl]
        qh = qh * cos + pltpu.roll(qh, HEAD_DIM // 2, 1) * sin
        kh = uk[:, sl]
        kh = (kh * cos + pltpu.roll(kh, HEAD_DIM // 2, 1) * sin) * QK_SCALE
        vh = uv[:, sl]
        rows = []
        for c in range(ts // RET_CHUNK):
            r = slice(c * RET_CHUNK, (c + 1) * RET_CHUNK)
            qc = qh[r].astype(BF16)
            kc = kh[r]
            vc = vh[r].astype(BF16)
            sc = _dot_nt(qc, kc.astype(BF16)) * decay_ref[hd]
            inner = _dot(sc.astype(BF16), vc)
            st = state_ref[hd]
            cross = _dot(qc, st.astype(BF16)) * xi_ref[hd]
            kz = (kc * zeta_ref[hd]).astype(BF16)
            state_ref[hd] = chunk_decay[hd] * st + _dot_tn(kz, vc)
            rows.append(inner + cross)
        rh = jnp.concatenate(rows, axis=0)
        heads.append(rh * lax.rsqrt(jnp.mean(rh * rh, axis=-1, keepdims=True) + EPS))
    r = jnp.concatenate(heads, axis=1) * rg_ref[...]
    r = (sg * _sigmoid(sg)) * r
    y_ret = _dot(r.astype(BF16), wro_ref[...])

    gate = _sigmoid(gate_pre + bg_ref[...])
    merged = (gate[:, 0:d] * y_ret + gate[:, d:2 * d] * y_conv).astype(BF16)
    o_ref[0] = x + _dot(merged, wmo_ref[...])
    if add_moe:
        @pl.when(lin == total - 1)
        def _():
            _wait_gathered(ts, y_hbm, rows_ref, 1 - slot, sem)


def _mixer(x, cos, sin, g, w_in, b_gate, ret_g, w_ret_out, conv_w, conv_b, ln_g, ln_b, w_conv_out, w_mix_out,
           moe=None):
    b, s, d = x.shape
    ts = SEQ_TILE
    decay, zeta, xi, chunk_decay = _retention_tables()
    in_cols = w_in.shape[1]
    cw = jnp.zeros((CONV_HALO, CONV_CH), F32).at[:CONV_WIDTH].set(conv_w)
    row = lambda v: v[None, :]
    tab = (RET_HEADS, RET_CHUNK, HEAD_DIM)
    add_moe = moe is not None
    in_specs = [
        pl.BlockSpec((1, ts, d), lambda i, j, *_: (i, j, 0)),
        pl.BlockSpec((ts, HEAD_DIM), lambda i, j, *_: (j, 0)),
        pl.BlockSpec((ts, HEAD_DIM), lambda i, j, *_: (j, 0)),
        _const_spec((1, d)), _const_spec((d, in_cols)), _const_spec((1, 2 * d)),
        _const_spec((1, RET_WIDTH)), _const_spec((RET_WIDTH, d)),
        _const_spec((CONV_HALO, CONV_CH)), _const_spec((1, CONV_CH)), _const_spec((1, CONV_CH)),
        _const_spec((1, CONV_CH)), _const_spec((CONV_CH, d)), _const_spec((d, d)),
        _const_spec((RET_HEADS, RET_CHUNK, RET_CHUNK)), _const_spec(tab), _const_spec(tab),
    ]
    scratch = [pltpu.VMEM((RET_HEADS, HEAD_DIM, HEAD_DIM), F32),
               pltpu.VMEM((CONV_HALO + ts, CONV_CH), F32),
               pltpu.VMEM((7, ts + 8, CONV_CH), F32),
               pltpu.VMEM((ts, in_cols - 2 * CONV_CH), F32)]
    args = [x, cos, sin, row(g), w_in.astype(BF16), row(b_gate), row(ret_g), w_ret_out.astype(BF16), cw,
            row(conv_b), row(ln_g), row(ln_b), w_conv_out.astype(BF16), w_mix_out.astype(BF16), decay, zeta, xi]
    if add_moe:
        y_sorted, pos = moe
        in_specs.append(pl.BlockSpec(memory_space=pl.ANY))
        scratch += [pltpu.VMEM((2, ts, d), F32), pltpu.SemaphoreType.DMA((2,))]
        args = [pos] + args + [y_sorted]
    grid_spec = pltpu.PrefetchScalarGridSpec(
        num_scalar_prefetch=1 if add_moe else 0,
        grid=(b, s // ts),
        in_specs=in_specs,
        out_specs=pl.BlockSpec((1, ts, d), lambda i, j, *_: (i, j, 0)),
        scratch_shapes=scratch,
    )
    return pl.pallas_call(
        functools.partial(_mixer_kernel, chunk_decay, add_moe),
        name="mixer",
        grid_spec=grid_spec,
        out_shape=jax.ShapeDtypeStruct((b, s, d), F32),
        compiler_params=pltpu.CompilerParams(dimension_semantics=("arbitrary", "arbitrary"),
                                             vmem_limit_bytes=VMEM_LIMIT),
    )(*args)


def _route(logits, bias):
    scores = _sigmoid(logits)
    sel = scores + bias
    one, zero = jnp.float32(1.0), jnp.float32(0.0)
    top, gscore = [], []
    for g in range(N_GROUPS):
        a = [sel[EXPERTS_PER_GROUP * g + i:EXPERTS_PER_GROUP * g + i + 1, :] for i in range(EXPERTS_PER_GROUP)]
        tg, sg = [], None
        for i in range(EXPERTS_PER_GROUP):
            rank = None
            for j in range(EXPERTS_PER_GROUP):
                if j == i:
                    continue
                ahead = (a[j] >= a[i]) if j < i else (a[j] > a[i])
                ahead = jnp.where(ahead, one, zero)
                rank = ahead if rank is None else rank + ahead
            in_top = rank < 2.0
            tg.append(in_top)
            contrib = jnp.where(in_top, a[i], zero)
            sg = contrib if sg is None else sg + contrib
        top.append(tg)
        gscore.append(sg)
    cls = jnp.zeros_like(gscore[0])
    w_lo = jnp.zeros_like(cls)
    w_hi = jnp.zeros_like(cls)
    for g in range(N_GROUPS):
        behind = None
        for g2 in range(N_GROUPS):
            if g2 == g:
                continue
            ahead = (gscore[g2] >= gscore[g]) if g2 < g else (gscore[g2] > gscore[g])
            behind = ahead if behind is None else (behind | ahead)
        best = jnp.logical_not(behind)
        for p, (i, j) in enumerate(PAIRS):
            active = best & top[g][i] & top[g][j]
            e_lo, e_hi = EXPERTS_PER_GROUP * g + i, EXPERTS_PER_GROUP * g + j
            cls = jnp.where(active, jnp.float32(len(PAIRS) * g + p), cls)
            w_lo = jnp.where(active, scores[e_lo:e_lo + 1, :], w_lo)
            w_hi = jnp.where(active, scores[e_hi:e_hi + 1, :], w_hi)
    den = w_lo + w_hi
    return cls, w_lo / den, w_hi / den


def _xattn_route_kernel(x_ref, kv_ref, gx_ref, wq_ref, wo_ref, gf_ref, wrh_ref, wrl_ref, br_ref,
                        init_hbm, o_ref, meta_ref, alloc_ref, hs_hbm,
                        fill_ref, cur_ref, next_ref, rows_ref, posv_ref, pos_smem, sem, sem_pos):
    del init_hbm
    ts = x_ref.shape[1]
    d = x_ref.shape[2]
    tm = float(MOE_TILE)
    lin = pl.program_id(0) * pl.num_programs(1) + pl.program_id(1)
    total = pl.num_programs(0) * pl.num_programs(1)
    slot = lin % 2

    def wait_rows(sl):
        pltpu.make_async_copy(rows_ref.at[sl], hs_hbm.at[pl.ds(0, ts)], sem.at[sl]).wait()

    @pl.when(lin == 0)
    def _():
        fill_ref[...] = jnp.full(fill_ref.shape, tm, F32)
        cur_ref[...] = jnp.zeros_like(cur_ref)
        next_ref[...] = jnp.zeros_like(next_ref)
        rows_ref[1] = jnp.zeros(rows_ref.shape[1:], F32)
        for r in range(ts):
            pos_smem[0, r] = hs_hbm.shape[0] - ts + r

    def pos_to_smem():
        return pltpu.make_async_copy(posv_ref.at[pl.ds(0, 1)], pos_smem, sem_pos.at[0])

    def scatter_rows(sl):
        for r in range(ts):
            pltpu.make_async_copy(rows_ref.at[sl, pl.ds(r, 1)], hs_hbm.at[pl.ds(pos_smem[0, r], 1)],
                                  sem.at[sl]).start()

    @pl.when(lin >= 1)
    def _():
        pos_to_smem().wait()

    scatter_rows(1 - slot)

    x = x_ref[0]
    h = _rmsnorm(x, gx_ref[...]).astype(BF16)
    q = _dot(h, wq_ref[...])
    kv = kv_ref[0]
    heads = []
    for hd in range(XATTN_HEADS):
        sl = slice(hd * HEAD_DIM, (hd + 1) * HEAD_DIM)
        vs = slice(XATTN_WIDTH + hd * HEAD_DIM, XATTN_WIDTH + (hd + 1) * HEAD_DIM)
        sc = _dot_nt(q[:, sl].astype(BF16), kv[:, sl]) * QK_SCALE
        sc = sc - jnp.max(sc, axis=-1, keepdims=True)
        p = jnp.exp(sc)
        p = p / jnp.sum(p, axis=-1, keepdims=True)
        heads.append(_dot(p.astype(BF16), kv[:, vs]))
    att = jnp.concatenate(heads, axis=1).astype(BF16)
    x2 = x + _dot(att, wo_ref[...])
    o_ref[0] = x2

    hf = _rmsnorm(x2, gf_ref[...])
    h_hi = hf.astype(BF16)
    h_lo = (hf - h_hi.astype(F32)).astype(BF16)
    lg = _dot(h_hi, wrh_ref[...]) + (_dot(h_lo, wrh_ref[...]) + _dot(h_hi, wrl_ref[...]))
    logits = lg.T[0:N_EXPERTS, :]
    cls, w_lo, w_hi = _route(logits, br_ref[...])

    one, zero = jnp.float32(1.0), jnp.float32(0.0)
    crow = lax.broadcasted_iota(jnp.int32, (CLASS_ROWS, ts), 0).astype(F32)
    onehot = jnp.where(crow == cls, one, zero)
    before = lax.broadcasted_iota(jnp.int32, (ts, ts), 0) < lax.broadcasted_iota(jnp.int32, (ts, ts), 1)
    prefix = _dot(onehot.astype(BF16), jnp.where(before, one, zero).astype(BF16))
    count = jnp.sum(onehot, axis=1, keepdims=True)
    fill, cur, nxt = fill_ref[...], cur_ref[...], next_ref[...]
    need = jnp.where(fill + count > tm, one, zero)
    lower = (lax.broadcasted_iota(jnp.int32, (CLASS_ROWS, CLASS_ROWS), 1)
             < lax.broadcasted_iota(jnp.int32, (CLASS_ROWS, CLASS_ROWS), 0))
    opened_before = _dot(jnp.where(lower, one, zero).astype(BF16),
                         jnp.broadcast_to(need, (CLASS_ROWS, META_LANES)).astype(BF16))[:, 0:1]
    new_tile = nxt + opened_before
    per_token = lambda v: jnp.sum(onehot * v, axis=0, keepdims=True)
    slot_in_cur = per_token(fill) + per_token(prefix)
    pos = jnp.where(slot_in_cur >= tm, per_token(new_tile) * tm + (slot_in_cur - tm),
                    per_token(cur) * tm + slot_in_cur)
    fill_ref[...] = jnp.where(need > zero, fill + count - tm, fill + count)
    cur_ref[...] = jnp.where(need > zero, new_tile, cur)
    next_ref[...] = nxt + jnp.sum(need, axis=0, keepdims=True)
    lane = lax.broadcasted_iota(jnp.int32, (CLASS_ROWS, META_LANES), 1)
    alloc_ref[...] = jnp.where(lane == 0, need, jnp.where(lane == 1, new_tile, zero))

    meta = jnp.concatenate([cls, w_lo, w_hi, pos, jnp.zeros((META_LANES - 4, ts), F32)], axis=0)
    meta_ref[...] = meta[0:8, :]
    @pl.when(lin >= 1)
    def _():
        wait_rows(slot)

    rows_ref[slot, :, 0:d] = hf
    rows_ref[slot, :, d:d + META_LANES] = meta.T
    posv_ref[...] = jnp.broadcast_to(pos.astype(jnp.int32), posv_ref.shape)
    pos_to_smem().start()

    @pl.when(lin == total - 1)
    def _():
        pos_to_smem().wait()
        scatter_rows(slot)
        wait_rows(slot)
        wait_rows(1 - slot)


def _xattn_route(x, kv, g_x, w_xq, w_xo, g_f, w_router, b_router, n_rows):
    b, s, d = x.shape
    ts = SEQ_TILE
    assert ts <= MOE_TILE
    m = kv.shape[1]
    nt = s // ts
    w = d + META_LANES
    row = lambda v: v[None, :]
    wr = jnp.zeros((d, META_LANES), F32).at[:, :N_EXPERTS].set(w_router)
    wr_hi = wr.astype(BF16)
    wr_lo = (wr - wr_hi.astype(F32)).astype(BF16)
    return pl.pallas_call(
        _xattn_route_kernel,
        name="xattn_route",
        grid=(b, nt),
        in_specs=[
            pl.BlockSpec((1, ts, d), lambda i, j: (i, j, 0)),
            pl.BlockSpec((1, m, 2 * XATTN_WIDTH), lambda i, j: (i, 0, 0)),
            _const_spec((1, d)), _const_spec((d, XATTN_WIDTH)), _const_spec((XATTN_WIDTH, d)),
            _const_spec((1, d)), _const_spec((d, META_LANES)), _const_spec((d, META_LANES)),
            _const_spec((N_EXPERTS, 1)),
            pl.BlockSpec(memory_space=pl.ANY),
        ],
        out_specs=[
            pl.BlockSpec((1, ts, d), lambda i, j: (i, j, 0)),
            pl.BlockSpec((8, ts), lambda i, j: (0, i * nt + j)),
            pl.BlockSpec((CLASS_ROWS, META_LANES), lambda i, j: (i * nt + j, 0)),
            pl.BlockSpec(memory_space=pl.ANY),
        ],
        out_shape=[
            jax.ShapeDtypeStruct((b, s, d), F32),
            jax.ShapeDtypeStruct((8, b * s), F32),
            jax.ShapeDtypeStruct((b * nt * CLASS_ROWS, META_LANES), F32),
            jax.ShapeDtypeStruct((n_rows, w), F32),
        ],
        scratch_shapes=[pltpu.VMEM((CLASS_ROWS, 1), F32), pltpu.VMEM((CLASS_ROWS, 1), F32),
                        pltpu.VMEM((1, 1), F32), pltpu.VMEM((2, ts, w), F32),
                        pltpu.VMEM((8, ts), jnp.int32), pltpu.SMEM((1, ts), jnp.int32),
                        pltpu.SemaphoreType.DMA((2,)), pltpu.SemaphoreType.DMA((1,))],
        input_output_aliases={9: 3},
        compiler_params=pltpu.CompilerParams(dimension_semantics=("arbitrary", "arbitrary"),
                                             vmem_limit_bytes=VMEM_LIMIT),
    )(x, kv, row(g_x), w_xq.astype(BF16), w_xo.astype(BF16), row(g_f), wr_hi, wr_lo, b_router[:, None],
      jnp.zeros((n_rows, w), F32))


def _tile_tables(alloc, n_tiles):
    opened = alloc[:, 0] > 0.0
    tile_of = alloc[:, 1].astype(jnp.int32)
    cls_of = jnp.arange(alloc.shape[0], dtype=jnp.int32) % CLASS_ROWS
    tile_ids = jnp.arange(n_tiles, dtype=jnp.int32)
    hit = opened[None, :] & (tile_of[None, :] == tile_ids[:, None])
    tile_cls = jnp.sum(jnp.where(hit, cls_of[None, :], 0), axis=1)
    used = jnp.any(hit, axis=1)
    n_used = jnp.sum(used.astype(jnp.int32))
    key = jnp.where(used, tile_cls, N_CLASSES)
    ahead = (key[None, :] < key[:, None]) | ((key[None, :] == key[:, None]) & (tile_ids[None, :] < tile_ids[:, None]))
    rank = jnp.sum(ahead.astype(jnp.int32), axis=1)
    order = jnp.sum(jnp.where(rank[None, :] == tile_ids[:, None], tile_ids[None, :], 0), axis=1)
    order_in = order[jnp.minimum(tile_ids, n_used - 1)]
    step_cls = tile_cls[order_in]
    pair = step_cls % len(PAIRS)
    base = EXPERTS_PER_GROUP * (step_cls // len(PAIRS))
    lo_of = jnp.asarray([p[0] for p in PAIRS], jnp.int32)
    hi_of = jnp.asarray([p[1] for p in PAIRS], jnp.int32)
    i32 = lambda v: v.astype(jnp.int32)
    return i32(order_in), i32(order), i32(base + lo_of[pair]), i32(base + hi_of[pair]), i32(n_used.reshape(1))


def _moe_kernel(oin_ref, oout_ref, elo_ref, ehi_ref, nused_ref, rows_ref, wg_lo, wg_hi, wu_lo, wu_hi, wd_lo, wd_hi, y_ref):
    tm, d = y_ref.shape
    i = pl.program_id(0)
    used = i < nused_ref[0]

    @pl.when(used)
    def _():
        hb = rows_ref[:, 0:d].astype(BF16)
        y = None
        for col, wg, wu, wd in ((d + 1, wg_lo, wu_lo, wd_lo), (d + 2, wg_hi, wu_hi, wd_hi)):
            gate = _dot(hb, wg[0])
            act = (gate * _sigmoid(gate)) * _dot(hb, wu[0])
            part = rows_ref[:, col:col + 1] * _dot(act.astype(BF16), wd[0])
            y = part if y is None else y + part
        y_ref[...] = y

    @pl.when(jnp.logical_not(used))
    def _():
        y_ref[...] = jnp.zeros_like(y_ref)


def _moe(hs, order_in, order_out, e_lo, e_hi, n_used, w_gate, w_up, w_down, n_tiles, tm):
    w = hs.shape[1]
    d = w - META_LANES
    f = w_gate.shape[2]
    lo3 = lambda i, oin, oout, elo, ehi, nu: (elo[i], 0, 0)
    hi3 = lambda i, oin, oout, elo, ehi, nu: (ehi[i], 0, 0)
    grid_spec = pltpu.PrefetchScalarGridSpec(
        num_scalar_prefetch=5,
        grid=(n_tiles,),
        in_specs=[
            pl.BlockSpec((tm, w), lambda i, oin, oout, elo, ehi, nu: (oin[i], 0)),
            pl.BlockSpec((1, d, f), lo3), pl.BlockSpec((1, d, f), hi3),
            pl.BlockSpec((1, d, f), lo3), pl.BlockSpec((1, d, f), hi3),
            pl.BlockSpec((1, f, d), lo3), pl.BlockSpec((1, f, d), hi3),
        ],
        out_specs=pl.BlockSpec((tm, d), lambda i, oin, oout, elo, ehi, nu: (oout[i], 0)),
    )
    return pl.pallas_call(
        _moe_kernel,
        name="moe_experts",
        grid_spec=grid_spec,
        out_shape=jax.ShapeDtypeStruct((n_tiles * tm, d), F32),
        compiler_params=pltpu.CompilerParams(dimension_semantics=("arbitrary",), vmem_limit_bytes=VMEM_LIMIT),
    )(order_in, order_out, e_lo, e_hi, n_used, hs, w_gate, w_gate, w_up, w_up, w_down, w_down)


def _combine_kernel(final_norm, pos_ref, x_ref, y_hbm, g_ref, o_ref, rows_ref, sem):
    tc = x_ref.shape[0]
    i = pl.program_id(0)
    n = pl.num_programs(0)

    def issue(tile, slot):
        base = tile * tc
        for r in range(tc):
            pltpu.make_async_copy(y_hbm.at[pl.ds(pos_ref[base + r], 1)], rows_ref.at[slot, pl.ds(r, 1)],
                                  sem.at[slot]).start()

    @pl.when(i == 0)
    def _():
        issue(0, 0)

    for slot in range(2):
        @pl.when(i % 2 == slot)
        def _():
            @pl.when(i + 1 < n)
            def _():
                issue(i + 1, 1 - slot)

            pltpu.make_async_copy(y_hbm.at[pl.ds(0, tc)], rows_ref.at[slot], sem.at[slot]).wait()
            x = x_ref[...] + rows_ref[slot]
            if final_norm:
                x = _rmsnorm(x, g_ref[...])
            o_ref[...] = x


def _combine(x, y_sorted, pos, g, final_norm):
    t, d = x.shape
    tc = COMBINE_TILE
    grid_spec = pltpu.PrefetchScalarGridSpec(
        num_scalar_prefetch=1,
        grid=(t // tc,),
        in_specs=[pl.BlockSpec((tc, d), lambda i, p: (i, 0)), pl.BlockSpec(memory_space=pl.ANY),
                  pl.BlockSpec((1, d), lambda i, p: (0, 0))],
        out_specs=pl.BlockSpec((tc, d), lambda i, p: (i, 0)),
        scratch_shapes=[pltpu.VMEM((2, tc, d), F32), pltpu.SemaphoreType.DMA((2,))],
    )
    return pl.pallas_call(
        functools.partial(_combine_kernel, final_norm),
        name="combine",
        grid_spec=grid_spec,
        out_shape=jax.ShapeDtypeStruct((t, d), F32),
        compiler_params=pltpu.CompilerParams(dimension_semantics=("arbitrary",), vmem_limit_bytes=VMEM_LIMIT),
    )(pos, x, y_sorted, g[None, :])


def kernel(x, mem, positions, norm_mix_g, w_in, b_branch_gate, ret_norm_g, w_ret_out, conv_w, conv_b, conv_ln_g, conv_ln_b, w_conv_out, w_mix_out, norm_xattn_g, norm_mem_g, w_xq, w_xkv, w_xo, norm_ffn_g, w_router, b_router, w_exp_gate, w_exp_up, w_exp_down, norm_final_g):
    b, s, d = x.shape
    depth = w_in.shape[0]
    t = b * s
    assert s % SEQ_TILE == 0 and SEQ_TILE % RET_CHUNK == 0
    assert t % MOE_TILE == 0 and t % COMBINE_TILE == 0
    tm = MOE_TILE
    n_tiles = t // tm + N_CLASSES
    cos, sin = _rope_tables(positions)
    n_exp, _, f = w_exp_gate.shape[1:]
    wg_all = w_exp_gate.astype(BF16).reshape(depth * n_exp, d, f)
    wu_all = w_exp_up.astype(BF16).reshape(depth * n_exp, d, f)
    wd_all = w_exp_down.astype(BF16).reshape(depth * n_exp, f, d)
    moe = None
    for l in range(depth):
        x = _mixer(x, cos, sin, norm_mix_g[l], w_in[l], b_branch_gate[l], ret_norm_g[l], w_ret_out[l],
                   conv_w[l], conv_b[l], conv_ln_g[l], conv_ln_b[l], w_conv_out[l], w_mix_out[l], moe)
        kv = _mem_kv(mem, norm_mem_g[l], w_xkv[l].astype(BF16))
        x, meta, alloc, hs = _xattn_route(x, kv, norm_xattn_g[l], w_xq[l], w_xo[l], norm_ffn_g[l],
                                          w_router, b_router, (n_tiles + 1) * tm)
        pos = meta[3].astype(jnp.int32)
        order_in, order_out, e_lo, e_hi, n_used = _tile_tables(alloc, n_tiles)
        y_sorted = _moe(hs, order_in, order_out, e_lo + l * n_exp, e_hi + l * n_exp, n_used, wg_all, wu_all, wd_all,
                        n_tiles, tm)
        moe = (y_sorted, pos)
    return _combine(x.reshape(t, d), y_sorted, pos, norm_final_g, True).reshape(b, s, d)
```

```python
import functools

import jax
import jax.numpy as jnp
import numpy as np
from jax import lax
from jax.experimental import pallas as pl
from jax.experimental.pallas import tpu as pltpu

F32 = jnp.float32
BF16 = jnp.bfloat16

RET_HEADS = 4
HEAD_DIM = 128
RET_WIDTH = RET_HEADS * HEAD_DIM
RET_CHUNK = 128
CONV_CH = 512
CONV_WIDTH = 31
CONV_HALO = 32
XATTN_HEADS = 4
XATTN_WIDTH = XATTN_HEADS * HEAD_DIM
N_EXPERTS = 16
N_GROUPS = 4
EXPERTS_PER_GROUP = 4
ROPE_BASE = 10000.0
EPS = 1e-6
QK_SCALE = HEAD_DIM ** -0.5

PAIRS = ((0, 1), (0, 2), (0, 3), (1, 2), (1, 3), (2, 3))
N_CLASSES = N_GROUPS * len(PAIRS)
CLASS_ROWS = 32
META_LANES = 128

SEQ_TILE = 256
MOE_TILE = 256
COMBINE_TILE = 256
VMEM_LIMIT = 56 * 1024 * 1024
N_DMA_PRIORITIES = 2


def _rmsnorm(x, g):
    return x * lax.rsqrt(jnp.mean(x * x, axis=-1, keepdims=True) + EPS) * g


def _sigmoid(x):
    return 1.0 / (1.0 + jnp.exp(-x))


def _dot(a, b):
    return jnp.dot(a, b, preferred_element_type=F32)


def _dot_nt(a, b):
    return lax.dot_general(a, b, (((1,), (1,)), ((), ())), preferred_element_type=F32)


def _dot_tn(a, b):
    return lax.dot_general(a, b, (((0,), (0,)), ((), ())), preferred_element_type=F32)


def _const_spec(shape):
    n = len(shape)
    return pl.BlockSpec(shape, lambda *_: (0,) * n)


def _rope_kernel(pos_ref, invf_ref, cos_ref, sin_ref):
    ang = pos_ref[...] * invf_ref[...]
    lane = lax.broadcasted_iota(jnp.int32, ang.shape, 1)
    s = jnp.sin(ang)
    cos_ref[...] = jnp.cos(ang)
    sin_ref[...] = jnp.where(lane < HEAD_DIM // 2, -s, s)


def _rope_tables(positions):
    s = positions.shape[0]
    half = HEAD_DIM // 2
    inv_freq = ROPE_BASE ** (-jnp.arange(half, dtype=F32) / half)
    invf = jnp.concatenate([inv_freq, inv_freq])[None, :]
    pos = positions.astype(F32)[:, None]
    ts = min(s, 1024)
    return pl.pallas_call(
        _rope_kernel,
        name="rope_tables",
        grid=(s // ts,),
        in_specs=[pl.BlockSpec((ts, 1), lambda i: (i, 0)), _const_spec((1, HEAD_DIM))],
        out_specs=[pl.BlockSpec((ts, HEAD_DIM), lambda i: (i, 0))] * 2,
        out_shape=[jax.ShapeDtypeStruct((s, HEAD_DIM), F32)] * 2,
    )(pos, invf)


def _memkv_kernel(mem_ref, g_ref, w_ref, kv_ref):
    m = _rmsnorm(mem_ref[0], g_ref[...]).astype(BF16)
    kv_ref[0] = _dot(m, w_ref[...]).astype(BF16)


def _mem_kv(mem, g, w_xkv):
    b, m, d = mem.shape
    return pl.pallas_call(
        _memkv_kernel,
        name="mem_kv",
        grid=(b,),
        in_specs=[pl.BlockSpec((1, m, d), lambda i: (i, 0, 0)), _const_spec((1, d)),
                  _const_spec((d, 2 * XATTN_WIDTH))],
        out_specs=pl.BlockSpec((1, m, 2 * XATTN_WIDTH), lambda i: (i, 0, 0)),
        out_shape=jax.ShapeDtypeStruct((b, m, 2 * XATTN_WIDTH), BF16),
        compiler_params=pltpu.CompilerParams(vmem_limit_bytes=VMEM_LIMIT),
    )(mem, g[None, :], w_xkv)


def _retention_tables():
    h = np.arange(RET_HEADS, dtype=np.float64)
    log_gamma = np.log(1.0 - 2.0 ** (-5.0 - h))
    idx = np.arange(RET_CHUNK, dtype=np.float64)
    diff = idx[:, None] - idx[None, :]
    decay = np.where(diff[None] >= 0.0, np.exp(np.maximum(diff, 0.0)[None] * log_gamma[:, None, None]), 0.0)
    zeta = np.exp((RET_CHUNK - 1.0 - idx)[None, :] * log_gamma[:, None])
    xi = np.exp((idx + 1.0)[None, :] * log_gamma[:, None])
    chunk_decay = np.exp(RET_CHUNK * log_gamma)
    bc = lambda t: np.broadcast_to(t[:, :, None], (RET_HEADS, RET_CHUNK, HEAD_DIM))
    return (jnp.asarray(decay, F32), jnp.asarray(bc(zeta), F32), jnp.asarray(bc(xi), F32),
            tuple(float(c) for c in chunk_decay))


def _gather_rows(idx_ref, tile, n, src_hbm, dst_ref, slot, sem):
    base = tile * n
    for r in range(n):
        pltpu.make_async_copy(src_hbm.at[pl.ds(idx_ref[base + r], 1)], dst_ref.at[slot, pl.ds(r, 1)],
                              sem.at[slot]).start(priority=r % N_DMA_PRIORITIES)


def _wait_gathered(n, src_hbm, dst_ref, slot, sem):
    pltpu.make_async_copy(src_hbm.at[pl.ds(0, n)], dst_ref.at[slot], sem.at[slot]).wait()


N_MIXER_IN = 17
N_XATTN_IN = 9
N_LAYER_OUT = 4
N_MIXER_SCRATCH = 4


def _layer_kernel(chunk_decay, add_moe, *refs):
    pos_ref = y_hbm = None
    if add_moe:
        pos_ref, refs = refs[0], refs[1:]
    mixer_in, refs = refs[:N_MIXER_IN], refs[N_MIXER_IN:]
    if add_moe:
        y_hbm, refs = refs[0], refs[1:]
    xattn_in, refs = refs[:N_XATTN_IN], refs[N_XATTN_IN:]
    outs, refs = refs[:N_LAYER_OUT], refs[N_LAYER_OUT:]
    n_ms = N_MIXER_SCRATCH + (2 if add_moe else 0)
    mixer_scratch, xattn_scratch = refs[:n_ms], refs[n_ms:]
    ts, d = mixer_in[0].shape[1], mixer_in[0].shape[2]
    mix = functools.partial(_mixer_compute, chunk_decay, pos_ref, y_hbm, mixer_in, mixer_scratch)
    _xattn_route_part(mix, ts, d, *xattn_in, *outs, *xattn_scratch)
    if add_moe:
        lin = pl.program_id(0) * pl.num_programs(1) + pl.program_id(1)

        @pl.when(lin == pl.num_programs(0) * pl.num_programs(1) - 1)
        def _():
            rows_ref, sem = mixer_scratch[N_MIXER_SCRATCH:]
            _wait_gathered(ts, y_hbm, rows_ref, 1 - lin % 2, sem)


def _mixer_compute(chunk_decay, pos_ref, y_hbm, in_refs, scratch, after_waits):
    add_moe = pos_ref is not None
    (x_ref, cos_ref, sin_ref, g_ref, win_ref, bg_ref, rg_ref, wro_ref, cw_ref, cb_ref, lng_ref, lnb_ref,
     wco_ref, wmo_ref, decay_ref, zeta_ref, xi_ref) = in_refs
    state_ref, zext_ref, phase_ref, u_ref = scratch[:N_MIXER_SCRATCH]
    ts = x_ref.shape[1]
    d = x_ref.shape[2]

    @pl.when(pl.program_id(1) == 0)
    def _():
        state_ref[...] = jnp.zeros_like(state_ref)
        zext_ref[0:CONV_HALO, :] = jnp.zeros((CONV_HALO, CONV_CH), F32)

    x = x_ref[0]
    if add_moe:
        rows_ref, sem = scratch[N_MIXER_SCRATCH:]
        lin = pl.program_id(0) * pl.num_programs(1) + pl.program_id(1)
        total = pl.num_programs(0) * pl.num_programs(1)
        slot = lin % 2

        @pl.when(lin == 0)
        def _():
            _gather_rows(pos_ref, 0, ts, y_hbm, rows_ref, 0, sem)

        _wait_gathered(ts, y_hbm, rows_ref, slot, sem)
        x = x + rows_ref[slot]
        _gather_rows(pos_ref, jnp.minimum(lin + 1, total - 1), ts, y_hbm, rows_ref, 1 - slot, sem)
    after_waits()
    h = _rmsnorm(x, g_ref[...]).astype(BF16)

    def proj(a, b):
        return _dot(h, win_ref[:, a:b])

    o_q, o_k, o_v, o_g = RET_WIDTH, 2 * RET_WIDTH, 3 * RET_WIDTH, 4 * RET_WIDTH
    o_c = o_g + 2 * CONV_CH

    ca, cg = proj(o_g, o_g + CONV_CH), proj(o_g + CONV_CH, o_c)
    zext_ref[CONV_HALO:CONV_HALO + ts, :] = ca * _sigmoid(cg)
    first = CONV_HALO - (CONV_WIDTH - 1)
    other_cols = [(c0, c0 + 256) for c0 in range(0, o_g, 256)] + [(c0, c0 + 256) for c0 in range(o_c, o_c + 2 * d, 256)]
    per_phase = len(other_cols) // 8
    acc = None
    for b in range(8):
        for k in range(b * per_phase, (b + 1) * per_phase):
            u_ref[:, 256 * k:256 * (k + 1)] = proj(*other_cols[k])
        rows = ts if b == 0 else ts + 8
        part = None
        for a in range(CONV_HALO // 8 + 1):
            w = 8 * a + b - first
            if 0 <= w < CONV_WIDTH:
                term = zext_ref[8 * a:8 * a + rows, :] * cw_ref[w:w + 1, :]
                part = term if part is None else part + term
        if b == 0:
            acc = part
        else:
            phase_ref[b - 1] = part
            acc = acc + phase_ref[b - 1, b:b + ts, :]
    acc = acc + cb_ref[...]
    zext_ref[0:CONV_HALO, :] = zext_ref[ts:ts + CONV_HALO, :]
    mu = jnp.mean(acc, axis=-1, keepdims=True)
    cen = acc - mu
    var = jnp.mean(cen * cen, axis=-1, keepdims=True)
    zf = cen * lax.rsqrt(var + EPS) * lng_ref[...] + lnb_ref[...]
    y_conv = _dot((zf * _sigmoid(zf)).astype(BF16), wco_ref[...])
    uq, uk, uv, sg = (u_ref[:, c0:c0 + RET_WIDTH] for c0 in (0, o_q, o_k, o_v))
    gate_pre = u_ref[:, o_g:o_g + 2 * d]

    cos, sin = cos_ref[...], sin_ref[...]
    heads = []
    for hd in range(RET_HEADS):
        sl = slice(hd * HEAD_DIM, (hd + 1) * HEAD_DIM)
        qh = uq[:, sl]
        qh = qh * cos + pltpu.roll(qh, HEAD_DIM // 2, 1) * sin
        kh = uk[:, sl]
        kh = (kh * cos + pltpu.roll(kh, HEAD_DIM // 2, 1) * sin) * QK_SCALE
        vh = uv[:, sl]
        rows = []
        for c in range(ts // RET_CHUNK):
            r = slice(c * RET_CHUNK, (c + 1) * RET_CHUNK)
            qc = qh[r].astype(BF16)
            kc = kh[r]
            vc = vh[r].astype(BF16)
            sc = _dot_nt(qc, kc.astype(BF16)) * decay_ref[hd]
            inner = _dot(sc.astype(BF16), vc)
            st = state_ref[hd]
            cross = _dot(qc, st.astype(BF16)) * xi_ref[hd]
            kz = (kc * zeta_ref[hd]).astype(BF16)
            state_ref[hd] = chunk_decay[hd] * st + _dot_tn(kz, vc)
            rows.append(inner + cross)
        rh = jnp.concatenate(rows, axis=0)
        heads.append(rh * lax.rsqrt(jnp.mean(rh * rh, axis=-1, keepdims=True) + EPS))
    r = jnp.concatenate(heads, axis=1) * rg_ref[...]
    r = (sg * _sigmoid(sg)) * r
    y_ret = _dot(r.astype(BF16), wro_ref[...])

    gate = _sigmoid(gate_pre + bg_ref[...])
    merged = (gate[:, 0:d] * y_ret + gate[:, d:2 * d] * y_conv).astype(BF16)
    return x + _dot(merged, wmo_ref[...])


def _layer(x, cos, sin, g, w_in, b_gate, ret_g, w_ret_out, conv_w, conv_b, ln_g, ln_b, w_conv_out, w_mix_out,
           kv, g_x, w_xq, w_xo, g_f, w_router, b_router, n_rows, moe=None):
    b, s, d = x.shape
    ts = SEQ_TILE
    assert ts <= MOE_TILE
    m = kv.shape[1]
    nt = s // ts
    w = d + META_LANES
    wr = jnp.zeros((d, META_LANES), F32).at[:, :N_EXPERTS].set(w_router)
    wr_hi = wr.astype(BF16)
    wr_lo = (wr - wr_hi.astype(F32)).astype(BF16)
    decay, zeta, xi, chunk_decay = _retention_tables()
    in_cols = w_in.shape[1]
    cw = jnp.zeros((CONV_HALO, CONV_CH), F32).at[:CONV_WIDTH].set(conv_w)
    row = lambda v: v[None, :]
    tab = (RET_HEADS, RET_CHUNK, HEAD_DIM)
    add_moe = moe is not None
    in_specs = [
        pl.BlockSpec((1, ts, d), lambda i, j, *_: (i, j, 0)),
        pl.BlockSpec((ts, HEAD_DIM), lambda i, j, *_: (j, 0)),
        pl.BlockSpec((ts, HEAD_DIM), lambda i, j, *_: (j, 0)),
        _const_spec((1, d)), _const_spec((d, in_cols)), _const_spec((1, 2 * d)),
        _const_spec((1, RET_WIDTH)), _const_spec((RET_WIDTH, d)),
        _const_spec((CONV_HALO, CONV_CH)), _const_spec((1, CONV_CH)), _const_spec((1, CONV_CH)),
        _const_spec((1, CONV_CH)), _const_spec((CONV_CH, d)), _const_spec((d, d)),
        _const_spec((RET_HEADS, RET_CHUNK, RET_CHUNK)), _const_spec(tab), _const_spec(tab),
    ]
    scratch = [pltpu.VMEM((RET_HEADS, HEAD_DIM, HEAD_DIM), F32),
               pltpu.VMEM((CONV_HALO + ts, CONV_CH), F32),
               pltpu.VMEM((7, ts + 8, CONV_CH), F32),
               pltpu.VMEM((ts, in_cols - 2 * CONV_CH), F32)]
    args = [x, cos, sin, row(g), w_in.astype(BF16), row(b_gate), row(ret_g), w_ret_out.astype(BF16), cw,
            row(conv_b), row(ln_g), row(ln_b), w_conv_out.astype(BF16), w_mix_out.astype(BF16), decay, zeta, xi]
    assert len(in_specs) == N_MIXER_IN and len(scratch) == N_MIXER_SCRATCH
    if add_moe:
        y_sorted, pos = moe
        in_specs.append(pl.BlockSpec(memory_space=pl.ANY))
        scratch += [pltpu.VMEM((2, ts, d), F32), pltpu.SemaphoreType.DMA((2,))]
        args = [pos] + args + [y_sorted]
    xattn_specs = [
        pl.BlockSpec((1, m, 2 * XATTN_WIDTH), lambda i, j, *_: (i, 0, 0)),
        _const_spec((1, d)), _const_spec((d, XATTN_WIDTH)), _const_spec((XATTN_WIDTH, d)),
        _const_spec((1, d)), _const_spec((d, META_LANES)), _const_spec((d, META_LANES)),
        _const_spec((N_EXPERTS, 1)),
        pl.BlockSpec(memory_space=pl.ANY),
    ]
    assert len(xattn_specs) == N_XATTN_IN
    args += [kv, row(g_x), w_xq.astype(BF16), w_xo.astype(BF16), row(g_f), wr_hi, wr_lo, b_router[:, None],
             jnp.zeros((n_rows, w), F32)]
    scratch += [pltpu.VMEM((CLASS_ROWS, 1), F32), pltpu.VMEM((CLASS_ROWS, 1), F32),
                pltpu.VMEM((1, 1), F32), pltpu.VMEM((2, ts, w), F32),
                pltpu.VMEM((8, ts), jnp.int32), pltpu.SMEM((1, ts), jnp.int32),
                pltpu.SemaphoreType.DMA((2,)), pltpu.SemaphoreType.DMA((1,))]
    grid_spec = pltpu.PrefetchScalarGridSpec(
        num_scalar_prefetch=1 if add_moe else 0,
        grid=(b, nt),
        in_specs=in_specs + xattn_specs,
        out_specs=[
            pl.BlockSpec((1, ts, d), lambda i, j, *_: (i, j, 0)),
            pl.BlockSpec((8, ts), lambda i, j, *_: (0, i * nt + j)),
            pl.BlockSpec((CLASS_ROWS, META_LANES), lambda i, j, *_: (i * nt + j, 0)),
            pl.BlockSpec(memory_space=pl.ANY),
        ],
        scratch_shapes=scratch,
    )
    return pl.pallas_call(
        functools.partial(_layer_kernel, chunk_decay, add_moe),
        name="layer",
        grid_spec=grid_spec,
        out_shape=[
            jax.ShapeDtypeStruct((b, s, d), F32),
            jax.ShapeDtypeStruct((8, b * s), F32),
            jax.ShapeDtypeStruct((b * nt * CLASS_ROWS, META_LANES), F32),
            jax.ShapeDtypeStruct((n_rows, w), F32),
        ],
        input_output_aliases={len(args) - 1: 3},
        compiler_params=pltpu.CompilerParams(dimension_semantics=("arbitrary", "arbitrary"),
                                             vmem_limit_bytes=VMEM_LIMIT),
    )(*args)


def _route(logits, bias):
    scores = _sigmoid(logits)
    sel = scores + bias
    one, zero = jnp.float32(1.0), jnp.float32(0.0)
    top, gscore = [], []
    for g in range(N_GROUPS):
        a = [sel[EXPERTS_PER_GROUP * g + i:EXPERTS_PER_GROUP * g + i + 1, :] for i in range(EXPERTS_PER_GROUP)]
        tg, sg = [], None
        for i in range(EXPERTS_PER_GROUP):
            rank = None
            for j in range(EXPERTS_PER_GROUP):
                if j == i:
                    continue
                ahead = (a[j] >= a[i]) if j < i else (a[j] > a[i])
                ahead = jnp.where(ahead, one, zero)
                rank = ahead if rank is None else rank + ahead
            in_top = rank < 2.0
            tg.append(in_top)
            contrib = jnp.where(in_top, a[i], zero)
            sg = contrib if sg is None else sg + contrib
        top.append(tg)
        gscore.append(sg)
    cls = jnp.zeros_like(gscore[0])
    w_lo = jnp.zeros_like(cls)
    w_hi = jnp.zeros_like(cls)
    for g in range(N_GROUPS):
        behind = None
        for g2 in range(N_GROUPS):
            if g2 == g:
                continue
            ahead = (gscore[g2] >= gscore[g]) if g2 < g else (gscore[g2] > gscore[g])
            behind = ahead if behind is None else (behind | ahead)
        best = jnp.logical_not(behind)
        for p, (i, j) in enumerate(PAIRS):
            active = best & top[g][i] & top[g][j]
            e_lo, e_hi = EXPERTS_PER_GROUP * g + i, EXPERTS_PER_GROUP * g + j
            cls = jnp.where(active, jnp.float32(len(PAIRS) * g + p), cls)
            w_lo = jnp.where(active, scores[e_lo:e_lo + 1, :], w_lo)
            w_hi = jnp.where(active, scores[e_hi:e_hi + 1, :], w_hi)
    den = w_lo + w_hi
    return cls, w_lo / den, w_hi / den


def _xattn_route_part(mixer, ts, d, kv_ref, gx_ref, wq_ref, wo_ref, gf_ref, wrh_ref, wrl_ref, br_ref,
                      init_hbm, o_ref, meta_ref, alloc_ref, hs_hbm,
                      fill_ref, cur_ref, next_ref, rows_ref, posv_ref, pos_smem, sem, sem_pos):
    del init_hbm
    tm = float(MOE_TILE)
    lin = pl.program_id(0) * pl.num_programs(1) + pl.program_id(1)
    total = pl.num_programs(0) * pl.num_programs(1)
    slot = lin % 2

    def wait_rows(sl):
        pltpu.make_async_copy(rows_ref.at[sl], hs_hbm.at[pl.ds(0, ts)], sem.at[sl]).wait()

    @pl.when(lin == 0)
    def _():
        fill_ref[...] = jnp.full(fill_ref.shape, tm, F32)
        cur_ref[...] = jnp.zeros_like(cur_ref)
        next_ref[...] = jnp.zeros_like(next_ref)
        rows_ref[1] = jnp.zeros(rows_ref.shape[1:], F32)
        for r in range(ts):
            pos_smem[0, r] = hs_hbm.shape[0] - ts + r

    def pos_to_smem():
        return pltpu.make_async_copy(posv_ref.at[pl.ds(0, 1)], pos_smem, sem_pos.at[0])

    def scatter_rows(sl):
        for r in range(ts):
            pltpu.make_async_copy(rows_ref.at[sl, pl.ds(r, 1)], hs_hbm.at[pl.ds(pos_smem[0, r], 1)],
                                  sem.at[sl]).start(priority=r % N_DMA_PRIORITIES)

    @pl.when(lin >= 1)
    def _():
        pos_to_smem().wait()

    x = mixer(lambda: scatter_rows(1 - slot))
    h = _rmsnorm(x, gx_ref[...]).astype(BF16)
    q = _dot(h, wq_ref[...])
    kv = kv_ref[0]
    heads = []
    for hd in range(XATTN_HEADS):
        sl = slice(hd * HEAD_DIM, (hd + 1) * HEAD_DIM)
        vs = slice(XATTN_WIDTH + hd * HEAD_DIM, XATTN_WIDTH + (hd + 1) * HEAD_DIM)
        sc = _dot_nt(q[:, sl].astype(BF16), kv[:, sl]) * QK_SCALE
        sc = sc - jnp.max(sc, axis=-1, keepdims=True)
        p = jnp.exp(sc)
        p = p / jnp.sum(p, axis=-1, keepdims=True)
        heads.append(_dot(p.astype(BF16), kv[:, vs]))
    att = jnp.concatenate(heads, axis=1).astype(BF16)
    x2 = x + _dot(att, wo_ref[...])
    o_ref[0] = x2

    hf = _rmsnorm(x2, gf_ref[...])
    h_hi = hf.astype(BF16)
    h_lo = (hf - h_hi.astype(F32)).astype(BF16)
    lg = _dot(h_hi, wrh_ref[...]) + (_dot(h_lo, wrh_ref[...]) + _dot(h_hi, wrl_ref[...]))
    logits = lg.T[0:N_EXPERTS, :]
    cls, w_lo, w_hi = _route(logits, br_ref[...])

    one, zero = jnp.float32(1.0), jnp.float32(0.0)
    crow = lax.broadcasted_iota(jnp.int32, (CLASS_ROWS, ts), 0).astype(F32)
    onehot = jnp.where(crow == cls, one, zero)
    before = lax.broadcasted_iota(jnp.int32, (ts, ts), 0) < lax.broadcasted_iota(jnp.int32, (ts, ts), 1)
    prefix = _dot(onehot.astype(BF16), jnp.where(before, one, zero).astype(BF16))
    count = jnp.sum(onehot, axis=1, keepdims=True)
    fill, cur, nxt = fill_ref[...], cur_ref[...], next_ref[...]
    need = jnp.where(fill + count > tm, one, zero)
    lower = (lax.broadcasted_iota(jnp.int32, (CLASS_ROWS, CLASS_ROWS), 1)
             < lax.broadcasted_iota(jnp.int32, (CLASS_ROWS, CLASS_ROWS), 0))
    opened_before = _dot(jnp.where(lower, one, zero).astype(BF16),
                         jnp.broadcast_to(need, (CLASS_ROWS, META_LANES)).astype(BF16))[:, 0:1]
    new_tile = nxt + opened_before
    per_token = lambda v: jnp.sum(onehot * v, axis=0, keepdims=True)
    slot_in_cur = per_token(fill) + per_token(prefix)
    pos = jnp.where(slot_in_cur >= tm, per_token(new_tile) * tm + (slot_in_cur - tm),
                    per_token(cur) * tm + slot_in_cur)
    fill_ref[...] = jnp.where(need > zero, fill + count - tm, fill + count)
    cur_ref[...] = jnp.where(need > zero, new_tile, cur)
    next_ref[...] = nxt + jnp.sum(need, axis=0, keepdims=True)
    lane = lax.broadcasted_iota(jnp.int32, (CLASS_ROWS, META_LANES), 1)
    alloc_ref[...] = jnp.where(lane == 0, need, jnp.where(lane == 1, new_tile, zero))

    meta = jnp.concatenate([cls, w_lo, w_hi, pos, jnp.zeros((META_LANES - 4, ts), F32)], axis=0)
    meta_ref[...] = meta[0:8, :]
    @pl.when(lin >= 1)
    def _():
        wait_rows(slot)

    rows_ref[slot, :, 0:d] = hf
    rows_ref[slot, :, d:d + META_LANES] = meta.T
    posv_ref[...] = jnp.broadcast_to(pos.astype(jnp.int32), posv_ref.shape)
    pos_to_smem().start()

    @pl.when(lin == total - 1)
    def _():
        pos_to_smem().wait()
        scatter_rows(slot)
        wait_rows(slot)
        wait_rows(1 - slot)


def _tile_tables(alloc, n_tiles):
    opened = alloc[:, 0] > 0.0
    tile_of = alloc[:, 1].astype(jnp.int32)
    cls_of = jnp.arange(alloc.shape[0], dtype=jnp.int32) % CLASS_ROWS
    tile_ids = jnp.arange(n_tiles, dtype=jnp.int32)
    hit = opened[None, :] & (tile_of[None, :] == tile_ids[:, None])
    tile_cls = jnp.sum(jnp.where(hit, cls_of[None, :], 0), axis=1)
    used = jnp.any(hit, axis=1)
    n_used = jnp.sum(used.astype(jnp.int32))
    key = jnp.where(used, tile_cls, N_CLASSES)
    ahead = (key[None, :] < key[:, None]) | ((key[None, :] == key[:, None]) & (tile_ids[None, :] < tile_ids[:, None]))
    rank = jnp.sum(ahead.astype(jnp.int32), axis=1)
    order = jnp.sum(jnp.where(rank[None, :] == tile_ids[:, None], tile_ids[None, :], 0), axis=1)
    order_in = order[jnp.minimum(tile_ids, n_used - 1)]
    step_cls = tile_cls[order_in]
    pair = step_cls % len(PAIRS)
    base = EXPERTS_PER_GROUP * (step_cls // len(PAIRS))
    lo_of = jnp.asarray([p[0] for p in PAIRS], jnp.int32)
    hi_of = jnp.asarray([p[1] for p in PAIRS], jnp.int32)
    i32 = lambda v: v.astype(jnp.int32)
    return i32(order_in), i32(order), i32(base + lo_of[pair]), i32(base + hi_of[pair]), i32(n_used.reshape(1))


def _moe_kernel(oin_ref, oout_ref, elo_ref, ehi_ref, nused_ref, rows_ref, wg_lo, wg_hi, wu_lo, wu_hi, wd_lo, wd_hi, y_ref):
    tm, d = y_ref.shape
    i = pl.program_id(0)
    used = i < nused_ref[0]

    @pl.when(used)
    def _():
        hb = rows_ref[:, 0:d].astype(BF16)
        y = None
        for col, wg, wu, wd in ((d + 1, wg_lo, wu_lo, wd_lo), (d + 2, wg_hi, wu_hi, wd_hi)):
            gate = _dot(hb, wg[0])
            act = (gate * _sigmoid(gate)) * _dot(hb, wu[0])
            part = rows_ref[:, col:col + 1] * _dot(act.astype(BF16), wd[0])
            y = part if y is None else y + part
        y_ref[...] = y

    @pl.when(jnp.logical_not(used))
    def _():
        y_ref[...] = jnp.zeros_like(y_ref)


def _moe(hs, order_in, order_out, e_lo, e_hi, n_used, w_gate, w_up, w_down, n_tiles, tm):
    w = hs.shape[1]
    d = w - META_LANES
    f = w_gate.shape[2]
    lo3 = lambda i, oin, oout, elo, ehi, nu: (elo[i], 0, 0)
    hi3 = lambda i, oin, oout, elo, ehi, nu: (ehi[i], 0, 0)
    grid_spec = pltpu.PrefetchScalarGridSpec(
        num_scalar_prefetch=5,
        grid=(n_tiles,),
        in_specs=[
            pl.BlockSpec((tm, w), lambda i, oin, oout, elo, ehi, nu: (oin[i], 0)),
            pl.BlockSpec((1, d, f), lo3), pl.BlockSpec((1, d, f), hi3),
            pl.BlockSpec((1, d, f), lo3), pl.BlockSpec((1, d, f), hi3),
            pl.BlockSpec((1, f, d), lo3), pl.BlockSpec((1, f, d), hi3),
        ],
        out_specs=pl.BlockSpec((tm, d), lambda i, oin, oout, elo, ehi, nu: (oout[i], 0)),
    )
    return pl.pallas_call(
        _moe_kernel,
        name="moe_experts",
        grid_spec=grid_spec,
        out_shape=jax.ShapeDtypeStruct((n_tiles * tm, d), F32),
        compiler_params=pltpu.CompilerParams(dimension_semantics=("arbitrary",), vmem_limit_bytes=VMEM_LIMIT),
    )(order_in, order_out, e_lo, e_hi, n_used, hs, w_gate, w_gate, w_up, w_up, w_down, w_down)


def _combine_kernel(final_norm, pos_ref, x_ref, y_hbm, g_ref, o_ref, rows_ref, sem):
    tc = x_ref.shape[0]
    i = pl.program_id(0)
    n = pl.num_programs(0)

    def issue(tile, slot):
        base = tile * tc
        for r in range(tc):
            pltpu.make_async_copy(y_hbm.at[pl.ds(pos_ref[base + r], 1)], rows_ref.at[slot, pl.ds(r, 1)],
                                  sem.at[slot]).start(priority=r % N_DMA_PRIORITIES)

    @pl.when(i == 0)
    def _():
        issue(0, 0)

    for slot in range(2):
        @pl.when(i % 2 == slot)
        def _():
            @pl.when(i + 1 < n)
            def _():
                issue(i + 1, 1 - slot)

            pltpu.make_async_copy(y_hbm.at[pl.ds(0, tc)], rows_ref.at[slot], sem.at[slot]).wait()
            x = x_ref[...] + rows_ref[slot]
            if final_norm:
                x = _rmsnorm(x, g_ref[...])
            o_ref[...] = x


def _combine(x, y_sorted, pos, g, final_norm):
    t, d = x.shape
    tc = COMBINE_TILE
    grid_spec = pltpu.PrefetchScalarGridSpec(
        num_scalar_prefetch=1,
        grid=(t // tc,),
        in_specs=[pl.BlockSpec((tc, d), lambda i, p: (i, 0)), pl.BlockSpec(memory_space=pl.ANY),
                  pl.BlockSpec((1, d), lambda i, p: (0, 0))],
        out_specs=pl.BlockSpec((tc, d), lambda i, p: (i, 0)),
        scratch_shapes=[pltpu.VMEM((2, tc, d), F32), pltpu.SemaphoreType.DMA((2,))],
    )
    return pl.pallas_call(
        functools.partial(_combine_kernel, final_norm),
        name="combine",
        grid_spec=grid_spec,
        out_shape=jax.ShapeDtypeStruct((t, d), F32),
        compiler_params=pltpu.CompilerParams(dimension_semantics=("arbitrary",), vmem_limit_bytes=VMEM_LIMIT),
    )(pos, x, y_sorted, g[None, :])


def kernel(x, mem, positions, norm_mix_g, w_in, b_branch_gate, ret_norm_g, w_ret_out, conv_w, conv_b, conv_ln_g, conv_ln_b, w_conv_out, w_mix_out, norm_xattn_g, norm_mem_g, w_xq, w_xkv, w_xo, norm_ffn_g, w_router, b_router, w_exp_gate, w_exp_up, w_exp_down, norm_final_g):
    b, s, d = x.shape
    depth = w_in.shape[0]
    t = b * s
    assert s % SEQ_TILE == 0 and SEQ_TILE % RET_CHUNK == 0
    assert t % MOE_TILE == 0 and t % COMBINE_TILE == 0
    tm = MOE_TILE
    n_tiles = t // tm + N_CLASSES
    cos, sin = _rope_tables(positions)
    n_exp, _, f = w_exp_gate.shape[1:]
    wg_all = w_exp_gate.astype(BF16).reshape(depth * n_exp, d, f)
    wu_all = w_exp_up.astype(BF16).reshape(depth * n_exp, d, f)
    wd_all = w_exp_down.astype(BF16).reshape(depth * n_exp, f, d)
    moe = None
    for l in range(depth):
        kv = _mem_kv(mem, norm_mem_g[l], w_xkv[l].astype(BF16))
        x, meta, alloc, hs = _layer(x, cos, sin, norm_mix_g[l], w_in[l], b_branch_gate[l], ret_norm_g[l], w_ret_out[l],
                                    conv_w[l], conv_b[l], conv_ln_g[l], conv_ln_b[l], w_conv_out[l], w_mix_out[l],
                                    kv, norm_xattn_g[l], w_xq[l], w_xo[l], norm_ffn_g[l], w_router, b_router,
                                    (n_tiles + 1) * tm, moe)
        pos = meta[3].astype(jnp.int32)
        order_in, order_out, e_lo, e_hi, n_used = _tile_tables(alloc, n_tiles)
        y_sorted = _moe(hs, order_in, order_out, e_lo + l * n_exp, e_hi + l * n_exp, n_used, wg_all, wu_all, wd_all,
                        n_tiles, tm)
        moe = (y_sorted, pos)
    return _combine(x.reshape(t, d), y_sorted, pos, norm_final_g, True).reshape(b, s, d)
```

```python
import functools

import jax
import jax.numpy as jnp
import numpy as np
from jax import lax
from jax.experimental import pallas as pl
from jax.experimental.pallas import tpu as pltpu

F32 = jnp.float32
BF16 = jnp.bfloat16

RET_HEADS = 4
HEAD_DIM = 128
RET_WIDTH = RET_HEADS * HEAD_DIM
RET_CHUNK = 128
CONV_CH = 512
CONV_WIDTH = 31
CONV_HALO = 32
XATTN_HEADS = 4
XATTN_WIDTH = XATTN_HEADS * HEAD_DIM
N_EXPERTS = 16
N_GROUPS = 4
EXPERTS_PER_GROUP = 4
ROPE_BASE = 10000.0
EPS = 1e-6
QK_SCALE = HEAD_DIM ** -0.5

PAIRS = ((0, 1), (0, 2), (0, 3), (1, 2), (1, 3), (2, 3))
N_CLASSES = N_GROUPS * len(PAIRS)
CLASS_ROWS = 32
META_LANES = 128

SEQ_TILE = 256
MOE_TILE = 256
COMBINE_TILE = 256
VMEM_LIMIT = 56 * 1024 * 1024
N_DMA_PRIORITIES = 1


def _rmsnorm(x, g):
    return x * lax.rsqrt(jnp.mean(x * x, axis=-1, keepdims=True) + EPS) * g


def _sigmoid(x):
    return 1.0 / (1.0 + jnp.exp(-x))


def _dot(a, b):
    return jnp.dot(a, b, preferred_element_type=F32)


def _dot_nt(a, b):
    return lax.dot_general(a, b, (((1,), (1,)), ((), ())), preferred_element_type=F32)


def _dot_tn(a, b):
    return lax.dot_general(a, b, (((0,), (0,)), ((), ())), preferred_element_type=F32)


def _const_spec(shape):
    n = len(shape)
    return pl.BlockSpec(shape, lambda *_: (0,) * n)


def _rope_kernel(pos_ref, invf_ref, cos_ref, sin_ref):
    ang = pos_ref[...] * invf_ref[...]
    lane = lax.broadcasted_iota(jnp.int32, ang.shape, 1)
    s = jnp.sin(ang)
    cos_ref[...] = jnp.cos(ang)
    sin_ref[...] = jnp.where(lane < HEAD_DIM // 2, -s, s)


def _rope_tables(positions):
    s = positions.shape[0]
    half = HEAD_DIM // 2
    inv_freq = ROPE_BASE ** (-jnp.arange(half, dtype=F32) / half)
    invf = jnp.concatenate([inv_freq, inv_freq])[None, :]
    pos = positions.astype(F32)[:, None]
    ts = min(s, 1024)
    return pl.pallas_call(
        _rope_kernel,
        name="rope_tables",
        grid=(s // ts,),
        in_specs=[pl.BlockSpec((ts, 1), lambda i: (i, 0)), _const_spec((1, HEAD_DIM))],
        out_specs=[pl.BlockSpec((ts, HEAD_DIM), lambda i: (i, 0))] * 2,
        out_shape=[jax.ShapeDtypeStruct((s, HEAD_DIM), F32)] * 2,
    )(pos, invf)


def _memkv_kernel(mem_ref, g_ref, w_ref, kv_ref):
    m = _rmsnorm(mem_ref[0], g_ref[...]).astype(BF16)
    kv_ref[0] = _dot(m, w_ref[...]).astype(BF16)


def _mem_kv(mem, g, w_xkv):
    b, m, d = mem.shape
    return pl.pallas_call(
        _memkv_kernel,
        name="mem_kv",
        grid=(b,),
        in_specs=[pl.BlockSpec((1, m, d), lambda i: (i, 0, 0)), _const_spec((1, d)),
                  _const_spec((d, 2 * XATTN_WIDTH))],
        out_specs=pl.BlockSpec((1, m, 2 * XATTN_WIDTH), lambda i: (i, 0, 0)),
        out_shape=jax.ShapeDtypeStruct((b, m, 2 * XATTN_WIDTH), BF16),
        compiler_params=pltpu.CompilerParams(vmem_limit_bytes=VMEM_LIMIT),
    )(mem, g[None, :], w_xkv)


def _retention_tables():
    h = np.arange(RET_HEADS, dtype=np.float64)
    log_gamma = np.log(1.0 - 2.0 ** (-5.0 - h))
    idx = np.arange(RET_CHUNK, dtype=np.float64)
    diff = idx[:, None] - idx[None, :]
    decay = np.where(diff[None] >= 0.0, np.exp(np.maximum(diff, 0.0)[None] * log_gamma[:, None, None]), 0.0)
    zeta = np.exp((RET_CHUNK - 1.0 - idx)[None, :] * log_gamma[:, None])
    xi = np.exp((idx + 1.0)[None, :] * log_gamma[:, None])
    chunk_decay = np.exp(RET_CHUNK * log_gamma)
    bc = lambda t: np.broadcast_to(t[:, :, None], (RET_HEADS, RET_CHUNK, HEAD_DIM))
    return (jnp.asarray(decay, F32), jnp.asarray(bc(zeta), F32), jnp.asarray(bc(xi), F32),
            tuple(float(c) for c in chunk_decay))


def _gather_rows(idx_ref, tile, n, src_hbm, dst_ref, slot, sem):
    base = tile * n
    for r in range(n):
        pltpu.make_async_copy(src_hbm.at[pl.ds(idx_ref[base + r], 1)], dst_ref.at[slot, pl.ds(r, 1)],
                              sem.at[slot]).start(priority=r % N_DMA_PRIORITIES)


def _wait_gathered(n, src_hbm, dst_ref, slot, sem):
    pltpu.make_async_copy(src_hbm.at[pl.ds(0, n)], dst_ref.at[slot], sem.at[slot]).wait()


N_MIXER_IN = 17
N_XATTN_IN = 9
N_LAYER_OUT = 4
N_MIXER_SCRATCH = 4


def _layer_kernel(chunk_decay, add_moe, *refs):
    pos_ref = y_hbm = None
    if add_moe:
        pos_ref, refs = refs[0], refs[1:]
    mixer_in, refs = refs[:N_MIXER_IN], refs[N_MIXER_IN:]
    if add_moe:
        y_hbm, refs = refs[0], refs[1:]
    xattn_in, refs = refs[:N_XATTN_IN], refs[N_XATTN_IN:]
    outs, refs = refs[:N_LAYER_OUT], refs[N_LAYER_OUT:]
    n_ms = N_MIXER_SCRATCH + (2 if add_moe else 0)
    mixer_scratch, xattn_scratch = refs[:n_ms], refs[n_ms:]
    ts, d = mixer_in[0].shape[1], mixer_in[0].shape[2]
    mix = functools.partial(_mixer_compute, chunk_decay, pos_ref, y_hbm, mixer_in, mixer_scratch)
    _xattn_route_part(mix, ts, d, *xattn_in, *outs, *xattn_scratch)
    if add_moe:
        lin = pl.program_id(0) * pl.num_programs(1) + pl.program_id(1)

        @pl.when(lin == pl.num_programs(0) * pl.num_programs(1) - 1)
        def _():
            rows_ref, sem = mixer_scratch[N_MIXER_SCRATCH:]
            _wait_gathered(ts, y_hbm, rows_ref, 1 - lin % 2, sem)


def _mixer_compute(chunk_decay, pos_ref, y_hbm, in_refs, scratch, after_waits):
    add_moe = pos_ref is not None
    (x_ref, cos_ref, sin_ref, g_ref, win_ref, bg_ref, rg_ref, wro_ref, cw_ref, cb_ref, lng_ref, lnb_ref,
     wco_ref, wmo_ref, decay_ref, zeta_ref, xi_ref) = in_refs
    state_ref, zext_ref, phase_ref, u_ref = scratch[:N_MIXER_SCRATCH]
    ts = x_ref.shape[1]
    d = x_ref.shape[2]

    @pl.when(pl.program_id(1) == 0)
    def _():
        state_ref[...] = jnp.zeros_like(state_ref)
        zext_ref[0:CONV_HALO, :] = jnp.zeros((CONV_HALO, CONV_CH), F32)

    x = x_ref[0]
    if add_moe:
        rows_ref, sem = scratch[N_MIXER_SCRATCH:]
        lin = pl.program_id(0) * pl.num_programs(1) + pl.program_id(1)
        total = pl.num_programs(0) * pl.num_programs(1)
        slot = lin % 2

        @pl.when(lin == 0)
        def _():
            _gather_rows(pos_ref, 0, ts, y_hbm, rows_ref, 0, sem)

        _wait_gathered(ts, y_hbm, rows_ref, slot, sem)
        x = x + rows_ref[slot]
        _gather_rows(pos_ref, jnp.minimum(lin + 1, total - 1), ts, y_hbm, rows_ref, 1 - slot, sem)
    after_waits()
    h = _rmsnorm(x, g_ref[...]).astype(BF16)

    def proj(a, b):
        return _dot(h, win_ref[:, a:b])

    o_q, o_k, o_v, o_g = RET_WIDTH, 2 * RET_WIDTH, 3 * RET_WIDTH, 4 * RET_WIDTH
    o_c = o_g + 2 * CONV_CH

    ca, cg = proj(o_g, o_g + CONV_CH), proj(o_g + CONV_CH, o_c)
    zext_ref[CONV_HALO:CONV_HALO + ts, :] = ca * _sigmoid(cg)
    first = CONV_HALO - (CONV_WIDTH - 1)
    other_cols = [(c0, c0 + 256) for c0 in range(0, o_g, 256)] + [(c0, c0 + 256) for c0 in range(o_c, o_c + 2 * d, 256)]
    per_phase = len(other_cols) // 8
    acc = None
    for b in range(8):
        for k in range(b * per_phase, (b + 1) * per_phase):
            u_ref[:, 256 * k:256 * (k + 1)] = proj(*other_cols[k])
        rows = ts if b == 0 else ts + 8
        part = None
        for a in range(CONV_HALO // 8 + 1):
            w = 8 * a + b - first
            if 0 <= w < CONV_WIDTH:
                term = zext_ref[8 * a:8 * a + rows, :] * cw_ref[w:w + 1, :]
                part = term if part is None else part + term
        if b == 0:
            acc = part
        else:
            phase_ref[b - 1] = part
            acc = acc + phase_ref[b - 1, b:b + ts, :]
    acc = acc + cb_ref[...]
    zext_ref[0:CONV_HALO, :] = zext_ref[ts:ts + CONV_HALO, :]
    mu = jnp.mean(acc, axis=-1, keepdims=True)
    cen = acc - mu
    var = jnp.mean(cen * cen, axis=-1, keepdims=True)
    zf = cen * lax.rsqrt(var + EPS) * lng_ref[...] + lnb_ref[...]
    y_conv = _dot((zf * _sigmoid(zf)).astype(BF16), wco_ref[...])
    uq, uk, uv, sg = (u_ref[:, c0:c0 + RET_WIDTH] for c0 in (0, o_q, o_k, o_v))
    gate_pre = u_ref[:, o_g:o_g + 2 * d]

    cos, sin = cos_ref[...], sin_ref[...]
    heads = []
    for hd in range(RET_HEADS):
        sl = slice(hd * HEAD_DIM, (hd + 1) * HEAD_DIM)
        qh = uq[:, sl]
        qh = qh * cos + pltpu.roll(qh, HEAD_DIM // 2, 1) * sin
        kh = uk[:, sl]
        kh = (kh * cos + pltpu.roll(kh, HEAD_DIM // 2, 1) * sin) * QK_SCALE
        vh = uv[:, sl]
        rows = []
        for c in range(ts // RET_CHUNK):
            r = slice(c * RET_CHUNK, (c + 1) * RET_CHUNK)
            qc = qh[r].astype(BF16)
            kc = kh[r]
            vc = vh[r].astype(BF16)
            sc = _dot_nt(qc, kc.astype(BF16)) * decay_ref[hd]
            inner = _dot(sc.astype(BF16), vc)
            st = state_ref[hd]
            cross = _dot(qc, st.astype(BF16)) * xi_ref[hd]
            kz = (kc * zeta_ref[hd]).astype(BF16)
            state_ref[hd] = chunk_decay[hd] * st + _dot_tn(kz, vc)
            rows.append(inner + cross)
        rh = jnp.concatenate(rows, axis=0)
        heads.append(rh * lax.rsqrt(jnp.mean(rh * rh, axis=-1, keepdims=True) + EPS))
    r = jnp.concatenate(heads, axis=1) * rg_ref[...]
    r = (sg * _sigmoid(sg)) * r
    y_ret = _dot(r.astype(BF16), wro_ref[...])

    gate = _sigmoid(gate_pre + bg_ref[...])
    merged = (gate[:, 0:d] * y_ret + gate[:, d:2 * d] * y_conv).astype(BF16)
    return x + _dot(merged, wmo_ref[...])


def _layer(x, cos, sin, g, w_in, b_gate, ret_g, w_ret_out, conv_w, conv_b, ln_g, ln_b, w_conv_out, w_mix_out,
           kv, g_x, w_xq, w_xo, g_f, w_router, b_router, n_rows, moe=None):
    b, s, d = x.shape
    ts = SEQ_TILE
    assert ts <= MOE_TILE
    m = kv.shape[1]
    nt = s // ts
    w = d + META_LANES
    wr = jnp.zeros((d, META_LANES), F32).at[:, :N_EXPERTS].set(w_router)
    wr_hi = wr.astype(BF16)
    wr_lo = (wr - wr_hi.astype(F32)).astype(BF16)
    decay, zeta, xi, chunk_decay = _retention_tables()
    in_cols = w_in.shape[1]
    cw = jnp.zeros((CONV_HALO, CONV_CH), F32).at[:CONV_WIDTH].set(conv_w)
    row = lambda v: v[None, :]
    tab = (RET_HEADS, RET_CHUNK, HEAD_DIM)
    add_moe = moe is not None
    in_specs = [
        pl.BlockSpec((1, ts, d), lambda i, j, *_: (i, j, 0)),
        pl.BlockSpec((ts, HEAD_DIM), lambda i, j, *_: (j, 0)),
        pl.BlockSpec((ts, HEAD_DIM), lambda i, j, *_: (j, 0)),
        _const_spec((1, d)), _const_spec((d, in_cols)), _const_spec((1, 2 * d)),
        _const_spec((1, RET_WIDTH)), _const_spec((RET_WIDTH, d)),
        _const_spec((CONV_HALO, CONV_CH)), _const_spec((1, CONV_CH)), _const_spec((1, CONV_CH)),
        _const_spec((1, CONV_CH)), _const_spec((CONV_CH, d)), _const_spec((d, d)),
        _const_spec((RET_HEADS, RET_CHUNK, RET_CHUNK)), _const_spec(tab), _const_spec(tab),
    ]
    scratch = [pltpu.VMEM((RET_HEADS, HEAD_DIM, HEAD_DIM), F32),
               pltpu.VMEM((CONV_HALO + ts, CONV_CH), F32),
               pltpu.VMEM((7, ts + 8, CONV_CH), F32),
               pltpu.VMEM((ts, in_cols - 2 * CONV_CH), F32)]
    args = [x, cos, sin, row(g), w_in.astype(BF16), row(b_gate), row(ret_g), w_ret_out.astype(BF16), cw,
            row(conv_b), row(ln_g), row(ln_b), w_conv_out.astype(BF16), w_mix_out.astype(BF16), decay, zeta, xi]
    assert len(in_specs) == N_MIXER_IN and len(scratch) == N_MIXER_SCRATCH
    if add_moe:
        y_sorted, pos = moe
        in_specs.append(pl.BlockSpec(memory_space=pl.ANY))
        scratch += [pltpu.VMEM((2, ts, d), F32), pltpu.SemaphoreType.DMA((2,))]
        args = [pos] + args + [y_sorted]
    xattn_specs = [
        pl.BlockSpec((1, m, 2 * XATTN_WIDTH), lambda i, j, *_: (i, 0, 0)),
        _const_spec((1, d)), _const_spec((d, XATTN_WIDTH)), _const_spec((XATTN_WIDTH, d)),
        _const_spec((1, d)), _const_spec((d, META_LANES)), _const_spec((d, META_LANES)),
        _const_spec((N_EXPERTS, 1)),
        pl.BlockSpec(memory_space=pl.ANY),
    ]
    assert len(xattn_specs) == N_XATTN_IN
    args += [kv, row(g_x), w_xq.astype(BF16), w_xo.astype(BF16), row(g_f), wr_hi, wr_lo, b_router[:, None],
             jnp.zeros((n_rows, w), F32)]
    scratch += [pltpu.VMEM((CLASS_ROWS, 1), F32), pltpu.VMEM((CLASS_ROWS, 1), F32),
                pltpu.VMEM((1, 1), F32), pltpu.VMEM((2, ts, w), F32),
                pltpu.VMEM((8, ts), jnp.int32), pltpu.SMEM((1, ts), jnp.int32),
                pltpu.SemaphoreType.DMA((2,)), pltpu.SemaphoreType.DMA((1,))]
    grid_spec = pltpu.PrefetchScalarGridSpec(
        num_scalar_prefetch=1 if add_moe else 0,
        grid=(b, nt),
        in_specs=in_specs + xattn_specs,
        out_specs=[
            pl.BlockSpec((1, ts, d), lambda i, j, *_: (i, j, 0)),
            pl.BlockSpec((8, ts), lambda i, j, *_: (0, i * nt + j)),
            pl.BlockSpec((CLASS_ROWS, META_LANES), lambda i, j, *_: (i * nt + j, 0)),
            pl.BlockSpec(memory_space=pl.ANY),
        ],
        scratch_shapes=scratch,
    )
    return pl.pallas_call(
        functools.partial(_layer_kernel, chunk_decay, add_moe),
        name="layer",
        grid_spec=grid_spec,
        out_shape=[
            jax.ShapeDtypeStruct((b, s, d), F32),
            jax.ShapeDtypeStruct((8, b * s), F32),
            jax.ShapeDtypeStruct((b * nt * CLASS_ROWS, META_LANES), F32),
            jax.ShapeDtypeStruct((n_rows, w), F32),
        ],
        input_output_aliases={len(args) - 1: 3},
        compiler_params=pltpu.CompilerParams(dimension_semantics=("arbitrary", "arbitrary"),
                                             vmem_limit_bytes=VMEM_LIMIT),
    )(*args)


def _route(logits, bias):
    scores = _sigmoid(logits)
    sel = scores + bias
    one, zero = jnp.float32(1.0), jnp.float32(0.0)
    top, gscore = [], []
    for g in range(N_GROUPS):
        a = [sel[EXPERTS_PER_GROUP * g + i:EXPERTS_PER_GROUP * g + i + 1, :] for i in range(EXPERTS_PER_GROUP)]
        tg, sg = [], None
        for i in range(EXPERTS_PER_GROUP):
            rank = None
            for j in range(EXPERTS_PER_GROUP):
                if j == i:
                    continue
                ahead = (a[j] >= a[i]) if j < i else (a[j] > a[i])
                ahead = jnp.where(ahead, one, zero)
                rank = ahead if rank is None else rank + ahead
            in_top = rank < 2.0
            tg.append(in_top)
            contrib = jnp.where(in_top, a[i], zero)
            sg = contrib if sg is None else sg + contrib
        top.append(tg)
        gscore.append(sg)
    cls = jnp.zeros_like(gscore[0])
    w_lo = jnp.zeros_like(cls)
    w_hi = jnp.zeros_like(cls)
    for g in range(N_GROUPS):
        behind = None
        for g2 in range(N_GROUPS):
            if g2 == g:
                continue
            ahead = (gscore[g2] >= gscore[g]) if g2 < g else (gscore[g2] > gscore[g])
            behind = ahead if behind is None else (behind | ahead)
        best = jnp.logical_not(behind)
        for p, (i, j) in enumerate(PAIRS):
            active = best & top[g][i] & top[g][j]
            e_lo, e_hi = EXPERTS_PER_GROUP * g + i, EXPERTS_PER_GROUP * g + j
            cls = jnp.where(active, jnp.float32(len(PAIRS) * g + p), cls)
            w_lo = jnp.where(active, scores[e_lo:e_lo + 1, :], w_lo)
            w_hi = jnp.where(active, scores[e_hi:e_hi + 1, :], w_hi)
    den = w_lo + w_hi
    return cls, w_lo / den, w_hi / den


def _xattn_route_part(mixer, ts, d, kv_ref, gx_ref, wq_ref, wo_ref, gf_ref, wrh_ref, wrl_ref, br_ref,
                      init_hbm, o_ref, meta_ref, alloc_ref, hs_hbm,
                      fill_ref, cur_ref, next_ref, rows_ref, posv_ref, pos_smem, sem, sem_pos):
    del init_hbm
    tm = float(MOE_TILE)
    lin = pl.program_id(0) * pl.num_programs(1) + pl.program_id(1)
    total = pl.num_programs(0) * pl.num_programs(1)
    slot = lin % 2

    def wait_rows(sl):
        pltpu.make_async_copy(rows_ref.at[sl], hs_hbm.at[pl.ds(0, ts)], sem.at[sl]).wait()

    @pl.when(lin == 0)
    def _():
        fill_ref[...] = jnp.full(fill_ref.shape, tm, F32)
        cur_ref[...] = jnp.zeros_like(cur_ref)
        next_ref[...] = jnp.zeros_like(next_ref)
        rows_ref[1] = jnp.zeros(rows_ref.shape[1:], F32)
        for r in range(ts):
            pos_smem[0, r] = hs_hbm.shape[0] - ts + r

    def pos_to_smem():
        return pltpu.make_async_copy(posv_ref.at[pl.ds(0, 1)], pos_smem, sem_pos.at[0])

    def scatter_rows(sl):
        for r in range(ts):
            pltpu.make_async_copy(rows_ref.at[sl, pl.ds(r, 1)], hs_hbm.at[pl.ds(pos_smem[0, r], 1)],
                                  sem.at[sl]).start(priority=r % N_DMA_PRIORITIES)

    @pl.when(lin >= 1)
    def _():
        pos_to_smem().wait()

    x = mixer(lambda: scatter_rows(1 - slot))
    h = _rmsnorm(x, gx_ref[...]).astype(BF16)
    q = _dot(h, wq_ref[...])
    kv = kv_ref[0]
    heads = []
    for hd in range(XATTN_HEADS):
        sl = slice(hd * HEAD_DIM, (hd + 1) * HEAD_DIM)
        vs = slice(XATTN_WIDTH + hd * HEAD_DIM, XATTN_WIDTH + (hd + 1) * HEAD_DIM)
        sc = _dot_nt(q[:, sl].astype(BF16), kv[:, sl]) * QK_SCALE
        sc = sc - jnp.max(sc, axis=-1, keepdims=True)
        p = jnp.exp(sc)
        p = p / jnp.sum(p, axis=-1, keepdims=True)
        heads.append(_dot(p.astype(BF16), kv[:, vs]))
    att = jnp.concatenate(heads, axis=1).astype(BF16)
    x2 = x + _dot(att, wo_ref[...])
    o_ref[0] = x2

    hf = _rmsnorm(x2, gf_ref[...])
    h_hi = hf.astype(BF16)
    h_lo = (hf - h_hi.astype(F32)).astype(BF16)
    lg = _dot(h_hi, wrh_ref[...]) + (_dot(h_lo, wrh_ref[...]) + _dot(h_hi, wrl_ref[...]))
    logits = lg.T[0:N_EXPERTS, :]
    cls, w_lo, w_hi = _route(logits, br_ref[...])

    one, zero = jnp.float32(1.0), jnp.float32(0.0)
    crow = lax.broadcasted_iota(jnp.int32, (CLASS_ROWS, ts), 0).astype(F32)
    onehot = jnp.where(crow == cls, one, zero)
    before = lax.broadcasted_iota(jnp.int32, (ts, ts), 0) < lax.broadcasted_iota(jnp.int32, (ts, ts), 1)
    prefix = _dot(onehot.astype(BF16), jnp.where(before, one, zero).astype(BF16))
    count = jnp.sum(onehot, axis=1, keepdims=True)
    fill, cur, nxt = fill_ref[...], cur_ref[...], next_ref[...]
    need = jnp.where(fill + count > tm, one, zero)
    lower = (lax.broadcasted_iota(jnp.int32, (CLASS_ROWS, CLASS_ROWS), 1)
             < lax.broadcasted_iota(jnp.int32, (CLASS_ROWS, CLASS_ROWS), 0))
    opened_before = _dot(jnp.where(lower, one, zero).astype(BF16),
                         jnp.broadcast_to(need, (CLASS_ROWS, META_LANES)).astype(BF16))[:, 0:1]
    new_tile = nxt + opened_before
    per_token = lambda v: jnp.sum(onehot * v, axis=0, keepdims=True)
    slot_in_cur = per_token(fill) + per_token(prefix)
    pos = jnp.where(slot_in_cur >= tm, per_token(new_tile) * tm + (slot_in_cur - tm),
                    per_token(cur) * tm + slot_in_cur)
    fill_ref[...] = jnp.where(need > zero, fill + count - tm, fill + count)
    cur_ref[...] = jnp.where(need > zero, new_tile, cur)
    next_ref[...] = nxt + jnp.sum(need, axis=0, keepdims=True)
    lane = lax.broadcasted_iota(jnp.int32, (CLASS_ROWS, META_LANES), 1)
    alloc_ref[...] = jnp.where(lane == 0, need, jnp.where(lane == 1, new_tile, zero))

    meta = jnp.concatenate([cls, w_lo, w_hi, pos, jnp.zeros((META_LANES - 4, ts), F32)], axis=0)
    meta_ref[...] = meta[0:8, :]
    @pl.when(lin >= 1)
    def _():
        wait_rows(slot)

    rows_ref[slot, :, 0:d] = hf
    rows_ref[slot, :, d:d + META_LANES] = meta.T
    posv_ref[...] = jnp.broadcast_to(pos.astype(jnp.int32), posv_ref.shape)
    pos_to_smem().start()

    @pl.when(lin == total - 1)
    def _():
        pos_to_smem().wait()
        scatter_rows(slot)
        wait_rows(slot)
        wait_rows(1 - slot)


def _tile_tables(alloc, n_tiles):
    opened = alloc[:, 0] > 0.0
    tile_of = alloc[:, 1].astype(jnp.int32)
    cls_of = jnp.arange(alloc.shape[0], dtype=jnp.int32) % CLASS_ROWS
    tile_ids = jnp.arange(n_tiles, dtype=jnp.int32)
    hit = opened[None, :] & (tile_of[None, :] == tile_ids[:, None])
    tile_cls = jnp.sum(jnp.where(hit, cls_of[None, :], 0), axis=1)
    used = jnp.any(hit, axis=1)
    n_used = jnp.sum(used.astype(jnp.int32))
    key = jnp.where(used, tile_cls, N_CLASSES)
    ahead = (key[None, :] < key[:, None]) | ((key[None, :] == key[:, None]) & (tile_ids[None, :] < tile_ids[:, None]))
    rank = jnp.sum(ahead.astype(jnp.int32), axis=1)
    order = jnp.sum(jnp.where(rank[None, :] == tile_ids[:, None], tile_ids[None, :], 0), axis=1)
    order_in = order[jnp.minimum(tile_ids, n_used - 1)]
    step_cls = tile_cls[order_in]
    pair = step_cls % len(PAIRS)
    base = EXPERTS_PER_GROUP * (step_cls // len(PAIRS))
    lo_of = jnp.asarray([p[0] for p in PAIRS], jnp.int32)
    hi_of = jnp.asarray([p[1] for p in PAIRS], jnp.int32)
    i32 = lambda v: v.astype(jnp.int32)
    return i32(order_in), i32(order), i32(base + lo_of[pair]), i32(base + hi_of[pair]), i32(n_used.reshape(1))


def _moe_kernel(oin_ref, oout_ref, elo_ref, ehi_ref, nused_ref, rows_ref, wg_lo, wg_hi, wu_lo, wu_hi, wd_lo, wd_hi, y_ref):
    tm, d = y_ref.shape
    i = pl.program_id(0)
    used = i < nused_ref[0]

    @pl.when(used)
    def _():
        hb = rows_ref[:, 0:d].astype(BF16)
        y = None
        for col, wg, wu, wd in ((d + 1, wg_lo, wu_lo, wd_lo), (d + 2, wg_hi, wu_hi, wd_hi)):
            gate = _dot(hb, wg[0])
            act = (gate * _sigmoid(gate)) * _dot(hb, wu[0])
            part = rows_ref[:, col:col + 1] * _dot(act.astype(BF16), wd[0])
            y = part if y is None else y + part
        y_ref[...] = y

    @pl.when(jnp.logical_not(used))
    def _():
        y_ref[...] = jnp.zeros_like(y_ref)


def _moe(hs, order_in, order_out, e_lo, e_hi, n_used, w_gate, w_up, w_down, n_tiles, tm):
    w = hs.shape[1]
    d = w - META_LANES
    f = w_gate.shape[2]
    lo3 = lambda i, oin, oout, elo, ehi, nu: (elo[i], 0, 0)
    hi3 = lambda i, oin, oout, elo, ehi, nu: (ehi[i], 0, 0)
    grid_spec = pltpu.PrefetchScalarGridSpec(
        num_scalar_prefetch=5,
        grid=(n_tiles,),
        in_specs=[
            pl.BlockSpec((tm, w), lambda i, oin, oout, elo, ehi, nu: (oin[i], 0)),
            pl.BlockSpec((1, d, f), lo3), pl.BlockSpec((1, d, f), hi3),
            pl.BlockSpec((1, d, f), lo3), pl.BlockSpec((1, d, f), hi3),
            pl.BlockSpec((1, f, d), lo3), pl.BlockSpec((1, f, d), hi3),
        ],
        out_specs=pl.BlockSpec((tm, d), lambda i, oin, oout, elo, ehi, nu: (oout[i], 0)),
    )
    return pl.pallas_call(
        _moe_kernel,
        name="moe_experts",
        grid_spec=grid_spec,
        out_shape=jax.ShapeDtypeStruct((n_tiles * tm, d), F32),
        compiler_params=pltpu.CompilerParams(dimension_semantics=("arbitrary",), vmem_limit_bytes=VMEM_LIMIT),
    )(order_in, order_out, e_lo, e_hi, n_used, hs, w_gate, w_gate, w_up, w_up, w_down, w_down)


def _combine_kernel(final_norm, pos_ref, x_ref, y_hbm, g_ref, o_ref, rows_ref, sem):
    tc = x_ref.shape[0]
    i = pl.program_id(0)
    n = pl.num_programs(0)

    def issue(tile, slot):
        base = tile * tc
        for r in range(tc):
            pltpu.make_async_copy(y_hbm.at[pl.ds(pos_ref[base + r], 1)], rows_ref.at[slot, pl.ds(r, 1)],
                                  sem.at[slot]).start(priority=r % N_DMA_PRIORITIES)

    @pl.when(i == 0)
    def _():
        issue(0, 0)

    for slot in range(2):
        @pl.when(i % 2 == slot)
        def _():
            @pl.when(i + 1 < n)
            def _():
                issue(i + 1, 1 - slot)

            pltpu.make_async_copy(y_hbm.at[pl.ds(0, tc)], rows_ref.at[slot], sem.at[slot]).wait()
            x = x_ref[...] + rows_ref[slot]
            if final_norm:
                x = _rmsnorm(x, g_ref[...])
            o_ref[...] = x


def _combine(x, y_sorted, pos, g, final_norm):
    t, d = x.shape
    tc = COMBINE_TILE
    grid_spec = pltpu.PrefetchScalarGridSpec(
        num_scalar_prefetch=1,
        grid=(t // tc,),
        in_specs=[pl.BlockSpec((tc, d), lambda i, p: (i, 0)), pl.BlockSpec(memory_space=pl.ANY),
                  pl.BlockSpec((1, d), lambda i, p: (0, 0))],
        out_specs=pl.BlockSpec((tc, d), lambda i, p: (i, 0)),
        scratch_shapes=[pltpu.VMEM((2, tc, d), F32), pltpu.SemaphoreType.DMA((2,))],
    )
    return pl.pallas_call(
        functools.partial(_combine_kernel, final_norm),
        name="combine",
        grid_spec=grid_spec,
        out_shape=jax.ShapeDtypeStruct((t, d), F32),
        compiler_params=pltpu.CompilerParams(dimension_semantics=("arbitrary",), vmem_limit_bytes=VMEM_LIMIT),
    )(pos, x, y_sorted, g[None, :])


def kernel(x, mem, positions, norm_mix_g, w_in, b_branch_gate, ret_norm_g, w_ret_out, conv_w, conv_b, conv_ln_g, conv_ln_b, w_conv_out, w_mix_out, norm_xattn_g, norm_mem_g, w_xq, w_xkv, w_xo, norm_ffn_g, w_router, b_router, w_exp_gate, w_exp_up, w_exp_down, norm_final_g):
    b, s, d = x.shape
    depth = w_in.shape[0]
    t = b * s
    assert s % SEQ_TILE == 0 and SEQ_TILE % RET_CHUNK == 0
    assert t % MOE_TILE == 0 and t % COMBINE_TILE == 0
    tm = MOE_TILE
    n_tiles = t // tm + N_CLASSES
    cos, sin = _rope_tables(positions)
    n_exp, _, f = w_exp_gate.shape[1:]
    wg_all = w_exp_gate.astype(BF16).reshape(depth * n_exp, d, f)
    wu_all = w_exp_up.astype(BF16).reshape(depth * n_exp, d, f)
    wd_all = w_exp_down.astype(BF16).reshape(depth * n_exp, f, d)
    moe = None
    for l in range(depth):
        kv = _mem_kv(mem, norm_mem_g[l], w_xkv[l].astype(BF16))
        x, meta, alloc, hs = _layer(x, cos, sin, norm_mix_g[l], w_in[l], b_branch_gate[l], ret_norm_g[l], w_ret_out[l],
                                    conv_w[l], conv_b[l], conv_ln_g[l], conv_ln_b[l], w_conv_out[l], w_mix_out[l],
                                    kv, norm_xattn_g[l], w_xq[l], w_xo[l], norm_ffn_g[l], w_router, b_router,
                                    (n_tiles + 1) * tm, moe)
        pos = meta[3].astype(jnp.int32)
        order_in, order_out, e_lo, e_hi, n_used = _tile_tables(alloc, n_tiles)
        y_sorted = _moe(hs, order_in, order_out, e_lo + l * n_exp, e_hi + l * n_exp, n_used, wg_all, wu_all, wd_all,
                        n_tiles, tm)
        moe = (y_sorted, pos)
    return _combine(x.reshape(t, d), y_sorted, pos, norm_final_g, True).reshape(b, s, d)
```

```python
import functools

import jax
import jax.numpy as jnp
import numpy as np
from jax import lax
from jax.experimental import pallas as pl
from jax.experimental.pallas import tpu as pltpu

F32 = jnp.float32
BF16 = jnp.bfloat16

RET_HEADS = 4
HEAD_DIM = 128
RET_WIDTH = RET_HEADS * HEAD_DIM
RET_CHUNK = 128
CONV_CH = 512
CONV_WIDTH = 31
CONV_HALO = 32
XATTN_HEADS = 4
XATTN_WIDTH = XATTN_HEADS * HEAD_DIM
N_EXPERTS = 16
N_GROUPS = 4
EXPERTS_PER_GROUP = 4
ROPE_BASE = 10000.0
EPS = 1e-6
QK_SCALE = HEAD_DIM ** -0.5

PAIRS = ((0, 1), (0, 2), (0, 3), (1, 2), (1, 3), (2, 3))
N_CLASSES = N_GROUPS * len(PAIRS)
CLASS_ROWS = 32
META_LANES = 128

SEQ_TILE = 256
MIXER_TILE = 512
MOE_TILE = 256
COMBINE_TILE = 256
VMEM_LIMIT = 56 * 1024 * 1024


def _rmsnorm(x, g):
    return x * lax.rsqrt(jnp.mean(x * x, axis=-1, keepdims=True) + EPS) * g


def _sigmoid(x):
    return 1.0 / (1.0 + jnp.exp(-x))


def _dot(a, b):
    return jnp.dot(a, b, preferred_element_type=F32)


def _dot_nt(a, b):
    return lax.dot_general(a, b, (((1,), (1,)), ((), ())), preferred_element_type=F32)


def _dot_tn(a, b):
    return lax.dot_general(a, b, (((0,), (0,)), ((), ())), preferred_element_type=F32)


def _const_spec(shape):
    n = len(shape)
    return pl.BlockSpec(shape, lambda *_: (0,) * n, pipeline_mode=pl.Buffered(1))


def _rope_kernel(pos_ref, invf_ref, cos_ref, sin_ref):
    ang = pos_ref[...] * invf_ref[...]
    lane = lax.broadcasted_iota(jnp.int32, ang.shape, 1)
    s = jnp.sin(ang)
    cos_ref[...] = jnp.cos(ang)
    sin_ref[...] = jnp.where(lane < HEAD_DIM // 2, -s, s)


def _rope_tables(positions):
    s = positions.shape[0]
    half = HEAD_DIM // 2
    inv_freq = ROPE_BASE ** (-jnp.arange(half, dtype=F32) / half)
    invf = jnp.concatenate([inv_freq, inv_freq])[None, :]
    pos = positions.astype(F32)[:, None]
    ts = min(s, 1024)
    return pl.pallas_call(
        _rope_kernel,
        name="rope_tables",
        grid=(s // ts,),
        in_specs=[pl.BlockSpec((ts, 1), lambda i: (i, 0)), pl.BlockSpec((1, HEAD_DIM), lambda i: (0, 0))],
        out_specs=[pl.BlockSpec((ts, HEAD_DIM), lambda i: (i, 0))] * 2,
        out_shape=[jax.ShapeDtypeStruct((s, HEAD_DIM), F32)] * 2,
    )(pos, invf)


def _memkv_kernel(mem_ref, g_ref, w_ref, kv_ref):
    m = _rmsnorm(mem_ref[0], g_ref[...]).astype(BF16)
    kv_ref[0] = _dot(m, w_ref[...]).astype(BF16)


def _mem_kv(mem, g, w_xkv):
    b, m, d = mem.shape
    return pl.pallas_call(
        _memkv_kernel,
        name="mem_kv",
        grid=(b,),
        in_specs=[pl.BlockSpec((1, m, d), lambda i: (i, 0, 0)), pl.BlockSpec((1, d), lambda i: (0, 0)),
                  pl.BlockSpec((d, 2 * XATTN_WIDTH), lambda i: (0, 0))],
        out_specs=pl.BlockSpec((1, m, 2 * XATTN_WIDTH), lambda i: (i, 0, 0)),
        out_shape=jax.ShapeDtypeStruct((b, m, 2 * XATTN_WIDTH), BF16),
        compiler_params=pltpu.CompilerParams(vmem_limit_bytes=VMEM_LIMIT),
    )(mem, g[None, :], w_xkv)


def _retention_tables():
    h = np.arange(RET_HEADS, dtype=np.float64)
    log_gamma = np.log(1.0 - 2.0 ** (-5.0 - h))
    idx = np.arange(RET_CHUNK, dtype=np.float64)
    diff = idx[:, None] - idx[None, :]
    decay = np.where(diff[None] >= 0.0, np.exp(np.maximum(diff, 0.0)[None] * log_gamma[:, None, None]), 0.0)
    zeta = np.exp((RET_CHUNK - 1.0 - idx)[None, :] * log_gamma[:, None])
    xi = np.exp((idx + 1.0)[None, :] * log_gamma[:, None])
    chunk_decay = np.exp(RET_CHUNK * log_gamma)
    bc = lambda t: np.broadcast_to(t[:, :, None], (RET_HEADS, RET_CHUNK, HEAD_DIM))
    return (jnp.asarray(decay, F32), jnp.asarray(bc(zeta), F32), jnp.asarray(bc(xi), F32),
            tuple(float(c) for c in chunk_decay))


def _gather_rows(idx_ref, tile, n, src_hbm, dst_ref, slot, sem):
    base = tile * n
    for r in range(n):
        pltpu.make_async_copy(src_hbm.at[pl.ds(idx_ref[base + r], 1)], dst_ref.at[slot, pl.ds(r, 1)],
                              sem.at[slot]).start()


def _wait_gathered(n, src_hbm, dst_ref, slot, sem):
    pltpu.make_async_copy(src_hbm.at[pl.ds(0, n)], dst_ref.at[slot], sem.at[slot]).wait()


def _mixer_kernel(chunk_decay, add_moe, *refs):
    if add_moe:
        pos_ref, refs = refs[0], refs[1:]
    (x_ref, cos_ref, sin_ref, g_ref, win_ref, bg_ref, rg_ref, wro_ref, cw_ref, cb_ref, lng_ref, lnb_ref,
     wco_ref, wmo_ref, decay_ref, zeta_ref, xi_ref) = refs[:17]
    refs = refs[17:]
    if add_moe:
        y_hbm, refs = refs[0], refs[1:]
    o_ref, state_ref, zext_ref, phase_ref, u_ref = refs[:5]
    ts = x_ref.shape[1]
    d = x_ref.shape[2]

    @pl.when(pl.program_id(1) == 0)
    def _():
        state_ref[...] = jnp.zeros_like(state_ref)
        zext_ref[0:CONV_HALO, :] = jnp.zeros((CONV_HALO, CONV_CH), F32)

    x = x_ref[0]
    if add_moe:
        rows_ref, sem = refs[5:7]
        lin = pl.program_id(0) * pl.num_programs(1) + pl.program_id(1)
        total = pl.num_programs(0) * pl.num_programs(1)
        slot = lin % 2

        @pl.when(lin == 0)
        def _():
            _gather_rows(pos_ref, 0, ts, y_hbm, rows_ref, 0, sem)

        _wait_gathered(ts, y_hbm, rows_ref, slot, sem)
        x = x + rows_ref[slot]
        _gather_rows(pos_ref, jnp.minimum(lin + 1, total - 1), ts, y_hbm, rows_ref, 1 - slot, sem)
    h = _rmsnorm(x, g_ref[...]).astype(BF16)

    def proj(a, b):
        return _dot(h, win_ref[:, a:b])

    o_q, o_k, o_v, o_g = RET_WIDTH, 2 * RET_WIDTH, 3 * RET_WIDTH, 4 * RET_WIDTH
    o_c = o_g + 2 * CONV_CH

    ca, cg = proj(o_g, o_g + CONV_CH), proj(o_g + CONV_CH, o_c)
    zext_ref[CONV_HALO:CONV_HALO + ts, :] = ca * _sigmoid(cg)
    first = CONV_HALO - (CONV_WIDTH - 1)
    other_cols = [(c0, c0 + 256) for c0 in range(0, o_g, 256)] + [(c0, c0 + 256) for c0 in range(o_c, o_c + 2 * d, 256)]
    per_phase = len(other_cols) // 8
    acc = None
    for b in range(8):
        for k in range(b * per_phase, (b + 1) * per_phase):
            u_ref[:, 256 * k:256 * (k + 1)] = proj(*other_cols[k])
        rows = ts if b == 0 else ts + 8
        part = None
        for a in range(CONV_HALO // 8 + 1):
            w = 8 * a + b - first
            if 0 <= w < CONV_WIDTH:
                term = zext_ref[8 * a:8 * a + rows, :] * cw_ref[w:w + 1, :]
                part = term if part is None else part + term
        if b == 0:
            acc = part
        else:
            phase_ref[b - 1] = part
            acc = acc + phase_ref[b - 1, b:b + ts, :]
    acc = acc + cb_ref[...]
    zext_ref[0:CONV_HALO, :] = zext_ref[ts:ts + CONV_HALO, :]
    mu = jnp.mean(acc, axis=-1, keepdims=True)
    cen = acc - mu
    var = jnp.mean(cen * cen, axis=-1, keepdims=True)
    zf = cen * lax.rsqrt(var + EPS) * lng_ref[...] + lnb_ref[...]
    y_conv = _dot((zf * _sigmoid(zf)).astype(BF16), wco_ref[...])
    uq, uk, uv, sg = (u_ref[:, c0:c0 + RET_WIDTH] for c0 in (0, o_q, o_k, o_v))
    gate_pre = u_ref[:, o_g:o_g + 2 * d]

    cos, sin = cos_ref[...], sin_ref[...]
    heads = []
    for hd in range(RET_HEADS):
        sl = slice(hd * HEAD_DIM, (hd + 1) * HEAD_DIM)
        qh = uq[:, sl]
        qh = qh * cos + pltpu.roll(qh, HEAD_DIM // 2, 1) * sin
        kh = uk[:, sl]
        kh = (kh * cos + pltpu.roll(kh, HEAD_DIM // 2, 1) * sin) * QK_SCALE
        vh = uv[:, sl]
        rows = []
        for c in range(ts // RET_CHUNK):
            r = slice(c * RET_CHUNK, (c + 1) * RET_CHUNK)
            qc = qh[r].astype(BF16)
            kc = kh[r]
            vc = vh[r].astype(BF16)
            sc = _dot_nt(qc, kc.astype(BF16)) * decay_ref[hd]
            inner = _dot(sc.astype(BF16), vc)
            st = state_ref[hd]
            cross = _dot(qc, st.astype(BF16)) * xi_ref[hd]
            kz = (kc * zeta_ref[hd]).astype(BF16)
            state_ref[hd] = chunk_decay[hd] * st + _dot_tn(kz, vc)
            rows.append(inner + cross)
        rh = jnp.concatenate(rows, axis=0)
        heads.append(rh * lax.rsqrt(jnp.mean(rh * rh, axis=-1, keepdims=True) + EPS))
    r = jnp.concatenate(heads, axis=1) * rg_ref[...]
    r = (sg * _sigmoid(sg)) * r
    y_ret = _dot(r.astype(BF16), wro_ref[...])

    gate = _sigmoid(gate_pre + bg_ref[...])
    merged = (gate[:, 0:d] * y_ret + gate[:, d:2 * d] * y_conv).astype(BF16)
    o_ref[0] = x + _dot(merged, wmo_ref[...])
    if add_moe:
        @pl.when(lin == total - 1)
        def _():
            _wait_gathered(ts, y_hbm, rows_ref, 1 - slot, sem)


def _mixer(x, cos, sin, g, w_in, b_gate, ret_g, w_ret_out, conv_w, conv_b, ln_g, ln_b, w_conv_out, w_mix_out,
           moe=None):
    b, s, d = x.shape
    ts = MIXER_TILE
    decay, zeta, xi, chunk_decay = _retention_tables()
    in_cols = w_in.shape[1]
    cw = jnp.zeros((CONV_HALO, CONV_CH), F32).at[:CONV_WIDTH].set(conv_w)
    row = lambda v: v[None, :]
    tab = (RET_HEADS, RET_CHUNK, HEAD_DIM)
    add_moe = moe is not None
    in_specs = [
        pl.BlockSpec((1, ts, d), lambda i, j, *_: (i, j, 0)),
        pl.BlockSpec((ts, HEAD_DIM), lambda i, j, *_: (j, 0)),
        pl.BlockSpec((ts, HEAD_DIM), lambda i, j, *_: (j, 0)),
        _const_spec((1, d)), _const_spec((d, in_cols)), _const_spec((1, 2 * d)),
        _const_spec((1, RET_WIDTH)), _const_spec((RET_WIDTH, d)),
        _const_spec((CONV_HALO, CONV_CH)), _const_spec((1, CONV_CH)), _const_spec((1, CONV_CH)),
        _const_spec((1, CONV_CH)), _const_spec((CONV_CH, d)), _const_spec((d, d)),
        _const_spec((RET_HEADS, RET_CHUNK, RET_CHUNK)), _const_spec(tab), _const_spec(tab),
    ]
    scratch = [pltpu.VMEM((RET_HEADS, HEAD_DIM, HEAD_DIM), F32),
               pltpu.VMEM((CONV_HALO + ts, CONV_CH), F32),
               pltpu.VMEM((7, ts + 8, CONV_CH), F32),
               pltpu.VMEM((ts, in_cols - 2 * CONV_CH), F32)]
    args = [x, cos, sin, row(g), w_in.astype(BF16), row(b_gate), row(ret_g), w_ret_out.astype(BF16), cw,
            row(conv_b), row(ln_g), row(ln_b), w_conv_out.astype(BF16), w_mix_out.astype(BF16), decay, zeta, xi]
    if add_moe:
        y_sorted, pos = moe
        in_specs.append(pl.BlockSpec(memory_space=pl.ANY))
        scratch += [pltpu.VMEM((2, ts, d), F32), pltpu.SemaphoreType.DMA((2,))]
        args = [pos] + args + [y_sorted]
    grid_spec = pltpu.PrefetchScalarGridSpec(
        num_scalar_prefetch=1 if add_moe else 0,
        grid=(b, s // ts),
        in_specs=in_specs,
        out_specs=pl.BlockSpec((1, ts, d), lambda i, j, *_: (i, j, 0)),
        scratch_shapes=scratch,
    )
    return pl.pallas_call(
        functools.partial(_mixer_kernel, chunk_decay, add_moe),
        name="mixer",
        grid_spec=grid_spec,
        out_shape=jax.ShapeDtypeStruct((b, s, d), F32),
        compiler_params=pltpu.CompilerParams(dimension_semantics=("arbitrary", "arbitrary"),
                                             vmem_limit_bytes=VMEM_LIMIT),
    )(*args)


def _route(logits, bias):
    scores = _sigmoid(logits)
    sel = scores + bias
    one, zero = jnp.float32(1.0), jnp.float32(0.0)
    top, gscore = [], []
    for g in range(N_GROUPS):
        a = [sel[EXPERTS_PER_GROUP * g + i:EXPERTS_PER_GROUP * g + i + 1, :] for i in range(EXPERTS_PER_GROUP)]
        tg, sg = [], None
        for i in range(EXPERTS_PER_GROUP):
            rank = None
            for j in range(EXPERTS_PER_GROUP):
                if j == i:
                    continue
                ahead = (a[j] >= a[i]) if j < i else (a[j] > a[i])
                ahead = jnp.where(ahead, one, zero)
                rank = ahead if rank is None else rank + ahead
            in_top = rank < 2.0
            tg.append(in_top)
            contrib = jnp.where(in_top, a[i], zero)
            sg = contrib if sg is None else sg + contrib
        top.append(tg)
        gscore.append(sg)
    cls = jnp.zeros_like(gscore[0])
    w_lo = jnp.zeros_like(cls)
    w_hi = jnp.zeros_like(cls)
    for g in range(N_GROUPS):
        behind = None
        for g2 in range(N_GROUPS):
            if g2 == g:
                continue
            ahead = (gscore[g2] >= gscore[g]) if g2 < g else (gscore[g2] > gscore[g])
            behind = ahead if behind is None else (behind | ahead)
        best = jnp.logical_not(behind)
        for p, (i, j) in enumerate(PAIRS):
            active = best & top[g][i] & top[g][j]
            e_lo, e_hi = EXPERTS_PER_GROUP * g + i, EXPERTS_PER_GROUP * g + j
            cls = jnp.where(active, jnp.float32(len(PAIRS) * g + p), cls)
            w_lo = jnp.where(active, scores[e_lo:e_lo + 1, :], w_lo)
            w_hi = jnp.where(active, scores[e_hi:e_hi + 1, :], w_hi)
    den = w_lo + w_hi
    return cls, w_lo / den, w_hi / den


def _xattn_route_kernel(x_ref, kv_ref, gx_ref, wq_ref, wo_ref, gf_ref, wrh_ref, wrl_ref, br_ref,
                        init_hbm, o_ref, meta_ref, alloc_ref, hs_hbm,
                        fill_ref, cur_ref, next_ref, rows_ref, posv_ref, pos_smem, sem, sem_pos):
    del init_hbm
    ts = x_ref.shape[1]
    d = x_ref.shape[2]
    tm = float(MOE_TILE)
    lin = pl.program_id(0) * pl.num_programs(1) + pl.program_id(1)
    total = pl.num_programs(0) * pl.num_programs(1)
    slot = lin % 2

    def wait_rows(sl):
        pltpu.make_async_copy(rows_ref.at[sl], hs_hbm.at[pl.ds(0, ts)], sem.at[sl]).wait()

    @pl.when(lin == 0)
    def _():
        fill_ref[...] = jnp.full(fill_ref.shape, tm, F32)
        cur_ref[...] = jnp.zeros_like(cur_ref)
        next_ref[...] = jnp.zeros_like(next_ref)
        rows_ref[1] = jnp.zeros(rows_ref.shape[1:], F32)
        for r in range(ts):
            pos_smem[0, r] = hs_hbm.shape[0] - ts + r

    def pos_to_smem():
        return pltpu.make_async_copy(posv_ref.at[pl.ds(0, 1)], pos_smem, sem_pos.at[0])

    def scatter_rows(sl):
        for r in range(ts):
            pltpu.make_async_copy(rows_ref.at[sl, pl.ds(r, 1)], hs_hbm.at[pl.ds(pos_smem[0, r], 1)],
                                  sem.at[sl]).start()

    @pl.when(lin >= 1)
    def _():
        pos_to_smem().wait()

    scatter_rows(1 - slot)

    x = x_ref[0]
    h = _rmsnorm(x, gx_ref[...]).astype(BF16)
    q = _dot(h, wq_ref[...])
    kv = kv_ref[0]
    heads = []
    for hd in range(XATTN_HEADS):
        sl = slice(hd * HEAD_DIM, (hd + 1) * HEAD_DIM)
        vs = slice(XATTN_WIDTH + hd * HEAD_DIM, XATTN_WIDTH + (hd + 1) * HEAD_DIM)
        sc = _dot_nt(q[:, sl].astype(BF16), kv[:, sl]) * QK_SCALE
        sc = sc - jnp.max(sc, axis=-1, keepdims=True)
        p = jnp.exp(sc)
        p = p / jnp.sum(p, axis=-1, keepdims=True)
        heads.append(_dot(p.astype(BF16), kv[:, vs]))
    att = jnp.concatenate(heads, axis=1).astype(BF16)
    x2 = x + _dot(att, wo_ref[...])
    o_ref[0] = x2

    hf = _rmsnorm(x2, gf_ref[...])
    h_hi = hf.astype(BF16)
    h_lo = (hf - h_hi.astype(F32)).astype(BF16)
    lg = _dot(h_hi, wrh_ref[...]) + (_dot(h_lo, wrh_ref[...]) + _dot(h_hi, wrl_ref[...]))
    logits = lg.T[0:N_EXPERTS, :]
    cls, w_lo, w_hi = _route(logits, br_ref[...])

    one, zero = jnp.float32(1.0), jnp.float32(0.0)
    crow = lax.broadcasted_iota(jnp.int32, (CLASS_ROWS, ts), 0).astype(F32)
    onehot = jnp.where(crow == cls, one, zero)
    before = lax.broadcasted_iota(jnp.int32, (ts, ts), 0) < lax.broadcasted_iota(jnp.int32, (ts, ts), 1)
    prefix = _dot(onehot.astype(BF16), jnp.where(before, one, zero).astype(BF16))
    count = jnp.sum(onehot, axis=1, keepdims=True)
    fill, cur, nxt = fill_ref[...], cur_ref[...], next_ref[...]
    need = jnp.where(fill + count > tm, one, zero)
    lower = (lax.broadcasted_iota(jnp.int32, (CLASS_ROWS, CLASS_ROWS), 1)
             < lax.broadcasted_iota(jnp.int32, (CLASS_ROWS, CLASS_ROWS), 0))
    opened_before = _dot(jnp.where(lower, one, zero).astype(BF16),
                         jnp.broadcast_to(need, (CLASS_ROWS, META_LANES)).astype(BF16))[:, 0:1]
    new_tile = nxt + opened_before
    per_token = lambda v: jnp.sum(onehot * v, axis=0, keepdims=True)
    slot_in_cur = per_token(fill) + per_token(prefix)
    pos = jnp.where(slot_in_cur >= tm, per_token(new_tile) * tm + (slot_in_cur - tm),
                    per_token(cur) * tm + slot_in_cur)
    fill_ref[...] = jnp.where(need > zero, fill + count - tm, fill + count)
    cur_ref[...] = jnp.where(need > zero, new_tile, cur)
    next_ref[...] = nxt + jnp.sum(need, axis=0, keepdims=True)
    lane = lax.broadcasted_iota(jnp.int32, (CLASS_ROWS, META_LANES), 1)
    alloc_ref[...] = jnp.where(lane == 0, need, jnp.where(lane == 1, new_tile, zero))

    meta = jnp.concatenate([cls, w_lo, w_hi, pos, jnp.zeros((META_LANES - 4, ts), F32)], axis=0)
    meta_ref[...] = meta[0:8, :]

    @pl.when(lin >= 1)
    def _():
        wait_rows(slot)

    rows_ref[slot, :, 0:d] = hf
    rows_ref[slot, :, d:d + META_LANES] = meta.T
    posv_ref[...] = jnp.broadcast_to(pos.astype(jnp.int32), posv_ref.shape)
    pos_to_smem().start()

    @pl.when(lin == total - 1)
    def _():
        pos_to_smem().wait()
        scatter_rows(slot)
        wait_rows(slot)
        wait_rows(1 - slot)


def _xattn_route(x, kv, g_x, w_xq, w_xo, g_f, w_router, b_router, n_rows):
    b, s, d = x.shape
    ts = SEQ_TILE
    assert ts <= MOE_TILE
    m = kv.shape[1]
    nt = s // ts
    w = d + META_LANES
    row = lambda v: v[None, :]
    wr = jnp.zeros((d, META_LANES), F32).at[:, :N_EXPERTS].set(w_router)
    wr_hi = wr.astype(BF16)
    wr_lo = (wr - wr_hi.astype(F32)).astype(BF16)
    return pl.pallas_call(
        _xattn_route_kernel,
        name="xattn_route",
        grid=(b, nt),
        in_specs=[
            pl.BlockSpec((1, ts, d), lambda i, j: (i, j, 0)),
            pl.BlockSpec((1, m, 2 * XATTN_WIDTH), lambda i, j: (i, 0, 0)),
            _const_spec((1, d)), _const_spec((d, XATTN_WIDTH)), _const_spec((XATTN_WIDTH, d)),
            _const_spec((1, d)), _const_spec((d, META_LANES)), _const_spec((d, META_LANES)),
            _const_spec((N_EXPERTS, 1)),
            pl.BlockSpec(memory_space=pl.ANY),
        ],
        out_specs=[
            pl.BlockSpec((1, ts, d), lambda i, j: (i, j, 0)),
            pl.BlockSpec((8, ts), lambda i, j: (0, i * nt + j)),
            pl.BlockSpec((CLASS_ROWS, META_LANES), lambda i, j: (i * nt + j, 0)),
            pl.BlockSpec(memory_space=pl.ANY),
        ],
        out_shape=[
            jax.ShapeDtypeStruct((b, s, d), F32),
            jax.ShapeDtypeStruct((8, b * s), F32),
            jax.ShapeDtypeStruct((b * nt * CLASS_ROWS, META_LANES), F32),
            jax.ShapeDtypeStruct((n_rows, w), F32),
        ],
        scratch_shapes=[pltpu.VMEM((CLASS_ROWS, 1), F32), pltpu.VMEM((CLASS_ROWS, 1), F32),
                        pltpu.VMEM((1, 1), F32), pltpu.VMEM((2, ts, w), F32),
                        pltpu.VMEM((8, ts), jnp.int32), pltpu.SMEM((1, ts), jnp.int32),
                        pltpu.SemaphoreType.DMA((2,)), pltpu.SemaphoreType.DMA((1,))],
        input_output_aliases={9: 3},
        compiler_params=pltpu.CompilerParams(dimension_semantics=("arbitrary", "arbitrary"),
                                             vmem_limit_bytes=VMEM_LIMIT),
    )(x, kv, row(g_x), w_xq.astype(BF16), w_xo.astype(BF16), row(g_f), wr_hi, wr_lo, b_router[:, None],
      jnp.zeros((n_rows, w), F32))


def _tile_tables(alloc, n_tiles):
    opened = alloc[:, 0] > 0.0
    tile_of = alloc[:, 1].astype(jnp.int32)
    cls_of = jnp.arange(alloc.shape[0], dtype=jnp.int32) % CLASS_ROWS
    tile_ids = jnp.arange(n_tiles, dtype=jnp.int32)
    hit = opened[None, :] & (tile_of[None, :] == tile_ids[:, None])
    tile_cls = jnp.sum(jnp.where(hit, cls_of[None, :], 0), axis=1)
    used = jnp.any(hit, axis=1)
    n_used = jnp.sum(used.astype(jnp.int32))
    key = jnp.where(used, tile_cls, N_CLASSES)
    ahead = (key[None, :] < key[:, None]) | ((key[None, :] == key[:, None]) & (tile_ids[None, :] < tile_ids[:, None]))
    rank = jnp.sum(ahead.astype(jnp.int32), axis=1)
    order = jnp.sum(jnp.where(rank[None, :] == tile_ids[:, None], tile_ids[None, :], 0), axis=1)
    order_in = order[jnp.minimum(tile_ids, n_used - 1)]
    step_cls = tile_cls[order_in]
    pair = step_cls % len(PAIRS)
    base = EXPERTS_PER_GROUP * (step_cls // len(PAIRS))
    lo_of = jnp.asarray([p[0] for p in PAIRS], jnp.int32)
    hi_of = jnp.asarray([p[1] for p in PAIRS], jnp.int32)
    i32 = lambda v: v.astype(jnp.int32)
    return i32(order_in), i32(order), i32(base + lo_of[pair]), i32(base + hi_of[pair]), i32(n_used.reshape(1))


def _moe_kernel(oin_ref, oout_ref, elo_ref, ehi_ref, nused_ref, rows_ref, wg_lo, wg_hi, wu_lo, wu_hi, wd_lo, wd_hi, y_ref):
    tm, d = y_ref.shape
    i = pl.program_id(0)
    used = i < nused_ref[0]

    @pl.when(used)
    def _():
        hb = rows_ref[:, 0:d].astype(BF16)
        y = None
        for col, wg, wu, wd in ((d + 1, wg_lo, wu_lo, wd_lo), (d + 2, wg_hi, wu_hi, wd_hi)):
            gate = _dot(hb, wg[0])
            act = (gate * _sigmoid(gate)) * _dot(hb, wu[0])
            part = rows_ref[:, col:col + 1] * _dot(act.astype(BF16), wd[0])
            y = part if y is None else y + part
        y_ref[...] = y

    @pl.when(jnp.logical_not(used))
    def _():
        y_ref[...] = jnp.zeros_like(y_ref)


def _moe(hs, order_in, order_out, e_lo, e_hi, n_used, w_gate, w_up, w_down, n_tiles, tm):
    w = hs.shape[1]
    d = w - META_LANES
    f = w_gate.shape[2]
    lo3 = lambda i, oin, oout, elo, ehi, nu: (elo[i], 0, 0)
    hi3 = lambda i, oin, oout, elo, ehi, nu: (ehi[i], 0, 0)
    grid_spec = pltpu.PrefetchScalarGridSpec(
        num_scalar_prefetch=5,
        grid=(n_tiles,),
        in_specs=[
            pl.BlockSpec((tm, w), lambda i, oin, oout, elo, ehi, nu: (oin[i], 0)),
            pl.BlockSpec((1, d, f), lo3), pl.BlockSpec((1, d, f), hi3),
            pl.BlockSpec((1, d, f), lo3), pl.BlockSpec((1, d, f), hi3),
            pl.BlockSpec((1, f, d), lo3), pl.BlockSpec((1, f, d), hi3),
        ],
        out_specs=pl.BlockSpec((tm, d), lambda i, oin, oout, elo, ehi, nu: (oout[i], 0)),
    )
    return pl.pallas_call(
        _moe_kernel,
        name="moe_experts",
        grid_spec=grid_spec,
        out_shape=jax.ShapeDtypeStruct((n_tiles * tm, d), F32),
        compiler_params=pltpu.CompilerParams(dimension_semantics=("arbitrary",), vmem_limit_bytes=VMEM_LIMIT),
    )(order_in, order_out, e_lo, e_hi, n_used, hs, w_gate, w_gate, w_up, w_up, w_down, w_down)


def _combine_kernel(final_norm, pos_ref, x_ref, y_hbm, g_ref, o_ref, rows_ref, sem):
    tc = x_ref.shape[0]
    i = pl.program_id(0)
    n = pl.num_programs(0)

    def issue(tile, slot):
        base = tile * tc
        for r in range(tc):
            pltpu.make_async_copy(y_hbm.at[pl.ds(pos_ref[base + r], 1)], rows_ref.at[slot, pl.ds(r, 1)],
                                  sem.at[slot]).start()

    @pl.when(i == 0)
    def _():
        issue(0, 0)

    for slot in range(2):
        @pl.when(i % 2 == slot)
        def _():
            @pl.when(i + 1 < n)
            def _():
                issue(i + 1, 1 - slot)

            pltpu.make_async_copy(y_hbm.at[pl.ds(0, tc)], rows_ref.at[slot], sem.at[slot]).wait()
            x = x_ref[...] + rows_ref[slot]
            if final_norm:
                x = _rmsnorm(x, g_ref[...])
            o_ref[...] = x


def _combine(x, y_sorted, pos, g, final_norm):
    t, d = x.shape
    tc = COMBINE_TILE
    grid_spec = pltpu.PrefetchScalarGridSpec(
        num_scalar_prefetch=1,
        grid=(t // tc,),
        in_specs=[pl.BlockSpec((tc, d), lambda i, p: (i, 0)), pl.BlockSpec(memory_space=pl.ANY),
                  pl.BlockSpec((1, d), lambda i, p: (0, 0))],
        out_specs=pl.BlockSpec((tc, d), lambda i, p: (i, 0)),
        scratch_shapes=[pltpu.VMEM((2, tc, d), F32), pltpu.SemaphoreType.DMA((2,))],
    )
    return pl.pallas_call(
        functools.partial(_combine_kernel, final_norm),
        name="combine",
        grid_spec=grid_spec,
        out_shape=jax.ShapeDtypeStruct((t, d), F32),
        compiler_params=pltpu.CompilerParams(dimension_semantics=("arbitrary",), vmem_limit_bytes=VMEM_LIMIT),
    )(pos, x, y_sorted, g[None, :])


def kernel(x, mem, positions, norm_mix_g, w_in, b_branch_gate, ret_norm_g, w_ret_out, conv_w, conv_b, conv_ln_g, conv_ln_b, w_conv_out, w_mix_out, norm_xattn_g, norm_mem_g, w_xq, w_xkv, w_xo, norm_ffn_g, w_router, b_router, w_exp_gate, w_exp_up, w_exp_down, norm_final_g):
    b, s, d = x.shape
    depth = w_in.shape[0]
    t = b * s
    assert s % SEQ_TILE == 0 and s % MIXER_TILE == 0 and MIXER_TILE % RET_CHUNK == 0
    assert t % MOE_TILE == 0 and t % COMBINE_TILE == 0
    tm = MOE_TILE
    n_tiles = t // tm + N_CLASSES
    cos, sin = _rope_tables(positions)
    n_exp, _, f = w_exp_gate.shape[1:]
    wg_all = w_exp_gate.astype(BF16).reshape(depth * n_exp, d, f)
    wu_all = w_exp_up.astype(BF16).reshape(depth * n_exp, d, f)
    wd_all = w_exp_down.astype(BF16).reshape(depth * n_exp, f, d)
    moe = None
    for l in range(depth):
        x = _mixer(x, cos, sin, norm_mix_g[l], w_in[l], b_branch_gate[l], ret_norm_g[l], w_ret_out[l],
                   conv_w[l], conv_b[l], conv_ln_g[l], conv_ln_b[l], w_conv_out[l], w_mix_out[l], moe)
        kv = _mem_kv(mem, norm_mem_g[l], w_xkv[l].astype(BF16))
        x, meta, alloc, hs = _xattn_route(x, kv, norm_xattn_g[l], w_xq[l], w_xo[l], norm_ffn_g[l],
                                          w_router, b_router, (n_tiles + 1) * tm)
        pos = meta[3].astype(jnp.int32)
        order_in, order_out, e_lo, e_hi, n_used = _tile_tables(alloc, n_tiles)
        y_sorted = _moe(hs, order_in, order_out, e_lo + l * n_exp, e_hi + l * n_exp, n_used, wg_all, wu_all, wd_all,
                        n_tiles, tm)
        moe = (y_sorted, pos)
    return _combine(x.reshape(t, d), y_sorted, pos, norm_final_g, True).reshape(b, s, d)
```

```python
import functools

import jax
import jax.numpy as jnp
import numpy as np
from jax import lax
from jax.experimental import pallas as pl
from jax.experimental.pallas import tpu as pltpu

F32 = jnp.float32
BF16 = jnp.bfloat16

RET_HEADS = 4
HEAD_DIM = 128
RET_WIDTH = RET_HEADS * HEAD_DIM
RET_CHUNK = 128
CONV_CH = 512
CONV_WIDTH = 31
CONV_HALO = 32
XATTN_HEADS = 4
XATTN_WIDTH = XATTN_HEADS * HEAD_DIM
N_EXPERTS = 16
N_GROUPS = 4
EXPERTS_PER_GROUP = 4
ROPE_BASE = 10000.0
EPS = 1e-6
QK_SCALE = HEAD_DIM ** -0.5

PAIRS = ((0, 1), (0, 2), (0, 3), (1, 2), (1, 3), (2, 3))
N_CLASSES = N_GROUPS * len(PAIRS)
CLASS_ROWS = 32
META_LANES = 128

SEQ_TILE = 256
MIXER_TILE = 256
MOE_TILE = 256
COMBINE_TILE = 256
VMEM_LIMIT = 56 * 1024 * 1024


def _rmsnorm(x, g):
    return x * lax.rsqrt(jnp.mean(x * x, axis=-1, keepdims=True) + EPS) * g


def _sigmoid(x):
    return 1.0 / (1.0 + jnp.exp(-x))


def _dot(a, b):
    return jnp.dot(a, b, preferred_element_type=F32)


def _dot_nt(a, b):
    return lax.dot_general(a, b, (((1,), (1,)), ((), ())), preferred_element_type=F32)


def _dot_tn(a, b):
    return lax.dot_general(a, b, (((0,), (0,)), ((), ())), preferred_element_type=F32)


def _const_spec(shape):
    n = len(shape)
    return pl.BlockSpec(shape, lambda *_: (0,) * n, pipeline_mode=pl.Buffered(1))


def _rope_kernel(pos_ref, invf_ref, cos_ref, sin_ref):
    ang = pos_ref[...] * invf_ref[...]
    lane = lax.broadcasted_iota(jnp.int32, ang.shape, 1)
    s = jnp.sin(ang)
    cos_ref[...] = jnp.cos(ang)
    sin_ref[...] = jnp.where(lane < HEAD_DIM // 2, -s, s)


def _rope_tables(positions):
    s = positions.shape[0]
    half = HEAD_DIM // 2
    inv_freq = ROPE_BASE ** (-jnp.arange(half, dtype=F32) / half)
    invf = jnp.concatenate([inv_freq, inv_freq])[None, :]
    pos = positions.astype(F32)[:, None]
    ts = min(s, 1024)
    return pl.pallas_call(
        _rope_kernel,
        name="rope_tables",
        grid=(s // ts,),
        in_specs=[pl.BlockSpec((ts, 1), lambda i: (i, 0)), pl.BlockSpec((1, HEAD_DIM), lambda i: (0, 0))],
        out_specs=[pl.BlockSpec((ts, HEAD_DIM), lambda i: (i, 0))] * 2,
        out_shape=[jax.ShapeDtypeStruct((s, HEAD_DIM), F32)] * 2,
    )(pos, invf)


def _memkv_kernel(mem_ref, g_ref, w_ref, kv_ref):
    m = _rmsnorm(mem_ref[0], g_ref[...]).astype(BF16)
    kv_ref[0] = _dot(m, w_ref[...]).astype(BF16)


def _mem_kv(mem, g, w_xkv):
    b, m, d = mem.shape
    return pl.pallas_call(
        _memkv_kernel,
        name="mem_kv",
        grid=(b,),
        in_specs=[pl.BlockSpec((1, m, d), lambda i: (i, 0, 0)), pl.BlockSpec((1, d), lambda i: (0, 0)),
                  pl.BlockSpec((d, 2 * XATTN_WIDTH), lambda i: (0, 0))],
        out_specs=pl.BlockSpec((1, m, 2 * XATTN_WIDTH), lambda i: (i, 0, 0)),
        out_shape=jax.ShapeDtypeStruct((b, m, 2 * XATTN_WIDTH), BF16),
        compiler_params=pltpu.CompilerParams(vmem_limit_bytes=VMEM_LIMIT),
    )(mem, g[None, :], w_xkv)


def _retention_tables():
    h = np.arange(RET_HEADS, dtype=np.float64)
    log_gamma = np.log(1.0 - 2.0 ** (-5.0 - h))
    idx = np.arange(RET_CHUNK, dtype=np.float64)
    diff = idx[:, None] - idx[None, :]
    decay = np.where(diff[None] >= 0.0, np.exp(np.maximum(diff, 0.0)[None] * log_gamma[:, None, None]), 0.0)
    zeta = np.exp((RET_CHUNK - 1.0 - idx)[None, :] * log_gamma[:, None])
    xi = np.exp((idx + 1.0)[None, :] * log_gamma[:, None])
    chunk_decay = np.exp(RET_CHUNK * log_gamma)
    bc = lambda t: np.broadcast_to(t[:, :, None], (RET_HEADS, RET_CHUNK, HEAD_DIM))
    return (jnp.asarray(decay, F32), jnp.asarray(bc(zeta), F32), jnp.asarray(bc(xi), F32),
            tuple(float(c) for c in chunk_decay))


def _gather_rows(idx_ref, tile, n, src_hbm, dst_ref, slot, sem):
    base = tile * n
    for r in range(n):
        pltpu.make_async_copy(src_hbm.at[pl.ds(idx_ref[base + r], 1)], dst_ref.at[slot, pl.ds(r, 1)],
                              sem.at[slot]).start()


def _wait_gathered(n, src_hbm, dst_ref, slot, sem):
    pltpu.make_async_copy(src_hbm.at[pl.ds(0, n)], dst_ref.at[slot], sem.at[slot]).wait()


def _mixer_kernel(chunk_decay, add_moe, *refs):
    if add_moe:
        pos_ref, refs = refs[0], refs[1:]
    (x_ref, cos_ref, sin_ref, g_ref, win_ref, bg_ref, rg_ref, wro_ref, cw_ref, cb_ref, lng_ref, lnb_ref,
     wco_ref, wmo_ref, decay_ref, zeta_ref, xi_ref) = refs[:17]
    refs = refs[17:]
    if add_moe:
        y_hbm, refs = refs[0], refs[1:]
    o_ref, state_ref, zext_ref, phase_ref, u_ref = refs[:5]
    ts = x_ref.shape[1]
    d = x_ref.shape[2]

    @pl.when(pl.program_id(1) == 0)
    def _():
        state_ref[...] = jnp.zeros_like(state_ref)
        zext_ref[0:CONV_HALO, :] = jnp.zeros((CONV_HALO, CONV_CH), F32)

    x = x_ref[0]
    if add_moe:
        rows_ref, sem = refs[5:7]
        lin = pl.program_id(0) * pl.num_programs(1) + pl.program_id(1)
        total = pl.num_programs(0) * pl.num_programs(1)
        slot = lin % 2

        @pl.when(lin == 0)
        def _():
            _gather_rows(pos_ref, 0, ts, y_hbm, rows_ref, 0, sem)

        _wait_gathered(ts, y_hbm, rows_ref, slot, sem)
        x = x + rows_ref[slot]
        _gather_rows(pos_ref, jnp.minimum(lin + 1, total - 1), ts, y_hbm, rows_ref, 1 - slot, sem)
    h = _rmsnorm(x, g_ref[...]).astype(BF16)

    def proj(a, b):
        return _dot(h, win_ref[:, a:b])

    o_q, o_k, o_v, o_g = RET_WIDTH, 2 * RET_WIDTH, 3 * RET_WIDTH, 4 * RET_WIDTH
    o_c = o_g + 2 * CONV_CH

    ca, cg = proj(o_g, o_g + CONV_CH), proj(o_g + CONV_CH, o_c)
    zext_ref[CONV_HALO:CONV_HALO + ts, :] = ca * _sigmoid(cg)
    first = CONV_HALO - (CONV_WIDTH - 1)
    other_cols = [(c0, c0 + 256) for c0 in range(0, o_g, 256)] + [(c0, c0 + 256) for c0 in range(o_c, o_c + 2 * d, 256)]
    per_phase = len(other_cols) // 8
    acc = None
    for b in range(8):
        for k in range(b * per_phase, (b + 1) * per_phase):
            u_ref[:, 256 * k:256 * (k + 1)] = proj(*other_cols[k])
        rows = ts if b == 0 else ts + 8
        part = None
        for a in range(CONV_HALO // 8 + 1):
            w = 8 * a + b - first
            if 0 <= w < CONV_WIDTH:
                term = zext_ref[8 * a:8 * a + rows, :] * cw_ref[w:w + 1, :]
                part = term if part is None else part + term
        if b == 0:
            acc = part
        else:
            phase_ref[b - 1] = part
            acc = acc + phase_ref[b - 1, b:b + ts, :]
    acc = acc + cb_ref[...]
    zext_ref[0:CONV_HALO, :] = zext_ref[ts:ts + CONV_HALO, :]
    mu = jnp.mean(acc, axis=-1, keepdims=True)
    cen = acc - mu
    var = jnp.mean(cen * cen, axis=-1, keepdims=True)
    zf = cen * lax.rsqrt(var + EPS) * lng_ref[...] + lnb_ref[...]
    y_conv = _dot((zf * _sigmoid(zf)).astype(BF16), wco_ref[...])
    uq, uk, uv, sg = (u_ref[:, c0:c0 + RET_WIDTH] for c0 in (0, o_q, o_k, o_v))
    gate_pre = u_ref[:, o_g:o_g + 2 * d]

    cos, sin = cos_ref[...], sin_ref[...]
    heads = []
    for hd in range(RET_HEADS):
        sl = slice(hd * HEAD_DIM, (hd + 1) * HEAD_DIM)
        qh = uq[:, sl]
        qh = qh * cos + pltpu.roll(qh, HEAD_DIM // 2, 1) * sin
        kh = uk[:, sl]
        kh = (kh * cos + pltpu.roll(kh, HEAD_DIM // 2, 1) * sin) * QK_SCALE
        vh = uv[:, sl]
        rows = []
        for c in range(ts // RET_CHUNK):
            r = slice(c * RET_CHUNK, (c + 1) * RET_CHUNK)
            qc = qh[r].astype(BF16)
            kc = kh[r]
            vc = vh[r].astype(BF16)
            sc = _dot_nt(qc, kc.astype(BF16)) * decay_ref[hd]
            inner = _dot(sc.astype(BF16), vc)
            st = state_ref[hd]
            cross = _dot(qc, st.astype(BF16)) * xi_ref[hd]
            kz = (kc * zeta_ref[hd]).astype(BF16)
            state_ref[hd] = chunk_decay[hd] * st + _dot_tn(kz, vc)
            rows.append(inner + cross)
        rh = jnp.concatenate(rows, axis=0)
        heads.append(rh * lax.rsqrt(jnp.mean(rh * rh, axis=-1, keepdims=True) + EPS))
    r = jnp.concatenate(heads, axis=1) * rg_ref[...]
    r = (sg * _sigmoid(sg)) * r
    y_ret = _dot(r.astype(BF16), wro_ref[...])

    gate = _sigmoid(gate_pre + bg_ref[...])
    merged = (gate[:, 0:d] * y_ret + gate[:, d:2 * d] * y_conv).astype(BF16)
    o_ref[0] = x + _dot(merged, wmo_ref[...])
    if add_moe:
        @pl.when(lin == total - 1)
        def _():
            _wait_gathered(ts, y_hbm, rows_ref, 1 - slot, sem)


def _mixer(x, cos, sin, g, w_in, b_gate, ret_g, w_ret_out, conv_w, conv_b, ln_g, ln_b, w_conv_out, w_mix_out,
           moe=None):
    b, s, d = x.shape
    ts = MIXER_TILE
    decay, zeta, xi, chunk_decay = _retention_tables()
    in_cols = w_in.shape[1]
    cw = jnp.zeros((CONV_HALO, CONV_CH), F32).at[:CONV_WIDTH].set(conv_w)
    row = lambda v: v[None, :]
    tab = (RET_HEADS, RET_CHUNK, HEAD_DIM)
    add_moe = moe is not None
    in_specs = [
        pl.BlockSpec((1, ts, d), lambda i, j, *_: (i, j, 0)),
        pl.BlockSpec((ts, HEAD_DIM), lambda i, j, *_: (j, 0)),
        pl.BlockSpec((ts, HEAD_DIM), lambda i, j, *_: (j, 0)),
        _const_spec((1, d)), _const_spec((d, in_cols)), _const_spec((1, 2 * d)),
        _const_spec((1, RET_WIDTH)), _const_spec((RET_WIDTH, d)),
        _const_spec((CONV_HALO, CONV_CH)), _const_spec((1, CONV_CH)), _const_spec((1, CONV_CH)),
        _const_spec((1, CONV_CH)), _const_spec((CONV_CH, d)), _const_spec((d, d)),
        _const_spec((RET_HEADS, RET_CHUNK, RET_CHUNK)), _const_spec(tab), _const_spec(tab),
    ]
    scratch = [pltpu.VMEM((RET_HEADS, HEAD_DIM, HEAD_DIM), F32),
               pltpu.VMEM((CONV_HALO + ts, CONV_CH), F32),
               pltpu.VMEM((7, ts + 8, CONV_CH), F32),
               pltpu.VMEM((ts, in_cols - 2 * CONV_CH), F32)]
    args = [x, cos, sin, row(g), w_in.astype(BF16), row(b_gate), row(ret_g), w_ret_out.astype(BF16), cw,
            row(conv_b), row(ln_g), row(ln_b), w_conv_out.astype(BF16), w_mix_out.astype(BF16), decay, zeta, xi]
    if add_moe:
        y_sorted, pos = moe
        in_specs.append(pl.BlockSpec(memory_space=pl.ANY))
        scratch += [pltpu.VMEM((2, ts, d), F32), pltpu.SemaphoreType.DMA((2,))]
        args = [pos] + args + [y_sorted]
    grid_spec = pltpu.PrefetchScalarGridSpec(
        num_scalar_prefetch=1 if add_moe else 0,
        grid=(b, s // ts),
        in_specs=in_specs,
        out_specs=pl.BlockSpec((1, ts, d), lambda i, j, *_: (i, j, 0)),
        scratch_shapes=scratch,
    )
    return pl.pallas_call(
        functools.partial(_mixer_kernel, chunk_decay, add_moe),
        name="mixer",
        grid_spec=grid_spec,
        out_shape=jax.ShapeDtypeStruct((b, s, d), F32),
        compiler_params=pltpu.CompilerParams(dimension_semantics=("arbitrary", "arbitrary"),
                                             vmem_limit_bytes=VMEM_LIMIT),
    )(*args)


def _route(logits, bias):
    scores = _sigmoid(logits)
    sel = scores + bias
    one, zero = jnp.float32(1.0), jnp.float32(0.0)
    top, gscore = [], []
    for g in range(N_GROUPS):
        a = [sel[EXPERTS_PER_GROUP * g + i:EXPERTS_PER_GROUP * g + i + 1, :] for i in range(EXPERTS_PER_GROUP)]
        tg, sg = [], None
        for i in range(EXPERTS_PER_GROUP):
            rank = None
            for j in range(EXPERTS_PER_GROUP):
                if j == i:
                    continue
                ahead = (a[j] >= a[i]) if j < i else (a[j] > a[i])
                ahead = jnp.where(ahead, one, zero)
                rank = ahead if rank is None else rank + ahead
            in_top = rank < 2.0
            tg.append(in_top)
            contrib = jnp.where(in_top, a[i], zero)
            sg = contrib if sg is None else sg + contrib
        top.append(tg)
        gscore.append(sg)
    cls = jnp.zeros_like(gscore[0])
    w_lo = jnp.zeros_like(cls)
    w_hi = jnp.zeros_like(cls)
    for g in range(N_GROUPS):
        behind = None
        for g2 in range(N_GROUPS):
            if g2 == g:
                continue
            ahead = (gscore[g2] >= gscore[g]) if g2 < g else (gscore[g2] > gscore[g])
            behind = ahead if behind is None else (behind | ahead)
        best = jnp.logical_not(behind)
        for p, (i, j) in enumerate(PAIRS):
            active = best & top[g][i] & top[g][j]
            e_lo, e_hi = EXPERTS_PER_GROUP * g + i, EXPERTS_PER_GROUP * g + j
            cls = jnp.where(active, jnp.float32(len(PAIRS) * g + p), cls)
            w_lo = jnp.where(active, scores[e_lo:e_lo + 1, :], w_lo)
            w_hi = jnp.where(active, scores[e_hi:e_hi + 1, :], w_hi)
    den = w_lo + w_hi
    return cls, w_lo / den, w_hi / den


def _xattn_route_kernel(x_ref, kv_ref, gx_ref, wq_ref, wo_ref, gf_ref, wrh_ref, wrl_ref, br_ref,
                        init_hbm, o_ref, meta_ref, alloc_ref, hs_hbm,
                        fill_ref, cur_ref, next_ref, rows_ref, posv_ref, pos_smem, sem, sem_pos):
    del init_hbm
    ts = x_ref.shape[1]
    d = x_ref.shape[2]
    tm = float(MOE_TILE)
    lin = pl.program_id(0) * pl.num_programs(1) + pl.program_id(1)
    total = pl.num_programs(0) * pl.num_programs(1)
    slot = lin % 2

    def wait_rows(sl):
        pltpu.make_async_copy(rows_ref.at[sl], hs_hbm.at[pl.ds(0, ts)], sem.at[sl]).wait()

    @pl.when(lin == 0)
    def _():
        fill_ref[...] = jnp.full(fill_ref.shape, tm, F32)
        cur_ref[...] = jnp.zeros_like(cur_ref)
        next_ref[...] = jnp.zeros_like(next_ref)
        rows_ref[1] = jnp.zeros(rows_ref.shape[1:], F32)
        for r in range(ts):
            pos_smem[0, r] = hs_hbm.shape[0] - ts + r

    def pos_to_smem():
        return pltpu.make_async_copy(posv_ref.at[pl.ds(0, 1)], pos_smem, sem_pos.at[0])

    def scatter_rows(sl):
        for r in range(ts):
            pltpu.make_async_copy(rows_ref.at[sl, pl.ds(r, 1)], hs_hbm.at[pl.ds(pos_smem[0, r], 1)],
                                  sem.at[sl]).start()

    @pl.when(lin >= 1)
    def _():
        pos_to_smem().wait()

    scatter_rows(1 - slot)

    x = x_ref[0]
    h = _rmsnorm(x, gx_ref[...]).astype(BF16)
    q = _dot(h, wq_ref[...])
    kv = kv_ref[0]
    heads = []
    for hd in range(XATTN_HEADS):
        sl = slice(hd * HEAD_DIM, (hd + 1) * HEAD_DIM)
        vs = slice(XATTN_WIDTH + hd * HEAD_DIM, XATTN_WIDTH + (hd + 1) * HEAD_DIM)
        sc = _dot_nt(q[:, sl].astype(BF16), kv[:, sl]) * QK_SCALE
        sc = sc - jnp.max(sc, axis=-1, keepdims=True)
        p = jnp.exp(sc)
        p = p / jnp.sum(p, axis=-1, keepdims=True)
        heads.append(_dot(p.astype(BF16), kv[:, vs]))
    att = jnp.concatenate(heads, axis=1).astype(BF16)
    x2 = x + _dot(att, wo_ref[...])
    o_ref[0] = x2

    hf = _rmsnorm(x2, gf_ref[...])
    h_hi = hf.astype(BF16)
    h_lo = (hf - h_hi.astype(F32)).astype(BF16)
    lg = _dot(h_hi, wrh_ref[...]) + (_dot(h_lo, wrh_ref[...]) + _dot(h_hi, wrl_ref[...]))
    logits = lg.T[0:N_EXPERTS, :]
    cls, w_lo, w_hi = _route(logits, br_ref[...])

    one, zero = jnp.float32(1.0), jnp.float32(0.0)
    crow = lax.broadcasted_iota(jnp.int32, (CLASS_ROWS, ts), 0).astype(F32)
    onehot = jnp.where(crow == cls, one, zero)
    before = lax.broadcasted_iota(jnp.int32, (ts, ts), 0) < lax.broadcasted_iota(jnp.int32, (ts, ts), 1)
    prefix = _dot(onehot.astype(BF16), jnp.where(before, one, zero).astype(BF16))
    count = jnp.sum(onehot, axis=1, keepdims=True)
    fill, cur, nxt = fill_ref[...], cur_ref[...], next_ref[...]
    need = jnp.where(fill + count > tm, one, zero)
    lower = (lax.broadcasted_iota(jnp.int32, (CLASS_ROWS, CLASS_ROWS), 1)
             < lax.broadcasted_iota(jnp.int32, (CLASS_ROWS, CLASS_ROWS), 0))
    opened_before = _dot(jnp.where(lower, one, zero).astype(BF16),
                         jnp.broadcast_to(need, (CLASS_ROWS, META_LANES)).astype(BF16))[:, 0:1]
    new_tile = nxt + opened_before
    per_token = lambda v: jnp.sum(onehot * v, axis=0, keepdims=True)
    slot_in_cur = per_token(fill) + per_token(prefix)
    pos = jnp.where(slot_in_cur >= tm, per_token(new_tile) * tm + (slot_in_cur - tm),
                    per_token(cur) * tm + slot_in_cur)
    fill_ref[...] = jnp.where(need > zero, fill + count - tm, fill + count)
    cur_ref[...] = jnp.where(need > zero, new_tile, cur)
    next_ref[...] = nxt + jnp.sum(need, axis=0, keepdims=True)
    lane = lax.broadcasted_iota(jnp.int32, (CLASS_ROWS, META_LANES), 1)
    alloc_ref[...] = jnp.where(lane == 0, need, jnp.where(lane == 1, new_tile, zero))

    meta = jnp.concatenate([cls, w_lo, w_hi, pos, jnp.zeros((META_LANES - 4, ts), F32)], axis=0)
    meta_ref[...] = meta[0:8, :]

    @pl.when(lin >= 1)
    def _():
        wait_rows(slot)

    rows_ref[slot, :, 0:d] = hf
    rows_ref[slot, :, d:d + META_LANES] = meta.T
    posv_ref[...] = jnp.broadcast_to(pos.astype(jnp.int32), posv_ref.shape)
    pos_to_smem().start()

    @pl.when(lin == total - 1)
    def _():
        pos_to_smem().wait()
        scatter_rows(slot)
        wait_rows(slot)
        wait_rows(1 - slot)


def _xattn_route(x, kv, g_x, w_xq, w_xo, g_f, w_router, b_router, n_rows):
    b, s, d = x.shape
    ts = SEQ_TILE
    assert ts <= MOE_TILE
    m = kv.shape[1]
    nt = s // ts
    w = d + META_LANES
    row = lambda v: v[None, :]
    wr = jnp.zeros((d, META_LANES), F32).at[:, :N_EXPERTS].set(w_router)
    wr_hi = wr.astype(BF16)
    wr_lo = (wr - wr_hi.astype(F32)).astype(BF16)
    return pl.pallas_call(
        _xattn_route_kernel,
        name="xattn_route",
        grid=(b, nt),
        in_specs=[
            pl.BlockSpec((1, ts, d), lambda i, j: (i, j, 0)),
            pl.BlockSpec((1, m, 2 * XATTN_WIDTH), lambda i, j: (i, 0, 0)),
            _const_spec((1, d)), _const_spec((d, XATTN_WIDTH)), _const_spec((XATTN_WIDTH, d)),
            _const_spec((1, d)), _const_spec((d, META_LANES)), _const_spec((d, META_LANES)),
            _const_spec((N_EXPERTS, 1)),
            pl.BlockSpec(memory_space=pl.ANY),
        ],
        out_specs=[
            pl.BlockSpec((1, ts, d), lambda i, j: (i, j, 0)),
            pl.BlockSpec((8, ts), lambda i, j: (0, i * nt + j)),
            pl.BlockSpec((CLASS_ROWS, META_LANES), lambda i, j: (i * nt + j, 0)),
            pl.BlockSpec(memory_space=pl.ANY),
        ],
        out_shape=[
            jax.ShapeDtypeStruct((b, s, d), F32),
            jax.ShapeDtypeStruct((8, b * s), F32),
            jax.ShapeDtypeStruct((b * nt * CLASS_ROWS, META_LANES), F32),
            jax.ShapeDtypeStruct((n_rows, w), F32),
        ],
        scratch_shapes=[pltpu.VMEM((CLASS_ROWS, 1), F32), pltpu.VMEM((CLASS_ROWS, 1), F32),
                        pltpu.VMEM((1, 1), F32), pltpu.VMEM((2, ts, w), F32),
                        pltpu.VMEM((8, ts), jnp.int32), pltpu.SMEM((1, ts), jnp.int32),
                        pltpu.SemaphoreType.DMA((2,)), pltpu.SemaphoreType.DMA((1,))],
        input_output_aliases={9: 3},
        compiler_params=pltpu.CompilerParams(dimension_semantics=("arbitrary", "arbitrary"),
                                             vmem_limit_bytes=VMEM_LIMIT),
    )(x, kv, row(g_x), w_xq.astype(BF16), w_xo.astype(BF16), row(g_f), wr_hi, wr_lo, b_router[:, None],
      jnp.zeros((n_rows, w), F32))


def _tile_tables(alloc, n_tiles):
    opened = alloc[:, 0] > 0.0
    tile_of = alloc[:, 1].astype(jnp.int32)
    cls_of = jnp.arange(alloc.shape[0], dtype=jnp.int32) % CLASS_ROWS
    tile_ids = jnp.arange(n_tiles, dtype=jnp.int32)
    hit = opened[None, :] & (tile_of[None, :] == tile_ids[:, None])
    tile_cls = jnp.sum(jnp.where(hit, cls_of[None, :], 0), axis=1)
    used = jnp.any(hit, axis=1)
    n_used = jnp.sum(used.astype(jnp.int32))
    key = jnp.where(used, tile_cls, N_CLASSES)
    ahead = (key[None, :] < key[:, None]) | ((key[None, :] == key[:, None]) & (tile_ids[None, :] < tile_ids[:, None]))
    rank = jnp.sum(ahead.astype(jnp.int32), axis=1)
    order = jnp.sum(jnp.where(rank[None, :] == tile_ids[:, None], tile_ids[None, :], 0), axis=1)
    order_in = order[jnp.minimum(tile_ids, n_used - 1)]
    step_cls = tile_cls[order_in]
    pair = step_cls % len(PAIRS)
    base = EXPERTS_PER_GROUP * (step_cls // len(PAIRS))
    lo_of = jnp.asarray([p[0] for p in PAIRS], jnp.int32)
    hi_of = jnp.asarray([p[1] for p in PAIRS], jnp.int32)
    i32 = lambda v: v.astype(jnp.int32)
    return i32(order_in), i32(order), i32(base + lo_of[pair]), i32(base + hi_of[pair]), i32(n_used.reshape(1))


def _moe_kernel(oin_ref, oout_ref, elo_ref, ehi_ref, nused_ref, rows_ref, wg_lo, wg_hi, wu_lo, wu_hi, wd_lo, wd_hi, y_ref):
    tm, d = y_ref.shape
    i = pl.program_id(0)
    used = i < nused_ref[0]

    @pl.when(used)
    def _():
        hb = rows_ref[:, 0:d].astype(BF16)
        y = None
        for col, wg, wu, wd in ((d + 1, wg_lo, wu_lo, wd_lo), (d + 2, wg_hi, wu_hi, wd_hi)):
            gate = _dot(hb, wg[0].astype(BF16))
            act = (gate * _sigmoid(gate)) * _dot(hb, wu[0].astype(BF16))
            part = rows_ref[:, col:col + 1] * _dot(act.astype(BF16), wd[0].astype(BF16))
            y = part if y is None else y + part
        y_ref[...] = y

    @pl.when(jnp.logical_not(used))
    def _():
        y_ref[...] = jnp.zeros_like(y_ref)


def _moe(hs, order_in, order_out, e_lo, e_hi, n_used, w_gate, w_up, w_down, n_tiles, tm):
    w = hs.shape[1]
    d = w - META_LANES
    f = w_gate.shape[2]
    lo3 = lambda i, oin, oout, elo, ehi, nu: (elo[i], 0, 0)
    hi3 = lambda i, oin, oout, elo, ehi, nu: (ehi[i], 0, 0)
    grid_spec = pltpu.PrefetchScalarGridSpec(
        num_scalar_prefetch=5,
        grid=(n_tiles,),
        in_specs=[
            pl.BlockSpec((tm, w), lambda i, oin, oout, elo, ehi, nu: (oin[i], 0)),
            pl.BlockSpec((1, d, f), lo3), pl.BlockSpec((1, d, f), hi3),
            pl.BlockSpec((1, d, f), lo3), pl.BlockSpec((1, d, f), hi3),
            pl.BlockSpec((1, f, d), lo3), pl.BlockSpec((1, f, d), hi3),
        ],
        out_specs=pl.BlockSpec((tm, d), lambda i, oin, oout, elo, ehi, nu: (oout[i], 0)),
    )
    return pl.pallas_call(
        _moe_kernel,
        name="moe_experts",
        grid_spec=grid_spec,
        out_shape=jax.ShapeDtypeStruct((n_tiles * tm, d), F32),
        compiler_params=pltpu.CompilerParams(dimension_semantics=("arbitrary",), vmem_limit_bytes=VMEM_LIMIT),
    )(order_in, order_out, e_lo, e_hi, n_used, hs, w_gate, w_gate, w_up, w_up, w_down, w_down)


def _combine_kernel(final_norm, pos_ref, x_ref, y_hbm, g_ref, o_ref, rows_ref, sem):
    tc = x_ref.shape[0]
    i = pl.program_id(0)
    n = pl.num_programs(0)

    def issue(tile, slot):
        base = tile * tc
        for r in range(tc):
            pltpu.make_async_copy(y_hbm.at[pl.ds(pos_ref[base + r], 1)], rows_ref.at[slot, pl.ds(r, 1)],
                                  sem.at[slot]).start()

    @pl.when(i == 0)
    def _():
        issue(0, 0)

    for slot in range(2):
        @pl.when(i % 2 == slot)
        def _():
            @pl.when(i + 1 < n)
            def _():
                issue(i + 1, 1 - slot)

            pltpu.make_async_copy(y_hbm.at[pl.ds(0, tc)], rows_ref.at[slot], sem.at[slot]).wait()
            x = x_ref[...] + rows_ref[slot]
            if final_norm:
                x = _rmsnorm(x, g_ref[...])
            o_ref[...] = x


def _combine(x, y_sorted, pos, g, final_norm):
    t, d = x.shape
    tc = COMBINE_TILE
    grid_spec = pltpu.PrefetchScalarGridSpec(
        num_scalar_prefetch=1,
        grid=(t // tc,),
        in_specs=[pl.BlockSpec((tc, d), lambda i, p: (i, 0)), pl.BlockSpec(memory_space=pl.ANY),
                  pl.BlockSpec((1, d), lambda i, p: (0, 0))],
        out_specs=pl.BlockSpec((tc, d), lambda i, p: (i, 0)),
        scratch_shapes=[pltpu.VMEM((2, tc, d), F32), pltpu.SemaphoreType.DMA((2,))],
    )
    return pl.pallas_call(
        functools.partial(_combine_kernel, final_norm),
        name="combine",
        grid_spec=grid_spec,
        out_shape=jax.ShapeDtypeStruct((t, d), F32),
        compiler_params=pltpu.CompilerParams(dimension_semantics=("arbitrary",), vmem_limit_bytes=VMEM_LIMIT),
    )(pos, x, y_sorted, g[None, :])


def kernel(x, mem, positions, norm_mix_g, w_in, b_branch_gate, ret_norm_g, w_ret_out, conv_w, conv_b, conv_ln_g, conv_ln_b, w_conv_out, w_mix_out, norm_xattn_g, norm_mem_g, w_xq, w_xkv, w_xo, norm_ffn_g, w_router, b_router, w_exp_gate, w_exp_up, w_exp_down, norm_final_g):
    b, s, d = x.shape
    depth = w_in.shape[0]
    t = b * s
    assert s % SEQ_TILE == 0 and s % MIXER_TILE == 0 and MIXER_TILE % RET_CHUNK == 0
    assert t % MOE_TILE == 0 and t % COMBINE_TILE == 0
    tm = MOE_TILE
    n_tiles = t // tm + N_CLASSES
    cos, sin = _rope_tables(positions)
    n_exp, _, f = w_exp_gate.shape[1:]
    wg_all = w_exp_gate.reshape(depth * n_exp, d, f)
    wu_all = w_exp_up.reshape(depth * n_exp, d, f)
    wd_all = w_exp_down.reshape(depth * n_exp, f, d)
    moe = None
    for l in range(depth):
        x = _mixer(x, cos, sin, norm_mix_g[l], w_in[l], b_branch_gate[l], ret_norm_g[l], w_ret_out[l],
                   conv_w[l], conv_b[l], conv_ln_g[l], conv_ln_b[l], w_conv_out[l], w_mix_out[l], moe)
        kv = _mem_kv(mem, norm_mem_g[l], w_xkv[l].astype(BF16))
        x, meta, alloc, hs = _xattn_route(x, kv, norm_xattn_g[l], w_xq[l], w_xo[l], norm_ffn_g[l],
                                          w_router, b_router, (n_tiles + 1) * tm)
        pos = meta[3].astype(jnp.int32)
        order_in, order_out, e_lo, e_hi, n_used = _tile_tables(alloc, n_tiles)
        y_sorted = _moe(hs, order_in, order_out, e_lo + l * n_exp, e_hi + l * n_exp, n_used, wg_all, wu_all, wd_all,
                        n_tiles, tm)
        moe = (y_sorted, pos)
    return _combine(x.reshape(t, d), y_sorted, pos, norm_final_g, True).reshape(b, s, d)
```

```python
import functools

import jax
import jax.numpy as jnp
import numpy as np
from jax import lax
from jax.experimental import pallas as pl
from jax.experimental.pallas import tpu as pltpu

F32 = jnp.float32
BF16 = jnp.bfloat16

RET_HEADS = 4
HEAD_DIM = 128
RET_WIDTH = RET_HEADS * HEAD_DIM
RET_CHUNK = 128
CONV_CH = 512
CONV_WIDTH = 31
CONV_HALO = 32
XATTN_HEADS = 4
XATTN_WIDTH = XATTN_HEADS * HEAD_DIM
N_EXPERTS = 16
N_GROUPS = 4
EXPERTS_PER_GROUP = 4
ROPE_BASE = 10000.0
EPS = 1e-6
QK_SCALE = HEAD_DIM ** -0.5

PAIRS = ((0, 1), (0, 2), (0, 3), (1, 2), (1, 3), (2, 3))
N_CLASSES = N_GROUPS * len(PAIRS)
CLASS_ROWS = 32
META_LANES = 128

SEQ_TILE = 256
MIXER_TILE = 256
MOE_TILE = 256
COMBINE_TILE = 256
VMEM_LIMIT = 56 * 1024 * 1024


def _rmsnorm(x, g):
    return x * lax.rsqrt(jnp.mean(x * x, axis=-1, keepdims=True) + EPS) * g


def _sigmoid(x):
    return 1.0 / (1.0 + jnp.exp(-x))


def _dot(a, b):
    return jnp.dot(a, b, preferred_element_type=F32)


def _dot_nt(a, b):
    return lax.dot_general(a, b, (((1,), (1,)), ((), ())), preferred_element_type=F32)


def _dot_tn(a, b):
    return lax.dot_general(a, b, (((0,), (0,)), ((), ())), preferred_element_type=F32)


def _const_spec(shape):
    n = len(shape)
    return pl.BlockSpec(shape, lambda *_: (0,) * n, pipeline_mode=pl.Buffered(1))


def _rope_kernel(pos_ref, invf_ref, cos_ref, sin_ref):
    ang = pos_ref[...] * invf_ref[...]
    lane = lax.broadcasted_iota(jnp.int32, ang.shape, 1)
    s = jnp.sin(ang)
    cos_ref[...] = jnp.cos(ang)
    sin_ref[...] = jnp.where(lane < HEAD_DIM // 2, -s, s)


def _rope_tables(positions):
    s = positions.shape[0]
    half = HEAD_DIM // 2
    inv_freq = ROPE_BASE ** (-jnp.arange(half, dtype=F32) / half)
    invf = jnp.concatenate([inv_freq, inv_freq])[None, :]
    pos = positions.astype(F32)[:, None]
    ts = min(s, 1024)
    return pl.pallas_call(
        _rope_kernel,
        name="rope_tables",
        grid=(s // ts,),
        in_specs=[pl.BlockSpec((ts, 1), lambda i: (i, 0)), pl.BlockSpec((1, HEAD_DIM), lambda i: (0, 0))],
        out_specs=[pl.BlockSpec((ts, HEAD_DIM), lambda i: (i, 0))] * 2,
        out_shape=[jax.ShapeDtypeStruct((s, HEAD_DIM), F32)] * 2,
    )(pos, invf)


def _memkv_kernel(mem_ref, g_ref, w_ref, kv_ref):
    m = _rmsnorm(mem_ref[0], g_ref[...]).astype(BF16)
    kv_ref[0] = _dot(m, w_ref[...]).astype(BF16)


def _mem_kv(mem, g, w_xkv):
    b, m, d = mem.shape
    return pl.pallas_call(
        _memkv_kernel,
        name="mem_kv",
        grid=(b,),
        in_specs=[pl.BlockSpec((1, m, d), lambda i: (i, 0, 0)), pl.BlockSpec((1, d), lambda i: (0, 0)),
                  pl.BlockSpec((d, 2 * XATTN_WIDTH), lambda i: (0, 0))],
        out_specs=pl.BlockSpec((1, m, 2 * XATTN_WIDTH), lambda i: (i, 0, 0)),
        out_shape=jax.ShapeDtypeStruct((b, m, 2 * XATTN_WIDTH), BF16),
        compiler_params=pltpu.CompilerParams(vmem_limit_bytes=VMEM_LIMIT),
    )(mem, g[None, :], w_xkv)


def _retention_tables():
    h = np.arange(RET_HEADS, dtype=np.float64)
    log_gamma = np.log(1.0 - 2.0 ** (-5.0 - h))
    idx = np.arange(RET_CHUNK, dtype=np.float64)
    diff = idx[:, None] - idx[None, :]
    decay = np.where(diff[None] >= 0.0, np.exp(np.maximum(diff, 0.0)[None] * log_gamma[:, None, None]), 0.0)
    zeta = np.exp((RET_CHUNK - 1.0 - idx)[None, :] * log_gamma[:, None])
    xi = np.exp((idx + 1.0)[None, :] * log_gamma[:, None])
    chunk_decay = np.exp(RET_CHUNK * log_gamma)
    bc = lambda t: np.broadcast_to(t[:, :, None], (RET_HEADS, RET_CHUNK, HEAD_DIM))
    return (jnp.asarray(decay, F32), jnp.asarray(bc(zeta), F32), jnp.asarray(bc(xi), F32),
            tuple(float(c) for c in chunk_decay))


def _gather_rows(idx_ref, tile, n, src_hbm, dst_ref, slot, sem):
    base = tile * n
    for r in range(n):
        pltpu.make_async_copy(src_hbm.at[pl.ds(idx_ref[base + r], 1)], dst_ref.at[slot, pl.ds(r, 1)],
                              sem.at[slot]).start()


def _wait_gathered(n, src_hbm, dst_ref, slot, sem):
    pltpu.make_async_copy(src_hbm.at[pl.ds(0, n)], dst_ref.at[slot], sem.at[slot]).wait()


def _mixer_kernel(chunk_decay, add_moe, *refs):
    if add_moe:
        pos_ref, refs = refs[0], refs[1:]
    (x_ref, cos_ref, sin_ref, g_ref, win_ref, bg_ref, rg_ref, wro_ref, cw_ref, cb_ref, lng_ref, lnb_ref,
     wco_ref, wmo_ref, decay_ref, zeta_ref, xi_ref) = refs[:17]
    refs = refs[17:]
    if add_moe:
        y_hbm, refs = refs[0], refs[1:]
    o_ref, hs_hbm, state_ref, zext_ref, phase_ref, u_ref, zeros_ref, zsem = refs[:8]
    ts = x_ref.shape[1]
    d = x_ref.shape[2]
    lin = pl.program_id(0) * pl.num_programs(1) + pl.program_id(1)
    total = pl.num_programs(0) * pl.num_programs(1)
    slot = lin % 2

    @pl.when(pl.program_id(1) == 0)
    def _():
        state_ref[...] = jnp.zeros_like(state_ref)
        zext_ref[0:CONV_HALO, :] = jnp.zeros((CONV_HALO, CONV_CH), F32)

    chunk = zeros_ref.shape[0]

    def zero_chunk(k):
        start = jnp.minimum(k * chunk, hs_hbm.shape[0] - chunk)
        return pltpu.make_async_copy(zeros_ref, hs_hbm.at[pl.ds(start, chunk)], zsem.at[0])

    @pl.when(lin == 0)
    def _():
        zeros_ref[...] = jnp.zeros_like(zeros_ref)

    @pl.when(lin >= 1)
    def _():
        zero_chunk(lin - 1).wait()

    zero_chunk(lin).start()

    x = x_ref[0]
    if add_moe:
        rows_ref, sem = refs[8:10]

        @pl.when(lin == 0)
        def _():
            _gather_rows(pos_ref, 0, ts, y_hbm, rows_ref, 0, sem)

        _wait_gathered(ts, y_hbm, rows_ref, slot, sem)
        x = x + rows_ref[slot]
        _gather_rows(pos_ref, jnp.minimum(lin + 1, total - 1), ts, y_hbm, rows_ref, 1 - slot, sem)
    h = _rmsnorm(x, g_ref[...]).astype(BF16)

    def proj(a, b):
        return _dot(h, win_ref[:, a:b])

    o_q, o_k, o_v, o_g = RET_WIDTH, 2 * RET_WIDTH, 3 * RET_WIDTH, 4 * RET_WIDTH
    o_c = o_g + 2 * CONV_CH

    ca, cg = proj(o_g, o_g + CONV_CH), proj(o_g + CONV_CH, o_c)
    zext_ref[CONV_HALO:CONV_HALO + ts, :] = ca * _sigmoid(cg)
    first = CONV_HALO - (CONV_WIDTH - 1)
    other_cols = [(c0, c0 + 256) for c0 in range(0, o_g, 256)] + [(c0, c0 + 256) for c0 in range(o_c, o_c + 2 * d, 256)]
    per_phase = len(other_cols) // 8
    acc = None
    for b in range(8):
        for k in range(b * per_phase, (b + 1) * per_phase):
            u_ref[:, 256 * k:256 * (k + 1)] = proj(*other_cols[k])
        rows = ts if b == 0 else ts + 8
        part = None
        for a in range(CONV_HALO // 8 + 1):
            w = 8 * a + b - first
            if 0 <= w < CONV_WIDTH:
                term = zext_ref[8 * a:8 * a + rows, :] * cw_ref[w:w + 1, :]
                part = term if part is None else part + term
        if b == 0:
            acc = part
        else:
            phase_ref[b - 1] = part
            acc = acc + phase_ref[b - 1, b:b + ts, :]
    acc = acc + cb_ref[...]
    zext_ref[0:CONV_HALO, :] = zext_ref[ts:ts + CONV_HALO, :]
    mu = jnp.mean(acc, axis=-1, keepdims=True)
    cen = acc - mu
    var = jnp.mean(cen * cen, axis=-1, keepdims=True)
    zf = cen * lax.rsqrt(var + EPS) * lng_ref[...] + lnb_ref[...]
    y_conv = _dot((zf * _sigmoid(zf)).astype(BF16), wco_ref[...])
    uq, uk, uv, sg = (u_ref[:, c0:c0 + RET_WIDTH] for c0 in (0, o_q, o_k, o_v))
    gate_pre = u_ref[:, o_g:o_g + 2 * d]

    cos, sin = cos_ref[...], sin_ref[...]
    heads = []
    for hd in range(RET_HEADS):
        sl = slice(hd * HEAD_DIM, (hd + 1) * HEAD_DIM)
        qh = uq[:, sl]
        qh = qh * cos + pltpu.roll(qh, HEAD_DIM // 2, 1) * sin
        kh = uk[:, sl]
        kh = (kh * cos + pltpu.roll(kh, HEAD_DIM // 2, 1) * sin) * QK_SCALE
        vh = uv[:, sl]
        rows = []
        for c in range(ts // RET_CHUNK):
            r = slice(c * RET_CHUNK, (c + 1) * RET_CHUNK)
            qc = qh[r].astype(BF16)
            kc = kh[r]
            vc = vh[r].astype(BF16)
            sc = _dot_nt(qc, kc.astype(BF16)) * decay_ref[hd]
            inner = _dot(sc.astype(BF16), vc)
            st = state_ref[hd]
            cross = _dot(qc, st.astype(BF16)) * xi_ref[hd]
            kz = (kc * zeta_ref[hd]).astype(BF16)
            state_ref[hd] = chunk_decay[hd] * st + _dot_tn(kz, vc)
            rows.append(inner + cross)
        rh = jnp.concatenate(rows, axis=0)
        heads.append(rh * lax.rsqrt(jnp.mean(rh * rh, axis=-1, keepdims=True) + EPS))
    r = jnp.concatenate(heads, axis=1) * rg_ref[...]
    r = (sg * _sigmoid(sg)) * r
    y_ret = _dot(r.astype(BF16), wro_ref[...])

    gate = _sigmoid(gate_pre + bg_ref[...])
    merged = (gate[:, 0:d] * y_ret + gate[:, d:2 * d] * y_conv).astype(BF16)
    o_ref[0] = x + _dot(merged, wmo_ref[...])

    @pl.when(lin == total - 1)
    def _():
        zero_chunk(lin).wait()
        if add_moe:
            _wait_gathered(ts, y_hbm, rows_ref, 1 - slot, sem)


def _mixer(x, cos, sin, g, w_in, b_gate, ret_g, w_ret_out, conv_w, conv_b, ln_g, ln_b, w_conv_out, w_mix_out,
           n_rows, moe=None):
    b, s, d = x.shape
    ts = MIXER_TILE
    steps = b * (s // ts)
    per_step = -(-n_rows // steps)
    zero_rows = min(n_rows, -(-per_step // 8) * 8)
    decay, zeta, xi, chunk_decay = _retention_tables()
    in_cols = w_in.shape[1]
    cw = jnp.zeros((CONV_HALO, CONV_CH), F32).at[:CONV_WIDTH].set(conv_w)
    row = lambda v: v[None, :]
    tab = (RET_HEADS, RET_CHUNK, HEAD_DIM)
    add_moe = moe is not None
    in_specs = [
        pl.BlockSpec((1, ts, d), lambda i, j, *_: (i, j, 0)),
        pl.BlockSpec((ts, HEAD_DIM), lambda i, j, *_: (j, 0)),
        pl.BlockSpec((ts, HEAD_DIM), lambda i, j, *_: (j, 0)),
        _const_spec((1, d)), _const_spec((d, in_cols)), _const_spec((1, 2 * d)),
        _const_spec((1, RET_WIDTH)), _const_spec((RET_WIDTH, d)),
        _const_spec((CONV_HALO, CONV_CH)), _const_spec((1, CONV_CH)), _const_spec((1, CONV_CH)),
        _const_spec((1, CONV_CH)), _const_spec((CONV_CH, d)), _const_spec((d, d)),
        _const_spec((RET_HEADS, RET_CHUNK, RET_CHUNK)), _const_spec(tab), _const_spec(tab),
    ]
    scratch = [pltpu.VMEM((RET_HEADS, HEAD_DIM, HEAD_DIM), F32),
               pltpu.VMEM((CONV_HALO + ts, CONV_CH), F32),
               pltpu.VMEM((7, ts + 8, CONV_CH), F32),
               pltpu.VMEM((ts, in_cols - 2 * CONV_CH), F32),
               pltpu.VMEM((zero_rows, d + META_LANES), F32), pltpu.SemaphoreType.DMA((1,))]
    args = [x, cos, sin, row(g), w_in.astype(BF16), row(b_gate), row(ret_g), w_ret_out.astype(BF16), cw,
            row(conv_b), row(ln_g), row(ln_b), w_conv_out.astype(BF16), w_mix_out.astype(BF16), decay, zeta, xi]
    if add_moe:
        y_sorted, pos = moe
        in_specs.append(pl.BlockSpec(memory_space=pl.ANY))
        scratch += [pltpu.VMEM((2, ts, d), F32), pltpu.SemaphoreType.DMA((2,))]
        args = [pos] + args + [y_sorted]
    grid_spec = pltpu.PrefetchScalarGridSpec(
        num_scalar_prefetch=1 if add_moe else 0,
        grid=(b, s // ts),
        in_specs=in_specs,
        out_specs=[pl.BlockSpec((1, ts, d), lambda i, j, *_: (i, j, 0)), pl.BlockSpec(memory_space=pl.ANY)],
        scratch_shapes=scratch,
    )
    return pl.pallas_call(
        functools.partial(_mixer_kernel, chunk_decay, add_moe),
        name="mixer",
        grid_spec=grid_spec,
        out_shape=[jax.ShapeDtypeStruct((b, s, d), F32), jax.ShapeDtypeStruct((n_rows, d + META_LANES), F32)],
        compiler_params=pltpu.CompilerParams(dimension_semantics=("arbitrary", "arbitrary"),
                                             vmem_limit_bytes=VMEM_LIMIT),
    )(*args)


def _route(logits, bias):
    scores = _sigmoid(logits)
    sel = scores + bias
    one, zero = jnp.float32(1.0), jnp.float32(0.0)
    top, gscore = [], []
    for g in range(N_GROUPS):
        a = [sel[EXPERTS_PER_GROUP * g + i:EXPERTS_PER_GROUP * g + i + 1, :] for i in range(EXPERTS_PER_GROUP)]
        tg, sg = [], None
        for i in range(EXPERTS_PER_GROUP):
            rank = None
            for j in range(EXPERTS_PER_GROUP):
                if j == i:
                    continue
                ahead = (a[j] >= a[i]) if j < i else (a[j] > a[i])
                ahead = jnp.where(ahead, one, zero)
                rank = ahead if rank is None else rank + ahead
            in_top = rank < 2.0
            tg.append(in_top)
            contrib = jnp.where(in_top, a[i], zero)
            sg = contrib if sg is None else sg + contrib
        top.append(tg)
        gscore.append(sg)
    cls = jnp.zeros_like(gscore[0])
    w_lo = jnp.zeros_like(cls)
    w_hi = jnp.zeros_like(cls)
    for g in range(N_GROUPS):
        behind = None
        for g2 in range(N_GROUPS):
            if g2 == g:
                continue
            ahead = (gscore[g2] >= gscore[g]) if g2 < g else (gscore[g2] > gscore[g])
            behind = ahead if behind is None else (behind | ahead)
        best = jnp.logical_not(behind)
        for p, (i, j) in enumerate(PAIRS):
            active = best & top[g][i] & top[g][j]
            e_lo, e_hi = EXPERTS_PER_GROUP * g + i, EXPERTS_PER_GROUP * g + j
            cls = jnp.where(active, jnp.float32(len(PAIRS) * g + p), cls)
            w_lo = jnp.where(active, scores[e_lo:e_lo + 1, :], w_lo)
            w_hi = jnp.where(active, scores[e_hi:e_hi + 1, :], w_hi)
    den = w_lo + w_hi
    return cls, w_lo / den, w_hi / den


def _xattn_route_kernel(x_ref, kv_ref, gx_ref, wq_ref, wo_ref, gf_ref, wrh_ref, wrl_ref, br_ref,
                        init_hbm, o_ref, meta_ref, alloc_ref, hs_hbm,
                        fill_ref, cur_ref, next_ref, rows_ref, posv_ref, pos_smem, sem, sem_pos):
    del init_hbm
    ts = x_ref.shape[1]
    d = x_ref.shape[2]
    tm = float(MOE_TILE)
    lin = pl.program_id(0) * pl.num_programs(1) + pl.program_id(1)
    total = pl.num_programs(0) * pl.num_programs(1)
    slot = lin % 2

    def wait_rows(sl):
        pltpu.make_async_copy(rows_ref.at[sl], hs_hbm.at[pl.ds(0, ts)], sem.at[sl]).wait()

    @pl.when(lin == 0)
    def _():
        fill_ref[...] = jnp.full(fill_ref.shape, tm, F32)
        cur_ref[...] = jnp.zeros_like(cur_ref)
        next_ref[...] = jnp.zeros_like(next_ref)
        rows_ref[1] = jnp.zeros(rows_ref.shape[1:], F32)
        for r in range(ts):
            pos_smem[0, r] = hs_hbm.shape[0] - ts + r

    def pos_to_smem():
        return pltpu.make_async_copy(posv_ref.at[pl.ds(0, 1)], pos_smem, sem_pos.at[0])

    def scatter_rows(sl):
        for r in range(ts):
            pltpu.make_async_copy(rows_ref.at[sl, pl.ds(r, 1)], hs_hbm.at[pl.ds(pos_smem[0, r], 1)],
                                  sem.at[sl]).start()

    @pl.when(lin >= 1)
    def _():
        pos_to_smem().wait()

    scatter_rows(1 - slot)

    x = x_ref[0]
    h = _rmsnorm(x, gx_ref[...]).astype(BF16)
    q = _dot(h, wq_ref[...])
    kv = kv_ref[0]
    heads = []
    for hd in range(XATTN_HEADS):
        sl = slice(hd * HEAD_DIM, (hd + 1) * HEAD_DIM)
        vs = slice(XATTN_WIDTH + hd * HEAD_DIM, XATTN_WIDTH + (hd + 1) * HEAD_DIM)
        sc = _dot_nt(q[:, sl].astype(BF16), kv[:, sl]) * QK_SCALE
        sc = sc - jnp.max(sc, axis=-1, keepdims=True)
        p = jnp.exp(sc)
        p = p / jnp.sum(p, axis=-1, keepdims=True)
        heads.append(_dot(p.astype(BF16), kv[:, vs]))
    att = jnp.concatenate(heads, axis=1).astype(BF16)
    x2 = x + _dot(att, wo_ref[...])
    o_ref[0] = x2

    hf = _rmsnorm(x2, gf_ref[...])
    h_hi = hf.astype(BF16)
    h_lo = (hf - h_hi.astype(F32)).astype(BF16)
    lg = _dot(h_hi, wrh_ref[...]) + (_dot(h_lo, wrh_ref[...]) + _dot(h_hi, wrl_ref[...]))
    logits = lg.T[0:N_EXPERTS, :]
    cls, w_lo, w_hi = _route(logits, br_ref[...])

    one, zero = jnp.float32(1.0), jnp.float32(0.0)
    crow = lax.broadcasted_iota(jnp.int32, (CLASS_ROWS, ts), 0).astype(F32)
    onehot = jnp.where(crow == cls, one, zero)
    before = lax.broadcasted_iota(jnp.int32, (ts, ts), 0) < lax.broadcasted_iota(jnp.int32, (ts, ts), 1)
    prefix = _dot(onehot.astype(BF16), jnp.where(before, one, zero).astype(BF16))
    count = jnp.sum(onehot, axis=1, keepdims=True)
    fill, cur, nxt = fill_ref[...], cur_ref[...], next_ref[...]
    need = jnp.where(fill + count > tm, one, zero)
    lower = (lax.broadcasted_iota(jnp.int32, (CLASS_ROWS, CLASS_ROWS), 1)
             < lax.broadcasted_iota(jnp.int32, (CLASS_ROWS, CLASS_ROWS), 0))
    opened_before = _dot(jnp.where(lower, one, zero).astype(BF16),
                         jnp.broadcast_to(need, (CLASS_ROWS, META_LANES)).astype(BF16))[:, 0:1]
    new_tile = nxt + opened_before
    per_token = lambda v: jnp.sum(onehot * v, axis=0, keepdims=True)
    slot_in_cur = per_token(fill) + per_token(prefix)
    pos = jnp.where(slot_in_cur >= tm, per_token(new_tile) * tm + (slot_in_cur - tm),
                    per_token(cur) * tm + slot_in_cur)
    fill_ref[...] = jnp.where(need > zero, fill + count - tm, fill + count)
    cur_ref[...] = jnp.where(need > zero, new_tile, cur)
    next_ref[...] = nxt + jnp.sum(need, axis=0, keepdims=True)
    lane = lax.broadcasted_iota(jnp.int32, (CLASS_ROWS, META_LANES), 1)
    alloc_ref[...] = jnp.where(lane == 0, need, jnp.where(lane == 1, new_tile, zero))

    meta = jnp.concatenate([cls, w_lo, w_hi, pos, jnp.zeros((META_LANES - 4, ts), F32)], axis=0)
    meta_ref[...] = meta[0:8, :]

    @pl.when(lin >= 1)
    def _():
        wait_rows(slot)

    rows_ref[slot, :, 0:d] = hf
    rows_ref[slot, :, d:d + META_LANES] = meta.T
    posv_ref[...] = jnp.broadcast_to(pos.astype(jnp.int32), posv_ref.shape)
    pos_to_smem().start()

    @pl.when(lin == total - 1)
    def _():
        pos_to_smem().wait()
        scatter_rows(slot)
        wait_rows(slot)
        wait_rows(1 - slot)


def _xattn_route(x, kv, g_x, w_xq, w_xo, g_f, w_router, b_router, hs_zero):
    n_rows = hs_zero.shape[0]
    b, s, d = x.shape
    ts = SEQ_TILE
    assert ts <= MOE_TILE
    m = kv.shape[1]
    nt = s // ts
    w = d + META_LANES
    row = lambda v: v[None, :]
    wr = jnp.zeros((d, META_LANES), F32).at[:, :N_EXPERTS].set(w_router)
    wr_hi = wr.astype(BF16)
    wr_lo = (wr - wr_hi.astype(F32)).astype(BF16)
    return pl.pallas_call(
        _xattn_route_kernel,
        name="xattn_route",
        grid=(b, nt),
        in_specs=[
            pl.BlockSpec((1, ts, d), lambda i, j: (i, j, 0)),
            pl.BlockSpec((1, m, 2 * XATTN_WIDTH), lambda i, j: (i, 0, 0)),
            _const_spec((1, d)), _const_spec((d, XATTN_WIDTH)), _const_spec((XATTN_WIDTH, d)),
            _const_spec((1, d)), _const_spec((d, META_LANES)), _const_spec((d, META_LANES)),
            _const_spec((N_EXPERTS, 1)),
            pl.BlockSpec(memory_space=pl.ANY),
        ],
        out_specs=[
            pl.BlockSpec((1, ts, d), lambda i, j: (i, j, 0)),
            pl.BlockSpec((8, ts), lambda i, j: (0, i * nt + j)),
            pl.BlockSpec((CLASS_ROWS, META_LANES), lambda i, j: (i * nt + j, 0)),
            pl.BlockSpec(memory_space=pl.ANY),
        ],
        out_shape=[
            jax.ShapeDtypeStruct((b, s, d), F32),
            jax.ShapeDtypeStruct((8, b * s), F32),
            jax.ShapeDtypeStruct((b * nt * CLASS_ROWS, META_LANES), F32),
            jax.ShapeDtypeStruct((n_rows, w), F32),
        ],
        scratch_shapes=[pltpu.VMEM((CLASS_ROWS, 1), F32), pltpu.VMEM((CLASS_ROWS, 1), F32),
                        pltpu.VMEM((1, 1), F32), pltpu.VMEM((2, ts, w), F32),
                        pltpu.VMEM((8, ts), jnp.int32), pltpu.SMEM((1, ts), jnp.int32),
                        pltpu.SemaphoreType.DMA((2,)), pltpu.SemaphoreType.DMA((1,))],
        input_output_aliases={9: 3},
        compiler_params=pltpu.CompilerParams(dimension_semantics=("arbitrary", "arbitrary"),
                                             vmem_limit_bytes=VMEM_LIMIT),
    )(x, kv, row(g_x), w_xq.astype(BF16), w_xo.astype(BF16), row(g_f), wr_hi, wr_lo, b_router[:, None], hs_zero)


def _tile_tables(alloc, n_tiles):
    opened = alloc[:, 0] > 0.0
    tile_of = alloc[:, 1].astype(jnp.int32)
    cls_of = jnp.arange(alloc.shape[0], dtype=jnp.int32) % CLASS_ROWS
    tile_ids = jnp.arange(n_tiles, dtype=jnp.int32)
    hit = opened[None, :] & (tile_of[None, :] == tile_ids[:, None])
    tile_cls = jnp.sum(jnp.where(hit, cls_of[None, :], 0), axis=1)
    used = jnp.any(hit, axis=1)
    n_used = jnp.sum(used.astype(jnp.int32))
    key = jnp.where(used, tile_cls, N_CLASSES)
    ahead = (key[None, :] < key[:, None]) | ((key[None, :] == key[:, None]) & (tile_ids[None, :] < tile_ids[:, None]))
    rank = jnp.sum(ahead.astype(jnp.int32), axis=1)
    order = jnp.sum(jnp.where(rank[None, :] == tile_ids[:, None], tile_ids[None, :], 0), axis=1)
    order_in = order[jnp.minimum(tile_ids, n_used - 1)]
    step_cls = tile_cls[order_in]
    pair = step_cls % len(PAIRS)
    base = EXPERTS_PER_GROUP * (step_cls // len(PAIRS))
    lo_of = jnp.asarray([p[0] for p in PAIRS], jnp.int32)
    hi_of = jnp.asarray([p[1] for p in PAIRS], jnp.int32)
    i32 = lambda v: v.astype(jnp.int32)
    return i32(order_in), i32(order), i32(base + lo_of[pair]), i32(base + hi_of[pair]), i32(n_used.reshape(1))


def _moe_kernel(oin_ref, oout_ref, elo_ref, ehi_ref, nused_ref, rows_ref, wg_lo, wg_hi, wu_lo, wu_hi, wd_lo, wd_hi, y_ref):
    tm, d = y_ref.shape
    i = pl.program_id(0)
    used = i < nused_ref[0]

    @pl.when(used)
    def _():
        hb = rows_ref[:, 0:d].astype(BF16)
        y = None
        for col, wg, wu, wd in ((d + 1, wg_lo, wu_lo, wd_lo), (d + 2, wg_hi, wu_hi, wd_hi)):
            gate = _dot(hb, wg[0].astype(BF16))
            act = (gate * _sigmoid(gate)) * _dot(hb, wu[0].astype(BF16))
            part = rows_ref[:, col:col + 1] * _dot(act.astype(BF16), wd[0].astype(BF16))
            y = part if y is None else y + part
        y_ref[...] = y

    @pl.when(jnp.logical_not(used))
    def _():
        y_ref[...] = jnp.zeros_like(y_ref)


def _moe(hs, order_in, order_out, e_lo, e_hi, n_used, w_gate, w_up, w_down, n_tiles, tm):
    w = hs.shape[1]
    d = w - META_LANES
    f = w_gate.shape[2]
    lo3 = lambda i, oin, oout, elo, ehi, nu: (elo[i], 0, 0)
    hi3 = lambda i, oin, oout, elo, ehi, nu: (ehi[i], 0, 0)
    grid_spec = pltpu.PrefetchScalarGridSpec(
        num_scalar_prefetch=5,
        grid=(n_tiles,),
        in_specs=[
            pl.BlockSpec((tm, w), lambda i, oin, oout, elo, ehi, nu: (oin[i], 0)),
            pl.BlockSpec((1, d, f), lo3), pl.BlockSpec((1, d, f), hi3),
            pl.BlockSpec((1, d, f), lo3), pl.BlockSpec((1, d, f), hi3),
            pl.BlockSpec((1, f, d), lo3), pl.BlockSpec((1, f, d), hi3),
        ],
        out_specs=pl.BlockSpec((tm, d), lambda i, oin, oout, elo, ehi, nu: (oout[i], 0)),
    )
    return pl.pallas_call(
        _moe_kernel,
        name="moe_experts",
        grid_spec=grid_spec,
        out_shape=jax.ShapeDtypeStruct((n_tiles * tm, d), F32),
        compiler_params=pltpu.CompilerParams(dimension_semantics=("arbitrary",), vmem_limit_bytes=VMEM_LIMIT),
    )(order_in, order_out, e_lo, e_hi, n_used, hs, w_gate, w_gate, w_up, w_up, w_down, w_down)


def _combine_kernel(final_norm, pos_ref, x_ref, y_hbm, g_ref, o_ref, rows_ref, sem):
    tc = x_ref.shape[0]
    i = pl.program_id(0)
    n = pl.num_programs(0)

    def issue(tile, slot):
        base = tile * tc
        for r in range(tc):
            pltpu.make_async_copy(y_hbm.at[pl.ds(pos_ref[base + r], 1)], rows_ref.at[slot, pl.ds(r, 1)],
                                  sem.at[slot]).start()

    @pl.when(i == 0)
    def _():
        issue(0, 0)

    for slot in range(2):
        @pl.when(i % 2 == slot)
        def _():
            @pl.when(i + 1 < n)
            def _():
                issue(i + 1, 1 - slot)

            pltpu.make_async_copy(y_hbm.at[pl.ds(0, tc)], rows_ref.at[slot], sem.at[slot]).wait()
            x = x_ref[...] + rows_ref[slot]
            if final_norm:
                x = _rmsnorm(x, g_ref[...])
            o_ref[...] = x


def _combine(x, y_sorted, pos, g, final_norm):
    t, d = x.shape
    tc = COMBINE_TILE
    grid_spec = pltpu.PrefetchScalarGridSpec(
        num_scalar_prefetch=1,
        grid=(t // tc,),
        in_specs=[pl.BlockSpec((tc, d), lambda i, p: (i, 0)), pl.BlockSpec(memory_space=pl.ANY),
                  pl.BlockSpec((1, d), lambda i, p: (0, 0))],
        out_specs=pl.BlockSpec((tc, d), lambda i, p: (i, 0)),
        scratch_shapes=[pltpu.VMEM((2, tc, d), F32), pltpu.SemaphoreType.DMA((2,))],
    )
    return pl.pallas_call(
        functools.partial(_combine_kernel, final_norm),
        name="combine",
        grid_spec=grid_spec,
        out_shape=jax.ShapeDtypeStruct((t, d), F32),
        compiler_params=pltpu.CompilerParams(dimension_semantics=("arbitrary",), vmem_limit_bytes=VMEM_LIMIT),
    )(pos, x, y_sorted, g[None, :])


def kernel(x, mem, positions, norm_mix_g, w_in, b_branch_gate, ret_norm_g, w_ret_out, conv_w, conv_b, conv_ln_g, conv_ln_b, w_conv_out, w_mix_out, norm_xattn_g, norm_mem_g, w_xq, w_xkv, w_xo, norm_ffn_g, w_router, b_router, w_exp_gate, w_exp_up, w_exp_down, norm_final_g):
    b, s, d = x.shape
    depth = w_in.shape[0]
    t = b * s
    assert s % SEQ_TILE == 0 and s % MIXER_TILE == 0 and MIXER_TILE % RET_CHUNK == 0
    assert t % MOE_TILE == 0 and t % COMBINE_TILE == 0
    tm = MOE_TILE
    n_tiles = t // tm + N_CLASSES
    cos, sin = _rope_tables(positions)
    n_exp, _, f = w_exp_gate.shape[1:]
    wg_all = w_exp_gate.reshape(depth * n_exp, d, f)
    wu_all = w_exp_up.reshape(depth * n_exp, d, f)
    wd_all = w_exp_down.reshape(depth * n_exp, f, d)
    moe = None
    for l in range(depth):
        x, hs_zero = _mixer(x, cos, sin, norm_mix_g[l], w_in[l], b_branch_gate[l], ret_norm_g[l], w_ret_out[l],
                            conv_w[l], conv_b[l], conv_ln_g[l], conv_ln_b[l], w_conv_out[l], w_mix_out[l],
                            (n_tiles + 1) * tm, moe)
        kv = _mem_kv(mem, norm_mem_g[l], w_xkv[l].astype(BF16))
        x, meta, alloc, hs = _xattn_route(x, kv, norm_xattn_g[l], w_xq[l], w_xo[l], norm_ffn_g[l],
                                          w_router, b_router, hs_zero)
        pos = meta[3].astype(jnp.int32)
        order_in, order_out, e_lo, e_hi, n_used = _tile_tables(alloc, n_tiles)
        y_sorted = _moe(hs, order_in, order_out, e_lo + l * n_exp, e_hi + l * n_exp, n_used, wg_all, wu_all, wd_all,
                        n_tiles, tm)
        moe = (y_sorted, pos)
    return _combine(x.reshape(t, d), y_sorted, pos, norm_final_g, True).reshape(b, s, d)
```

```python
import functools

import jax
import jax.numpy as jnp
import numpy as np
from jax import lax
from jax.experimental import pallas as pl
from jax.experimental.pallas import tpu as pltpu

F32 = jnp.float32
BF16 = jnp.bfloat16

RET_HEADS = 4
HEAD_DIM = 128
RET_WIDTH = RET_HEADS * HEAD_DIM
RET_CHUNK = 128
CONV_CH = 512
CONV_WIDTH = 31
CONV_HALO = 32
XATTN_HEADS = 4
XATTN_WIDTH = XATTN_HEADS * HEAD_DIM
N_EXPERTS = 16
N_GROUPS = 4
EXPERTS_PER_GROUP = 4
ROPE_BASE = 10000.0
EPS = 1e-6
QK_SCALE = HEAD_DIM ** -0.5

PAIRS = ((0, 1), (0, 2), (0, 3), (1, 2), (1, 3), (2, 3))
N_CLASSES = N_GROUPS * len(PAIRS)
PAIR_VISIT_RANK = (0, 3, 2, 5, 1, 4)
PAIR_SLOT_A = (0, 2, 3, 2, 3, 2)
PAIR_SLOT_B = (1, 0, 0, 1, 1, 3)
CLASS_ROWS = 32
META_LANES = 128

SEQ_TILE = 256
MIXER_TILE = 256
MOE_TILE = 256
COMBINE_TILE = 256
VMEM_LIMIT = 56 * 1024 * 1024


def _rmsnorm(x, g):
    return x * lax.rsqrt(jnp.mean(x * x, axis=-1, keepdims=True) + EPS) * g


def _sigmoid(x):
    return 1.0 / (1.0 + jnp.exp(-x))


def _dot(a, b):
    return jnp.dot(a, b, preferred_element_type=F32)


def _dot_nt(a, b):
    return lax.dot_general(a, b, (((1,), (1,)), ((), ())), preferred_element_type=F32)


def _dot_tn(a, b):
    return lax.dot_general(a, b, (((0,), (0,)), ((), ())), preferred_element_type=F32)


def _const_spec(shape):
    n = len(shape)
    return pl.BlockSpec(shape, lambda *_: (0,) * n, pipeline_mode=pl.Buffered(1))


def _rope_kernel(pos_ref, invf_ref, cos_ref, sin_ref):
    ang = pos_ref[...] * invf_ref[...]
    lane = lax.broadcasted_iota(jnp.int32, ang.shape, 1)
    s = jnp.sin(ang)
    cos_ref[...] = jnp.cos(ang)
    sin_ref[...] = jnp.where(lane < HEAD_DIM // 2, -s, s)


def _rope_tables(positions):
    s = positions.shape[0]
    half = HEAD_DIM // 2
    inv_freq = ROPE_BASE ** (-jnp.arange(half, dtype=F32) / half)
    invf = jnp.concatenate([inv_freq, inv_freq])[None, :]
    pos = positions.astype(F32)[:, None]
    ts = min(s, 1024)
    return pl.pallas_call(
        _rope_kernel,
        name="rope_tables",
        grid=(s // ts,),
        in_specs=[pl.BlockSpec((ts, 1), lambda i: (i, 0)), pl.BlockSpec((1, HEAD_DIM), lambda i: (0, 0))],
        out_specs=[pl.BlockSpec((ts, HEAD_DIM), lambda i: (i, 0))] * 2,
        out_shape=[jax.ShapeDtypeStruct((s, HEAD_DIM), F32)] * 2,
    )(pos, invf)


def _memkv_kernel(mem_ref, g_ref, w_ref, kv_ref):
    m = _rmsnorm(mem_ref[0], g_ref[...]).astype(BF16)
    kv_ref[0] = _dot(m, w_ref[...]).astype(BF16)


def _mem_kv(mem, g, w_xkv):
    b, m, d = mem.shape
    return pl.pallas_call(
        _memkv_kernel,
        name="mem_kv",
        grid=(b,),
        in_specs=[pl.BlockSpec((1, m, d), lambda i: (i, 0, 0)), pl.BlockSpec((1, d), lambda i: (0, 0)),
                  pl.BlockSpec((d, 2 * XATTN_WIDTH), lambda i: (0, 0))],
        out_specs=pl.BlockSpec((1, m, 2 * XATTN_WIDTH), lambda i: (i, 0, 0)),
        out_shape=jax.ShapeDtypeStruct((b, m, 2 * XATTN_WIDTH), BF16),
        compiler_params=pltpu.CompilerParams(vmem_limit_bytes=VMEM_LIMIT),
    )(mem, g[None, :], w_xkv)


def _retention_tables():
    h = np.arange(RET_HEADS, dtype=np.float64)
    log_gamma = np.log(1.0 - 2.0 ** (-5.0 - h))
    idx = np.arange(RET_CHUNK, dtype=np.float64)
    diff = idx[:, None] - idx[None, :]
    decay = np.where(diff[None] >= 0.0, np.exp(np.maximum(diff, 0.0)[None] * log_gamma[:, None, None]), 0.0)
    zeta = np.exp((RET_CHUNK - 1.0 - idx)[None, :] * log_gamma[:, None])
    xi = np.exp((idx + 1.0)[None, :] * log_gamma[:, None])
    chunk_decay = np.exp(RET_CHUNK * log_gamma)
    bc = lambda t: np.broadcast_to(t[:, :, None], (RET_HEADS, RET_CHUNK, HEAD_DIM))
    return (jnp.asarray(decay, F32), jnp.asarray(bc(zeta), F32), jnp.asarray(bc(xi), F32),
            tuple(float(c) for c in chunk_decay))


def _gather_rows(idx_ref, tile, n, src_hbm, dst_ref, slot, sem):
    base = tile * n
    for r in range(n):
        pltpu.make_async_copy(src_hbm.at[pl.ds(idx_ref[base + r], 1)], dst_ref.at[slot, pl.ds(r, 1)],
                              sem.at[slot]).start()


def _wait_gathered(n, src_hbm, dst_ref, slot, sem):
    pltpu.make_async_copy(src_hbm.at[pl.ds(0, n)], dst_ref.at[slot], sem.at[slot]).wait()


def _mixer_kernel(chunk_decay, add_moe, *refs):
    if add_moe:
        pos_ref, refs = refs[0], refs[1:]
    (x_ref, cos_ref, sin_ref, g_ref, win_ref, bg_ref, rg_ref, wro_ref, cw_ref, cb_ref, lng_ref, lnb_ref,
     wco_ref, wmo_ref, decay_ref, zeta_ref, xi_ref) = refs[:17]
    refs = refs[17:]
    if add_moe:
        y_hbm, refs = refs[0], refs[1:]
    o_ref, hs_hbm, state_ref, zext_ref, phase_ref, u_ref, zeros_ref, zsem = refs[:8]
    ts = x_ref.shape[1]
    d = x_ref.shape[2]
    lin = pl.program_id(0) * pl.num_programs(1) + pl.program_id(1)
    total = pl.num_programs(0) * pl.num_programs(1)
    slot = lin % 2

    @pl.when(pl.program_id(1) == 0)
    def _():
        state_ref[...] = jnp.zeros_like(state_ref)
        zext_ref[0:CONV_HALO, :] = jnp.zeros((CONV_HALO, CONV_CH), F32)

    chunk = zeros_ref.shape[0]

    def zero_chunk(k):
        start = jnp.minimum(k * chunk, hs_hbm.shape[0] - chunk)
        return pltpu.make_async_copy(zeros_ref, hs_hbm.at[pl.ds(start, chunk)], zsem.at[0])

    @pl.when(lin == 0)
    def _():
        zeros_ref[...] = jnp.zeros_like(zeros_ref)

    @pl.when(lin >= 1)
    def _():
        zero_chunk(lin - 1).wait()

    zero_chunk(lin).start()

    x = x_ref[0]
    if add_moe:
        rows_ref, sem = refs[8:10]

        @pl.when(lin == 0)
        def _():
            _gather_rows(pos_ref, 0, ts, y_hbm, rows_ref, 0, sem)

        _wait_gathered(ts, y_hbm, rows_ref, slot, sem)
        x = x + rows_ref[slot]
        _gather_rows(pos_ref, jnp.minimum(lin + 1, total - 1), ts, y_hbm, rows_ref, 1 - slot, sem)
    h = _rmsnorm(x, g_ref[...]).astype(BF16)

    def proj(a, b):
        return _dot(h, win_ref[:, a:b])

    o_q, o_k, o_v, o_g = RET_WIDTH, 2 * RET_WIDTH, 3 * RET_WIDTH, 4 * RET_WIDTH
    o_c = o_g + 2 * CONV_CH

    ca, cg = proj(o_g, o_g + CONV_CH), proj(o_g + CONV_CH, o_c)
    zext_ref[CONV_HALO:CONV_HALO + ts, :] = ca * _sigmoid(cg)
    first = CONV_HALO - (CONV_WIDTH - 1)
    other_cols = [(c0, c0 + 256) for c0 in range(0, o_g, 256)] + [(c0, c0 + 256) for c0 in range(o_c, o_c + 2 * d, 256)]
    per_phase = len(other_cols) // 8
    acc = None
    for b in range(8):
        for k in range(b * per_phase, (b + 1) * per_phase):
            u_ref[:, 256 * k:256 * (k + 1)] = proj(*other_cols[k])
        rows = ts if b == 0 else ts + 8
        part = None
        for a in range(CONV_HALO // 8 + 1):
            w = 8 * a + b - first
            if 0 <= w < CONV_WIDTH:
                term = zext_ref[8 * a:8 * a + rows, :] * cw_ref[w:w + 1, :]
                part = term if part is None else part + term
        if b == 0:
            acc = part
        else:
            phase_ref[b - 1] = part
            acc = acc + phase_ref[b - 1, b:b + ts, :]
    acc = acc + cb_ref[...]
    zext_ref[0:CONV_HALO, :] = zext_ref[ts:ts + CONV_HALO, :]
    mu = jnp.mean(acc, axis=-1, keepdims=True)
    cen = acc - mu
    var = jnp.mean(cen * cen, axis=-1, keepdims=True)
    zf = cen * lax.rsqrt(var + EPS) * lng_ref[...] + lnb_ref[...]
    y_conv = _dot((zf * _sigmoid(zf)).astype(BF16), wco_ref[...])
    uq, uk, uv, sg = (u_ref[:, c0:c0 + RET_WIDTH] for c0 in (0, o_q, o_k, o_v))
    gate_pre = u_ref[:, o_g:o_g + 2 * d]

    cos, sin = cos_ref[...], sin_ref[...]
    heads = []
    for hd in range(RET_HEADS):
        sl = slice(hd * HEAD_DIM, (hd + 1) * HEAD_DIM)
        qh = uq[:, sl]
        qh = qh * cos + pltpu.roll(qh, HEAD_DIM // 2, 1) * sin
        kh = uk[:, sl]
        kh = (kh * cos + pltpu.roll(kh, HEAD_DIM // 2, 1) * sin) * QK_SCALE
        vh = uv[:, sl]
        rows = []
        for c in range(ts // RET_CHUNK):
            r = slice(c * RET_CHUNK, (c + 1) * RET_CHUNK)
            qc = qh[r].astype(BF16)
            kc = kh[r]
            vc = vh[r].astype(BF16)
            sc = _dot_nt(qc, kc.astype(BF16)) * decay_ref[hd]
            inner = _dot(sc.astype(BF16), vc)
            st = state_ref[hd]
            cross = _dot(qc, st.astype(BF16)) * xi_ref[hd]
            kz = (kc * zeta_ref[hd]).astype(BF16)
            state_ref[hd] = chunk_decay[hd] * st + _dot_tn(kz, vc)
            rows.append(inner + cross)
        rh = jnp.concatenate(rows, axis=0)
        heads.append(rh * lax.rsqrt(jnp.mean(rh * rh, axis=-1, keepdims=True) + EPS))
    r = jnp.concatenate(heads, axis=1) * rg_ref[...]
    r = (sg * _sigmoid(sg)) * r
    y_ret = _dot(r.astype(BF16), wro_ref[...])

    gate = _sigmoid(gate_pre + bg_ref[...])
    merged = (gate[:, 0:d] * y_ret + gate[:, d:2 * d] * y_conv).astype(BF16)
    o_ref[0] = x + _dot(merged, wmo_ref[...])

    @pl.when(lin == total - 1)
    def _():
        zero_chunk(lin).wait()
        if add_moe:
            _wait_gathered(ts, y_hbm, rows_ref, 1 - slot, sem)


def _mixer(x, cos, sin, g, w_in, b_gate, ret_g, w_ret_out, conv_w, conv_b, ln_g, ln_b, w_conv_out, w_mix_out,
           n_rows, moe=None):
    b, s, d = x.shape
    ts = MIXER_TILE
    steps = b * (s // ts)
    per_step = -(-n_rows // steps)
    zero_rows = min(n_rows, -(-per_step // 8) * 8)
    decay, zeta, xi, chunk_decay = _retention_tables()
    in_cols = w_in.shape[1]
    cw = jnp.zeros((CONV_HALO, CONV_CH), F32).at[:CONV_WIDTH].set(conv_w)
    row = lambda v: v[None, :]
    tab = (RET_HEADS, RET_CHUNK, HEAD_DIM)
    add_moe = moe is not None
    in_specs = [
        pl.BlockSpec((1, ts, d), lambda i, j, *_: (i, j, 0)),
        pl.BlockSpec((ts, HEAD_DIM), lambda i, j, *_: (j, 0)),
        pl.BlockSpec((ts, HEAD_DIM), lambda i, j, *_: (j, 0)),
        _const_spec((1, d)), _const_spec((d, in_cols)), _const_spec((1, 2 * d)),
        _const_spec((1, RET_WIDTH)), _const_spec((RET_WIDTH, d)),
        _const_spec((CONV_HALO, CONV_CH)), _const_spec((1, CONV_CH)), _const_spec((1, CONV_CH)),
        _const_spec((1, CONV_CH)), _const_spec((CONV_CH, d)), _const_spec((d, d)),
        _const_spec((RET_HEADS, RET_CHUNK, RET_CHUNK)), _const_spec(tab), _const_spec(tab),
    ]
    scratch = [pltpu.VMEM((RET_HEADS, HEAD_DIM, HEAD_DIM), F32),
               pltpu.VMEM((CONV_HALO + ts, CONV_CH), F32),
               pltpu.VMEM((7, ts + 8, CONV_CH), F32),
               pltpu.VMEM((ts, in_cols - 2 * CONV_CH), F32),
               pltpu.VMEM((zero_rows, d + META_LANES), F32), pltpu.SemaphoreType.DMA((1,))]
    args = [x, cos, sin, row(g), w_in.astype(BF16), row(b_gate), row(ret_g), w_ret_out.astype(BF16), cw,
            row(conv_b), row(ln_g), row(ln_b), w_conv_out.astype(BF16), w_mix_out.astype(BF16), decay, zeta, xi]
    if add_moe:
        y_sorted, pos = moe
        in_specs.append(pl.BlockSpec(memory_space=pl.ANY))
        scratch += [pltpu.VMEM((2, ts, d), F32), pltpu.SemaphoreType.DMA((2,))]
        args = [pos] + args + [y_sorted]
    grid_spec = pltpu.PrefetchScalarGridSpec(
        num_scalar_prefetch=1 if add_moe else 0,
        grid=(b, s // ts),
        in_specs=in_specs,
        out_specs=[pl.BlockSpec((1, ts, d), lambda i, j, *_: (i, j, 0)), pl.BlockSpec(memory_space=pl.ANY)],
        scratch_shapes=scratch,
    )
    return pl.pallas_call(
        functools.partial(_mixer_kernel, chunk_decay, add_moe),
        name="mixer",
        grid_spec=grid_spec,
        out_shape=[jax.ShapeDtypeStruct((b, s, d), F32), jax.ShapeDtypeStruct((n_rows, d + META_LANES), F32)],
        compiler_params=pltpu.CompilerParams(dimension_semantics=("arbitrary", "arbitrary"),
                                             vmem_limit_bytes=VMEM_LIMIT),
    )(*args)


def _route(logits, bias):
    scores = _sigmoid(logits)
    sel = scores + bias
    one, zero = jnp.float32(1.0), jnp.float32(0.0)
    top, gscore = [], []
    for g in range(N_GROUPS):
        a = [sel[EXPERTS_PER_GROUP * g + i:EXPERTS_PER_GROUP * g + i + 1, :] for i in range(EXPERTS_PER_GROUP)]
        tg, sg = [], None
        for i in range(EXPERTS_PER_GROUP):
            rank = None
            for j in range(EXPERTS_PER_GROUP):
                if j == i:
                    continue
                ahead = (a[j] >= a[i]) if j < i else (a[j] > a[i])
                ahead = jnp.where(ahead, one, zero)
                rank = ahead if rank is None else rank + ahead
            in_top = rank < 2.0
            tg.append(in_top)
            contrib = jnp.where(in_top, a[i], zero)
            sg = contrib if sg is None else sg + contrib
        top.append(tg)
        gscore.append(sg)
    cls = jnp.zeros_like(gscore[0])
    w_lo = jnp.zeros_like(cls)
    w_hi = jnp.zeros_like(cls)
    for g in range(N_GROUPS):
        behind = None
        for g2 in range(N_GROUPS):
            if g2 == g:
                continue
            ahead = (gscore[g2] >= gscore[g]) if g2 < g else (gscore[g2] > gscore[g])
            behind = ahead if behind is None else (behind | ahead)
        best = jnp.logical_not(behind)
        for p, (i, j) in enumerate(PAIRS):
            active = best & top[g][i] & top[g][j]
            e_lo, e_hi = EXPERTS_PER_GROUP * g + i, EXPERTS_PER_GROUP * g + j
            cls = jnp.where(active, jnp.float32(len(PAIRS) * g + p), cls)
            w_lo = jnp.where(active, scores[e_lo:e_lo + 1, :], w_lo)
            w_hi = jnp.where(active, scores[e_hi:e_hi + 1, :], w_hi)
    den = w_lo + w_hi
    return cls, w_lo / den, w_hi / den


def _xattn_route_kernel(x_ref, kv_ref, gx_ref, wq_ref, wo_ref, gf_ref, wrh_ref, wrl_ref, br_ref,
                        init_hbm, o_ref, meta_ref, alloc_ref, hs_hbm,
                        fill_ref, cur_ref, next_ref, rows_ref, posv_ref, pos_smem, sem, sem_pos):
    del init_hbm
    ts = x_ref.shape[1]
    d = x_ref.shape[2]
    tm = float(MOE_TILE)
    lin = pl.program_id(0) * pl.num_programs(1) + pl.program_id(1)
    total = pl.num_programs(0) * pl.num_programs(1)
    slot = lin % 2

    def wait_rows(sl):
        pltpu.make_async_copy(rows_ref.at[sl], hs_hbm.at[pl.ds(0, ts)], sem.at[sl]).wait()

    @pl.when(lin == 0)
    def _():
        fill_ref[...] = jnp.full(fill_ref.shape, tm, F32)
        cur_ref[...] = jnp.zeros_like(cur_ref)
        next_ref[...] = jnp.zeros_like(next_ref)
        rows_ref[1] = jnp.zeros(rows_ref.shape[1:], F32)
        for r in range(ts):
            pos_smem[0, r] = hs_hbm.shape[0] - ts + r

    def pos_to_smem():
        return pltpu.make_async_copy(posv_ref.at[pl.ds(0, 1)], pos_smem, sem_pos.at[0])

    def scatter_rows(sl):
        for r in range(ts):
            pltpu.make_async_copy(rows_ref.at[sl, pl.ds(r, 1)], hs_hbm.at[pl.ds(pos_smem[0, r], 1)],
                                  sem.at[sl]).start()

    @pl.when(lin >= 1)
    def _():
        pos_to_smem().wait()

    scatter_rows(1 - slot)

    x = x_ref[0]
    h = _rmsnorm(x, gx_ref[...]).astype(BF16)
    q = _dot(h, wq_ref[...])
    kv = kv_ref[0]
    heads = []
    for hd in range(XATTN_HEADS):
        sl = slice(hd * HEAD_DIM, (hd + 1) * HEAD_DIM)
        vs = slice(XATTN_WIDTH + hd * HEAD_DIM, XATTN_WIDTH + (hd + 1) * HEAD_DIM)
        sc = _dot_nt(q[:, sl].astype(BF16), kv[:, sl]) * QK_SCALE
        sc = sc - jnp.max(sc, axis=-1, keepdims=True)
        p = jnp.exp(sc)
        p = p / jnp.sum(p, axis=-1, keepdims=True)
        heads.append(_dot(p.astype(BF16), kv[:, vs]))
    att = jnp.concatenate(heads, axis=1).astype(BF16)
    x2 = x + _dot(att, wo_ref[...])
    o_ref[0] = x2

    hf = _rmsnorm(x2, gf_ref[...])
    h_hi = hf.astype(BF16)
    h_lo = (hf - h_hi.astype(F32)).astype(BF16)
    lg = _dot(h_hi, wrh_ref[...]) + (_dot(h_lo, wrh_ref[...]) + _dot(h_hi, wrl_ref[...]))
    logits = lg.T[0:N_EXPERTS, :]
    cls, w_lo, w_hi = _route(logits, br_ref[...])

    one, zero = jnp.float32(1.0), jnp.float32(0.0)
    crow = lax.broadcasted_iota(jnp.int32, (CLASS_ROWS, ts), 0).astype(F32)
    onehot = jnp.where(crow == cls, one, zero)
    before = lax.broadcasted_iota(jnp.int32, (ts, ts), 0) < lax.broadcasted_iota(jnp.int32, (ts, ts), 1)
    prefix = _dot(onehot.astype(BF16), jnp.where(before, one, zero).astype(BF16))
    count = jnp.sum(onehot, axis=1, keepdims=True)
    fill, cur, nxt = fill_ref[...], cur_ref[...], next_ref[...]
    need = jnp.where(fill + count > tm, one, zero)
    lower = (lax.broadcasted_iota(jnp.int32, (CLASS_ROWS, CLASS_ROWS), 1)
             < lax.broadcasted_iota(jnp.int32, (CLASS_ROWS, CLASS_ROWS), 0))
    opened_before = _dot(jnp.where(lower, one, zero).astype(BF16),
                         jnp.broadcast_to(need, (CLASS_ROWS, META_LANES)).astype(BF16))[:, 0:1]
    new_tile = nxt + opened_before
    per_token = lambda v: jnp.sum(onehot * v, axis=0, keepdims=True)
    slot_in_cur = per_token(fill) + per_token(prefix)
    pos = jnp.where(slot_in_cur >= tm, per_token(new_tile) * tm + (slot_in_cur - tm),
                    per_token(cur) * tm + slot_in_cur)
    fill_ref[...] = jnp.where(need > zero, fill + count - tm, fill + count)
    cur_ref[...] = jnp.where(need > zero, new_tile, cur)
    next_ref[...] = nxt + jnp.sum(need, axis=0, keepdims=True)
    lane = lax.broadcasted_iota(jnp.int32, (CLASS_ROWS, META_LANES), 1)
    alloc_ref[...] = jnp.where(lane == 0, need, jnp.where(lane == 1, new_tile, zero))

    meta = jnp.concatenate([cls, w_lo, w_hi, pos, jnp.zeros((META_LANES - 4, ts), F32)], axis=0)
    meta_ref[...] = meta[0:8, :]

    @pl.when(lin >= 1)
    def _():
        wait_rows(slot)

    rows_ref[slot, :, 0:d] = hf
    rows_ref[slot, :, d:d + META_LANES] = meta.T
    posv_ref[...] = jnp.broadcast_to(pos.astype(jnp.int32), posv_ref.shape)
    pos_to_smem().start()

    @pl.when(lin == total - 1)
    def _():
        pos_to_smem().wait()
        scatter_rows(slot)
        wait_rows(slot)
        wait_rows(1 - slot)


def _xattn_route(x, kv, g_x, w_xq, w_xo, g_f, w_router, b_router, hs_zero):
    n_rows = hs_zero.shape[0]
    b, s, d = x.shape
    ts = SEQ_TILE
    assert ts <= MOE_TILE
    m = kv.shape[1]
    nt = s // ts
    w = d + META_LANES
    row = lambda v: v[None, :]
    wr = jnp.zeros((d, META_LANES), F32).at[:, :N_EXPERTS].set(w_router)
    wr_hi = wr.astype(BF16)
    wr_lo = (wr - wr_hi.astype(F32)).astype(BF16)
    return pl.pallas_call(
        _xattn_route_kernel,
        name="xattn_route",
        grid=(b, nt),
        in_specs=[
            pl.BlockSpec((1, ts, d), lambda i, j: (i, j, 0)),
            pl.BlockSpec((1, m, 2 * XATTN_WIDTH), lambda i, j: (i, 0, 0)),
            _const_spec((1, d)), _const_spec((d, XATTN_WIDTH)), _const_spec((XATTN_WIDTH, d)),
            _const_spec((1, d)), _const_spec((d, META_LANES)), _const_spec((d, META_LANES)),
            _const_spec((N_EXPERTS, 1)),
            pl.BlockSpec(memory_space=pl.ANY),
        ],
        out_specs=[
            pl.BlockSpec((1, ts, d), lambda i, j: (i, j, 0)),
            pl.BlockSpec((8, ts), lambda i, j: (0, i * nt + j)),
            pl.BlockSpec((CLASS_ROWS, META_LANES), lambda i, j: (i * nt + j, 0)),
            pl.BlockSpec(memory_space=pl.ANY),
        ],
        out_shape=[
            jax.ShapeDtypeStruct((b, s, d), F32),
            jax.ShapeDtypeStruct((8, b * s), F32),
            jax.ShapeDtypeStruct((b * nt * CLASS_ROWS, META_LANES), F32),
            jax.ShapeDtypeStruct((n_rows, w), F32),
        ],
        scratch_shapes=[pltpu.VMEM((CLASS_ROWS, 1), F32), pltpu.VMEM((CLASS_ROWS, 1), F32),
                        pltpu.VMEM((1, 1), F32), pltpu.VMEM((2, ts, w), F32),
                        pltpu.VMEM((8, ts), jnp.int32), pltpu.SMEM((1, ts), jnp.int32),
                        pltpu.SemaphoreType.DMA((2,)), pltpu.SemaphoreType.DMA((1,))],
        input_output_aliases={9: 3},
        compiler_params=pltpu.CompilerParams(dimension_semantics=("arbitrary", "arbitrary"),
                                             vmem_limit_bytes=VMEM_LIMIT),
    )(x, kv, row(g_x), w_xq.astype(BF16), w_xo.astype(BF16), row(g_f), wr_hi, wr_lo, b_router[:, None], hs_zero)


def _tile_tables(alloc, n_tiles):
    opened = alloc[:, 0] > 0.0
    tile_of = alloc[:, 1].astype(jnp.int32)
    cls_of = jnp.arange(alloc.shape[0], dtype=jnp.int32) % CLASS_ROWS
    tile_ids = jnp.arange(n_tiles, dtype=jnp.int32)
    hit = opened[None, :] & (tile_of[None, :] == tile_ids[:, None])
    tile_cls = jnp.sum(jnp.where(hit, cls_of[None, :], 0), axis=1)
    used = jnp.any(hit, axis=1)
    n_used = jnp.sum(used.astype(jnp.int32))
    visit = jnp.asarray(PAIR_VISIT_RANK, jnp.int32)
    key = jnp.where(used, len(PAIRS) * (tile_cls // len(PAIRS)) + visit[tile_cls % len(PAIRS)], N_CLASSES)
    ahead = (key[None, :] < key[:, None]) | ((key[None, :] == key[:, None]) & (tile_ids[None, :] < tile_ids[:, None]))
    rank = jnp.sum(ahead.astype(jnp.int32), axis=1)
    order = jnp.sum(jnp.where(rank[None, :] == tile_ids[:, None], tile_ids[None, :], 0), axis=1)
    order_in = order[jnp.minimum(tile_ids, n_used - 1)]
    step_cls = tile_cls[order_in]
    pair = step_cls % len(PAIRS)
    base = EXPERTS_PER_GROUP * (step_cls // len(PAIRS))
    slot_a = jnp.asarray(PAIR_SLOT_A, jnp.int32)[pair]
    slot_b = jnp.asarray(PAIR_SLOT_B, jnp.int32)[pair]
    i32 = lambda v: v.astype(jnp.int32)
    return (i32(order_in), i32(order), i32(base + slot_a), i32(base + slot_b), i32(slot_a > slot_b),
            i32(n_used.reshape(1)))


def _moe_kernel(oin_ref, oout_ref, ea_ref, eb_ref, swap_ref, nused_ref, rows_ref,
                wg_a, wg_b, wu_a, wu_b, wd_a, wd_b, y_ref):
    tm, d = y_ref.shape
    i = pl.program_id(0)
    used = i < nused_ref[0]

    @pl.when(used)
    def _():
        hb = rows_ref[:, 0:d].astype(BF16)
        w_lower, w_higher = rows_ref[:, d + 1:d + 2], rows_ref[:, d + 2:d + 3]
        a_is_higher = swap_ref[i] == 1
        w_a = jnp.where(a_is_higher, w_higher, w_lower)
        w_b = jnp.where(a_is_higher, w_lower, w_higher)
        y = None
        for w_slot, wg, wu, wd in ((w_a, wg_a, wu_a, wd_a), (w_b, wg_b, wu_b, wd_b)):
            gate = _dot(hb, wg[0].astype(BF16))
            act = (gate * _sigmoid(gate)) * _dot(hb, wu[0].astype(BF16))
            part = w_slot * _dot(act.astype(BF16), wd[0].astype(BF16))
            y = part if y is None else y + part
        y_ref[...] = y

    @pl.when(jnp.logical_not(used))
    def _():
        y_ref[...] = jnp.zeros_like(y_ref)


def _moe(hs, order_in, order_out, e_a, e_b, swap, n_used, w_gate, w_up, w_down, n_tiles, tm):
    w = hs.shape[1]
    d = w - META_LANES
    f = w_gate.shape[2]
    a3 = lambda i, oin, oout, ea, eb, sw, nu: (ea[i], 0, 0)
    b3 = lambda i, oin, oout, ea, eb, sw, nu: (eb[i], 0, 0)
    grid_spec = pltpu.PrefetchScalarGridSpec(
        num_scalar_prefetch=6,
        grid=(n_tiles,),
        in_specs=[
            pl.BlockSpec((tm, w), lambda i, oin, oout, ea, eb, sw, nu: (oin[i], 0)),
            pl.BlockSpec((1, d, f), a3), pl.BlockSpec((1, d, f), b3),
            pl.BlockSpec((1, d, f), a3), pl.BlockSpec((1, d, f), b3),
            pl.BlockSpec((1, f, d), a3), pl.BlockSpec((1, f, d), b3),
        ],
        out_specs=pl.BlockSpec((tm, d), lambda i, oin, oout, ea, eb, sw, nu: (oout[i], 0)),
    )
    return pl.pallas_call(
        _moe_kernel,
        name="moe_experts",
        grid_spec=grid_spec,
        out_shape=jax.ShapeDtypeStruct((n_tiles * tm, d), F32),
        compiler_params=pltpu.CompilerParams(dimension_semantics=("arbitrary",), vmem_limit_bytes=VMEM_LIMIT),
    )(order_in, order_out, e_a, e_b, swap, n_used, hs, w_gate, w_gate, w_up, w_up, w_down, w_down)


def _combine_kernel(final_norm, pos_ref, x_ref, y_hbm, g_ref, o_ref, rows_ref, sem):
    tc = x_ref.shape[0]
    i = pl.program_id(0)
    n = pl.num_programs(0)

    def issue(tile, slot):
        base = tile * tc
        for r in range(tc):
            pltpu.make_async_copy(y_hbm.at[pl.ds(pos_ref[base + r], 1)], rows_ref.at[slot, pl.ds(r, 1)],
                                  sem.at[slot]).start()

    @pl.when(i == 0)
    def _():
        issue(0, 0)

    for slot in range(2):
        @pl.when(i % 2 == slot)
        def _():
            @pl.when(i + 1 < n)
            def _():
                issue(i + 1, 1 - slot)

            pltpu.make_async_copy(y_hbm.at[pl.ds(0, tc)], rows_ref.at[slot], sem.at[slot]).wait()
            x = x_ref[...] + rows_ref[slot]
            if final_norm:
                x = _rmsnorm(x, g_ref[...])
            o_ref[...] = x


def _combine(x, y_sorted, pos, g, final_norm):
    t, d = x.shape
    tc = COMBINE_TILE
    grid_spec = pltpu.PrefetchScalarGridSpec(
        num_scalar_prefetch=1,
        grid=(t // tc,),
        in_specs=[pl.BlockSpec((tc, d), lambda i, p: (i, 0)), pl.BlockSpec(memory_space=pl.ANY),
                  pl.BlockSpec((1, d), lambda i, p: (0, 0))],
        out_specs=pl.BlockSpec((tc, d), lambda i, p: (i, 0)),
        scratch_shapes=[pltpu.VMEM((2, tc, d), F32), pltpu.SemaphoreType.DMA((2,))],
    )
    return pl.pallas_call(
        functools.partial(_combine_kernel, final_norm),
        name="combine",
        grid_spec=grid_spec,
        out_shape=jax.ShapeDtypeStruct((t, d), F32),
        compiler_params=pltpu.CompilerParams(dimension_semantics=("arbitrary",), vmem_limit_bytes=VMEM_LIMIT),
    )(pos, x, y_sorted, g[None, :])


def kernel(x, mem, positions, norm_mix_g, w_in, b_branch_gate, ret_norm_g, w_ret_out, conv_w, conv_b, conv_ln_g, conv_ln_b, w_conv_out, w_mix_out, norm_xattn_g, norm_mem_g, w_xq, w_xkv, w_xo, norm_ffn_g, w_router, b_router, w_exp_gate, w_exp_up, w_exp_down, norm_final_g):
    b, s, d = x.shape
    depth = w_in.shape[0]
    t = b * s
    assert s % SEQ_TILE == 0 and s % MIXER_TILE == 0 and MIXER_TILE % RET_CHUNK == 0
    assert t % MOE_TILE == 0 and t % COMBINE_TILE == 0
    tm = MOE_TILE
    n_tiles = t // tm + N_CLASSES
    cos, sin = _rope_tables(positions)
    n_exp, _, f = w_exp_gate.shape[1:]
    wg_all = w_exp_gate.reshape(depth * n_exp, d, f)
    wu_all = w_exp_up.reshape(depth * n_exp, d, f)
    wd_all = w_exp_down.reshape(depth * n_exp, f, d)
    moe = None
    for l in range(depth):
        x, hs_zero = _mixer(x, cos, sin, norm_mix_g[l], w_in[l], b_branch_gate[l], ret_norm_g[l], w_ret_out[l],
                            conv_w[l], conv_b[l], conv_ln_g[l], conv_ln_b[l], w_conv_out[l], w_mix_out[l],
                            (n_tiles + 1) * tm, moe)
        kv = _mem_kv(mem, norm_mem_g[l], w_xkv[l].astype(BF16))
        x, meta, alloc, hs = _xattn_route(x, kv, norm_xattn_g[l], w_xq[l], w_xo[l], norm_ffn_g[l],
                                          w_router, b_router, hs_zero)
        pos = meta[3].astype(jnp.int32)
        order_in, order_out, e_a, e_b, swap, n_used = _tile_tables(alloc, n_tiles)
        y_sorted = _moe(hs, order_in, order_out, e_a + l * n_exp, e_b + l * n_exp, swap, n_used,
                        wg_all, wu_all, wd_all, n_tiles, tm)
        moe = (y_sorted, pos)
    return _combine(x.reshape(t, d), y_sorted, pos, norm_final_g, True).reshape(b, s, d)
```

```python
import functools

import jax
import jax.numpy as jnp
import numpy as np
from jax import lax
from jax.experimental import pallas as pl
from jax.experimental.pallas import tpu as pltpu

F32 = jnp.float32
BF16 = jnp.bfloat16

RET_HEADS = 4
HEAD_DIM = 128
RET_WIDTH = RET_HEADS * HEAD_DIM
RET_CHUNK = 128
CONV_CH = 512
CONV_WIDTH = 31
CONV_HALO = 32
XATTN_HEADS = 4
XATTN_WIDTH = XATTN_HEADS * HEAD_DIM
N_EXPERTS = 16
N_GROUPS = 4
EXPERTS_PER_GROUP = 4
ROPE_BASE = 10000.0
EPS = 1e-6
QK_SCALE = HEAD_DIM ** -0.5

PAIRS = ((0, 1), (0, 2), (0, 3), (1, 2), (1, 3), (2, 3))
N_CLASSES = N_GROUPS * len(PAIRS)
PAIR_VISIT_RANK = (0, 3, 2, 5, 1, 4)
PAIR_SLOT_A = (0, 2, 3, 2, 3, 2)
PAIR_SLOT_B = (1, 0, 0, 1, 1, 3)
CLASS_ROWS = 32
META_LANES = 128

SEQ_TILE = 256
MIXER_TILE = 256
MOE_TILE = 256
COMBINE_TILE = 512
VMEM_LIMIT = 56 * 1024 * 1024


def _rmsnorm(x, g):
    return x * lax.rsqrt(jnp.mean(x * x, axis=-1, keepdims=True) + EPS) * g


def _sigmoid(x):
    return 1.0 / (1.0 + jnp.exp(-x))


def _dot(a, b):
    return jnp.dot(a, b, preferred_element_type=F32)


def _dot_nt(a, b):
    return lax.dot_general(a, b, (((1,), (1,)), ((), ())), preferred_element_type=F32)


def _dot_tn(a, b):
    return lax.dot_general(a, b, (((0,), (0,)), ((), ())), preferred_element_type=F32)


def _const_spec(shape):
    n = len(shape)
    return pl.BlockSpec(shape, lambda *_: (0,) * n, pipeline_mode=pl.Buffered(1))


def _rope_kernel(pos_ref, invf_ref, cos_ref, sin_ref):
    ang = pos_ref[...] * invf_ref[...]
    lane = lax.broadcasted_iota(jnp.int32, ang.shape, 1)
    s = jnp.sin(ang)
    cos_ref[...] = jnp.cos(ang)
    sin_ref[...] = jnp.where(lane < HEAD_DIM // 2, -s, s)


def _rope_tables(positions):
    s = positions.shape[0]
    half = HEAD_DIM // 2
    inv_freq = ROPE_BASE ** (-jnp.arange(half, dtype=F32) / half)
    invf = jnp.concatenate([inv_freq, inv_freq])[None, :]
    pos = positions.astype(F32)[:, None]
    ts = min(s, 1024)
    return pl.pallas_call(
        _rope_kernel,
        name="rope_tables",
        grid=(s // ts,),
        in_specs=[pl.BlockSpec((ts, 1), lambda i: (i, 0)), pl.BlockSpec((1, HEAD_DIM), lambda i: (0, 0))],
        out_specs=[pl.BlockSpec((ts, HEAD_DIM), lambda i: (i, 0))] * 2,
        out_shape=[jax.ShapeDtypeStruct((s, HEAD_DIM), F32)] * 2,
    )(pos, invf)


def _memkv_kernel(mem_ref, g_ref, w_ref, kv_ref):
    m = _rmsnorm(mem_ref[0], g_ref[...]).astype(BF16)
    kv_ref[0] = _dot(m, w_ref[...]).astype(BF16)


def _mem_kv(mem, g, w_xkv):
    b, m, d = mem.shape
    return pl.pallas_call(
        _memkv_kernel,
        name="mem_kv",
        grid=(b,),
        in_specs=[pl.BlockSpec((1, m, d), lambda i: (i, 0, 0)), pl.BlockSpec((1, d), lambda i: (0, 0)),
                  pl.BlockSpec((d, 2 * XATTN_WIDTH), lambda i: (0, 0))],
        out_specs=pl.BlockSpec((1, m, 2 * XATTN_WIDTH), lambda i: (i, 0, 0)),
        out_shape=jax.ShapeDtypeStruct((b, m, 2 * XATTN_WIDTH), BF16),
        compiler_params=pltpu.CompilerParams(vmem_limit_bytes=VMEM_LIMIT),
    )(mem, g[None, :], w_xkv)


def _retention_tables():
    h = np.arange(RET_HEADS, dtype=np.float64)
    log_gamma = np.log(1.0 - 2.0 ** (-5.0 - h))
    idx = np.arange(RET_CHUNK, dtype=np.float64)
    diff = idx[:, None] - idx[None, :]
    decay = np.where(diff[None] >= 0.0, np.exp(np.maximum(diff, 0.0)[None] * log_gamma[:, None, None]), 0.0)
    zeta = np.exp((RET_CHUNK - 1.0 - idx)[None, :] * log_gamma[:, None])
    xi = np.exp((idx + 1.0)[None, :] * log_gamma[:, None])
    chunk_decay = np.exp(RET_CHUNK * log_gamma)
    bc = lambda t: np.broadcast_to(t[:, :, None], (RET_HEADS, RET_CHUNK, HEAD_DIM))
    return (jnp.asarray(decay, F32), jnp.asarray(bc(zeta), F32), jnp.asarray(bc(xi), F32),
            tuple(float(c) for c in chunk_decay))


def _gather_rows(idx_ref, tile, n, src_hbm, dst_ref, slot, sem):
    base = tile * n
    for r in range(n):
        pltpu.make_async_copy(src_hbm.at[pl.ds(idx_ref[base + r], 1)], dst_ref.at[slot, pl.ds(r, 1)],
                              sem.at[slot]).start()


def _wait_gathered(n, src_hbm, dst_ref, slot, sem):
    pltpu.make_async_copy(src_hbm.at[pl.ds(0, n)], dst_ref.at[slot], sem.at[slot]).wait()


def _mixer_kernel(chunk_decay, add_moe, *refs):
    if add_moe:
        pos_ref, refs = refs[0], refs[1:]
    (x_ref, cos_ref, sin_ref, g_ref, win_ref, bg_ref, rg_ref, wro_ref, cw_ref, cb_ref, lng_ref, lnb_ref,
     wco_ref, wmo_ref, decay_ref, zeta_ref, xi_ref) = refs[:17]
    refs = refs[17:]
    if add_moe:
        y_hbm, refs = refs[0], refs[1:]
    o_ref, hs_hbm, state_ref, zext_ref, phase_ref, u_ref, zeros_ref, zsem = refs[:8]
    ts = x_ref.shape[1]
    d = x_ref.shape[2]
    lin = pl.program_id(0) * pl.num_programs(1) + pl.program_id(1)
    total = pl.num_programs(0) * pl.num_programs(1)
    slot = lin % 2

    @pl.when(pl.program_id(1) == 0)
    def _():
        state_ref[...] = jnp.zeros_like(state_ref)
        zext_ref[0:CONV_HALO, :] = jnp.zeros((CONV_HALO, CONV_CH), F32)

    chunk = zeros_ref.shape[0]

    def zero_chunk(k):
        start = jnp.minimum(k * chunk, hs_hbm.shape[0] - chunk)
        return pltpu.make_async_copy(zeros_ref, hs_hbm.at[pl.ds(start, chunk)], zsem.at[0])

    @pl.when(lin == 0)
    def _():
        zeros_ref[...] = jnp.zeros_like(zeros_ref)

    @pl.when(lin >= 1)
    def _():
        zero_chunk(lin - 1).wait()

    zero_chunk(lin).start()

    x = x_ref[0]
    if add_moe:
        rows_ref, sem = refs[8:10]

        @pl.when(lin == 0)
        def _():
            _gather_rows(pos_ref, 0, ts, y_hbm, rows_ref, 0, sem)

        _wait_gathered(ts, y_hbm, rows_ref, slot, sem)
        x = x + rows_ref[slot]
        _gather_rows(pos_ref, jnp.minimum(lin + 1, total - 1), ts, y_hbm, rows_ref, 1 - slot, sem)
    h = _rmsnorm(x, g_ref[...]).astype(BF16)

    def proj(a, b):
        return _dot(h, win_ref[:, a:b])

    o_q, o_k, o_v, o_g = RET_WIDTH, 2 * RET_WIDTH, 3 * RET_WIDTH, 4 * RET_WIDTH
    o_c = o_g + 2 * CONV_CH

    ca, cg = proj(o_g, o_g + CONV_CH), proj(o_g + CONV_CH, o_c)
    zext_ref[CONV_HALO:CONV_HALO + ts, :] = ca * _sigmoid(cg)
    first = CONV_HALO - (CONV_WIDTH - 1)
    other_cols = [(c0, c0 + 256) for c0 in range(0, o_g, 256)] + [(c0, c0 + 256) for c0 in range(o_c, o_c + 2 * d, 256)]
    per_phase = len(other_cols) // 8
    acc = None
    for b in range(8):
        for k in range(b * per_phase, (b + 1) * per_phase):
            u_ref[:, 256 * k:256 * (k + 1)] = proj(*other_cols[k])
        rows = ts if b == 0 else ts + 8
        part = None
        for a in range(CONV_HALO // 8 + 1):
            w = 8 * a + b - first
            if 0 <= w < CONV_WIDTH:
                term = zext_ref[8 * a:8 * a + rows, :] * cw_ref[w:w + 1, :]
                part = term if part is None else part + term
        if b == 0:
            acc = part
        else:
            phase_ref[b - 1] = part
            acc = acc + phase_ref[b - 1, b:b + ts, :]
    acc = acc + cb_ref[...]
    zext_ref[0:CONV_HALO, :] = zext_ref[ts:ts + CONV_HALO, :]
    mu = jnp.mean(acc, axis=-1, keepdims=True)
    cen = acc - mu
    var = jnp.mean(cen * cen, axis=-1, keepdims=True)
    zf = cen * lax.rsqrt(var + EPS) * lng_ref[...] + lnb_ref[...]
    y_conv = _dot((zf * _sigmoid(zf)).astype(BF16), wco_ref[...])
    uq, uk, uv, sg = (u_ref[:, c0:c0 + RET_WIDTH] for c0 in (0, o_q, o_k, o_v))
    gate_pre = u_ref[:, o_g:o_g + 2 * d]

    cos, sin = cos_ref[...], sin_ref[...]
    heads = []
    for hd in range(RET_HEADS):
        sl = slice(hd * HEAD_DIM, (hd + 1) * HEAD_DIM)
        qh = uq[:, sl]
        qh = qh * cos + pltpu.roll(qh, HEAD_DIM // 2, 1) * sin
        kh = uk[:, sl]
        kh = (kh * cos + pltpu.roll(kh, HEAD_DIM // 2, 1) * sin) * QK_SCALE
        vh = uv[:, sl]
        rows = []
        for c in range(ts // RET_CHUNK):
            r = slice(c * RET_CHUNK, (c + 1) * RET_CHUNK)
            qc = qh[r].astype(BF16)
            kc = kh[r]
            vc = vh[r].astype(BF16)
            sc = _dot_nt(qc, kc.astype(BF16)) * decay_ref[hd]
            inner = _dot(sc.astype(BF16), vc)
            st = state_ref[hd]
            cross = _dot(qc, st.astype(BF16)) * xi_ref[hd]
            kz = (kc * zeta_ref[hd]).astype(BF16)
            state_ref[hd] = chunk_decay[hd] * st + _dot_tn(kz, vc)
            rows.append(inner + cross)
        rh = jnp.concatenate(rows, axis=0)
        heads.append(rh * lax.rsqrt(jnp.mean(rh * rh, axis=-1, keepdims=True) + EPS))
    r = jnp.concatenate(heads, axis=1) * rg_ref[...]
    r = (sg * _sigmoid(sg)) * r
    y_ret = _dot(r.astype(BF16), wro_ref[...])

    gate = _sigmoid(gate_pre + bg_ref[...])
    merged = (gate[:, 0:d] * y_ret + gate[:, d:2 * d] * y_conv).astype(BF16)
    o_ref[0] = x + _dot(merged, wmo_ref[...])

    @pl.when(lin == total - 1)
    def _():
        zero_chunk(lin).wait()
        if add_moe:
            _wait_gathered(ts, y_hbm, rows_ref, 1 - slot, sem)


def _mixer(x, cos, sin, g, w_in, b_gate, ret_g, w_ret_out, conv_w, conv_b, ln_g, ln_b, w_conv_out, w_mix_out,
           n_rows, moe=None):
    b, s, d = x.shape
    ts = MIXER_TILE
    steps = b * (s // ts)
    per_step = -(-n_rows // steps)
    zero_rows = min(n_rows, -(-per_step // 8) * 8)
    decay, zeta, xi, chunk_decay = _retention_tables()
    in_cols = w_in.shape[1]
    cw = jnp.zeros((CONV_HALO, CONV_CH), F32).at[:CONV_WIDTH].set(conv_w)
    row = lambda v: v[None, :]
    tab = (RET_HEADS, RET_CHUNK, HEAD_DIM)
    add_moe = moe is not None
    in_specs = [
        pl.BlockSpec((1, ts, d), lambda i, j, *_: (i, j, 0)),
        pl.BlockSpec((ts, HEAD_DIM), lambda i, j, *_: (j, 0)),
        pl.BlockSpec((ts, HEAD_DIM), lambda i, j, *_: (j, 0)),
        _const_spec((1, d)), _const_spec((d, in_cols)), _const_spec((1, 2 * d)),
        _const_spec((1, RET_WIDTH)), _const_spec((RET_WIDTH, d)),
        _const_spec((CONV_HALO, CONV_CH)), _const_spec((1, CONV_CH)), _const_spec((1, CONV_CH)),
        _const_spec((1, CONV_CH)), _const_spec((CONV_CH, d)), _const_spec((d, d)),
        _const_spec((RET_HEADS, RET_CHUNK, RET_CHUNK)), _const_spec(tab), _const_spec(tab),
    ]
    scratch = [pltpu.VMEM((RET_HEADS, HEAD_DIM, HEAD_DIM), F32),
               pltpu.VMEM((CONV_HALO + ts, CONV_CH), F32),
               pltpu.VMEM((7, ts + 8, CONV_CH), F32),
               pltpu.VMEM((ts, in_cols - 2 * CONV_CH), F32),
               pltpu.VMEM((zero_rows, d + META_LANES), F32), pltpu.SemaphoreType.DMA((1,))]
    args = [x, cos, sin, row(g), w_in.astype(BF16), row(b_gate), row(ret_g), w_ret_out.astype(BF16), cw,
            row(conv_b), row(ln_g), row(ln_b), w_conv_out.astype(BF16), w_mix_out.astype(BF16), decay, zeta, xi]
    if add_moe:
        y_sorted, pos = moe
        in_specs.append(pl.BlockSpec(memory_space=pl.ANY))
        scratch += [pltpu.VMEM((2, ts, d), F32), pltpu.SemaphoreType.DMA((2,))]
        args = [pos] + args + [y_sorted]
    grid_spec = pltpu.PrefetchScalarGridSpec(
        num_scalar_prefetch=1 if add_moe else 0,
        grid=(b, s // ts),
        in_specs=in_specs,
        out_specs=[pl.BlockSpec((1, ts, d), lambda i, j, *_: (i, j, 0)), pl.BlockSpec(memory_space=pl.ANY)],
        scratch_shapes=scratch,
    )
    return pl.pallas_call(
        functools.partial(_mixer_kernel, chunk_decay, add_moe),
        name="mixer",
        grid_spec=grid_spec,
        out_shape=[jax.ShapeDtypeStruct((b, s, d), F32), jax.ShapeDtypeStruct((n_rows, d + META_LANES), F32)],
        compiler_params=pltpu.CompilerParams(dimension_semantics=("arbitrary", "arbitrary"),
                                             vmem_limit_bytes=VMEM_LIMIT),
    )(*args)


def _route(logits, bias):
    scores = _sigmoid(logits)
    sel = scores + bias
    one, zero = jnp.float32(1.0), jnp.float32(0.0)
    top, gscore = [], []
    for g in range(N_GROUPS):
        a = [sel[EXPERTS_PER_GROUP * g + i:EXPERTS_PER_GROUP * g + i + 1, :] for i in range(EXPERTS_PER_GROUP)]
        tg, sg = [], None
        for i in range(EXPERTS_PER_GROUP):
            rank = None
            for j in range(EXPERTS_PER_GROUP):
                if j == i:
                    continue
                ahead = (a[j] >= a[i]) if j < i else (a[j] > a[i])
                ahead = jnp.where(ahead, one, zero)
                rank = ahead if rank is None else rank + ahead
            in_top = rank < 2.0
            tg.append(in_top)
            contrib = jnp.where(in_top, a[i], zero)
            sg = contrib if sg is None else sg + contrib
        top.append(tg)
        gscore.append(sg)
    cls = jnp.zeros_like(gscore[0])
    w_lo = jnp.zeros_like(cls)
    w_hi = jnp.zeros_like(cls)
    for g in range(N_GROUPS):
        behind = None
        for g2 in range(N_GROUPS):
            if g2 == g:
                continue
            ahead = (gscore[g2] >= gscore[g]) if g2 < g else (gscore[g2] > gscore[g])
            behind = ahead if behind is None else (behind | ahead)
        best = jnp.logical_not(behind)
        for p, (i, j) in enumerate(PAIRS):
            active = best & top[g][i] & top[g][j]
            e_lo, e_hi = EXPERTS_PER_GROUP * g + i, EXPERTS_PER_GROUP * g + j
            cls = jnp.where(active, jnp.float32(len(PAIRS) * g + p), cls)
            w_lo = jnp.where(active, scores[e_lo:e_lo + 1, :], w_lo)
            w_hi = jnp.where(active, scores[e_hi:e_hi + 1, :], w_hi)
    den = w_lo + w_hi
    return cls, w_lo / den, w_hi / den


def _xattn_route_kernel(x_ref, kv_ref, gx_ref, wq_ref, wo_ref, gf_ref, wrh_ref, wrl_ref, br_ref,
                        init_hbm, o_ref, meta_ref, alloc_ref, hs_hbm,
                        fill_ref, cur_ref, next_ref, rows_ref, posv_ref, pos_smem, sem, sem_pos):
    del init_hbm
    ts = x_ref.shape[1]
    d = x_ref.shape[2]
    tm = float(MOE_TILE)
    lin = pl.program_id(0) * pl.num_programs(1) + pl.program_id(1)
    total = pl.num_programs(0) * pl.num_programs(1)
    slot = lin % 2

    def wait_rows(sl):
        pltpu.make_async_copy(rows_ref.at[sl], hs_hbm.at[pl.ds(0, ts)], sem.at[sl]).wait()

    @pl.when(lin == 0)
    def _():
        fill_ref[...] = jnp.full(fill_ref.shape, tm, F32)
        cur_ref[...] = jnp.zeros_like(cur_ref)
        next_ref[...] = jnp.zeros_like(next_ref)
        rows_ref[1] = jnp.zeros(rows_ref.shape[1:], F32)
        for r in range(ts):
            pos_smem[0, r] = hs_hbm.shape[0] - ts + r

    def pos_to_smem():
        return pltpu.make_async_copy(posv_ref.at[pl.ds(0, 1)], pos_smem, sem_pos.at[0])

    def scatter_rows(sl):
        for r in range(ts):
            pltpu.make_async_copy(rows_ref.at[sl, pl.ds(r, 1)], hs_hbm.at[pl.ds(pos_smem[0, r], 1)],
                                  sem.at[sl]).start()

    @pl.when(lin >= 1)
    def _():
        pos_to_smem().wait()

    scatter_rows(1 - slot)

    x = x_ref[0]
    h = _rmsnorm(x, gx_ref[...]).astype(BF16)
    q = _dot(h, wq_ref[...])
    kv = kv_ref[0]
    heads = []
    for hd in range(XATTN_HEADS):
        sl = slice(hd * HEAD_DIM, (hd + 1) * HEAD_DIM)
        vs = slice(XATTN_WIDTH + hd * HEAD_DIM, XATTN_WIDTH + (hd + 1) * HEAD_DIM)
        sc = _dot_nt(q[:, sl].astype(BF16), kv[:, sl]) * QK_SCALE
        sc = sc - jnp.max(sc, axis=-1, keepdims=True)
        p = jnp.exp(sc)
        p = p / jnp.sum(p, axis=-1, keepdims=True)
        heads.append(_dot(p.astype(BF16), kv[:, vs]))
    att = jnp.concatenate(heads, axis=1).astype(BF16)
    x2 = x + _dot(att, wo_ref[...])
    o_ref[0] = x2

    hf = _rmsnorm(x2, gf_ref[...])
    h_hi = hf.astype(BF16)
    h_lo = (hf - h_hi.astype(F32)).astype(BF16)
    lg = _dot(h_hi, wrh_ref[...]) + (_dot(h_lo, wrh_ref[...]) + _dot(h_hi, wrl_ref[...]))
    logits = lg.T[0:N_EXPERTS, :]
    cls, w_lo, w_hi = _route(logits, br_ref[...])

    one, zero = jnp.float32(1.0), jnp.float32(0.0)
    crow = lax.broadcasted_iota(jnp.int32, (CLASS_ROWS, ts), 0).astype(F32)
    onehot = jnp.where(crow == cls, one, zero)
    before = lax.broadcasted_iota(jnp.int32, (ts, ts), 0) < lax.broadcasted_iota(jnp.int32, (ts, ts), 1)
    prefix = _dot(onehot.astype(BF16), jnp.where(before, one, zero).astype(BF16))
    count = jnp.sum(onehot, axis=1, keepdims=True)
    fill, cur, nxt = fill_ref[...], cur_ref[...], next_ref[...]
    need = jnp.where(fill + count > tm, one, zero)
    lower = (lax.broadcasted_iota(jnp.int32, (CLASS_ROWS, CLASS_ROWS), 1)
             < lax.broadcasted_iota(jnp.int32, (CLASS_ROWS, CLASS_ROWS), 0))
    opened_before = _dot(jnp.where(lower, one, zero).astype(BF16),
                         jnp.broadcast_to(need, (CLASS_ROWS, META_LANES)).astype(BF16))[:, 0:1]
    new_tile = nxt + opened_before
    per_token = lambda v: jnp.sum(onehot * v, axis=0, keepdims=True)
    slot_in_cur = per_token(fill) + per_token(prefix)
    pos = jnp.where(slot_in_cur >= tm, per_token(new_tile) * tm + (slot_in_cur - tm),
                    per_token(cur) * tm + slot_in_cur)
    fill_ref[...] = jnp.where(need > zero, fill + count - tm, fill + count)
    cur_ref[...] = jnp.where(need > zero, new_tile, cur)
    next_ref[...] = nxt + jnp.sum(need, axis=0, keepdims=True)
    lane = lax.broadcasted_iota(jnp.int32, (CLASS_ROWS, META_LANES), 1)
    alloc_ref[...] = jnp.where(lane == 0, need, jnp.where(lane == 1, new_tile, zero))

    meta = jnp.concatenate([cls, w_lo, w_hi, pos, jnp.zeros((META_LANES - 4, ts), F32)], axis=0)
    meta_ref[...] = meta[0:8, :]

    @pl.when(lin >= 1)
    def _():
        wait_rows(slot)

    rows_ref[slot, :, 0:d] = hf
    rows_ref[slot, :, d:d + META_LANES] = meta.T
    posv_ref[...] = jnp.broadcast_to(pos.astype(jnp.int32), posv_ref.shape)
    pos_to_smem().start()

    @pl.when(lin == total - 1)
    def _():
        pos_to_smem().wait()
        scatter_rows(slot)
        wait_rows(slot)
        wait_rows(1 - slot)


def _xattn_route(x, kv, g_x, w_xq, w_xo, g_f, w_router, b_router, hs_zero):
    n_rows = hs_zero.shape[0]
    b, s, d = x.shape
    ts = SEQ_TILE
    assert ts <= MOE_TILE
    m = kv.shape[1]
    nt = s // ts
    w = d + META_LANES
    row = lambda v: v[None, :]
    wr = jnp.zeros((d, META_LANES), F32).at[:, :N_EXPERTS].set(w_router)
    wr_hi = wr.astype(BF16)
    wr_lo = (wr - wr_hi.astype(F32)).astype(BF16)
    return pl.pallas_call(
        _xattn_route_kernel,
        name="xattn_route",
        grid=(b, nt),
        in_specs=[
            pl.BlockSpec((1, ts, d), lambda i, j: (i, j, 0)),
            pl.BlockSpec((1, m, 2 * XATTN_WIDTH), lambda i, j: (i, 0, 0)),
            _const_spec((1, d)), _const_spec((d, XATTN_WIDTH)), _const_spec((XATTN_WIDTH, d)),
            _const_spec((1, d)), _const_spec((d, META_LANES)), _const_spec((d, META_LANES)),
            _const_spec((N_EXPERTS, 1)),
            pl.BlockSpec(memory_space=pl.ANY),
        ],
        out_specs=[
            pl.BlockSpec((1, ts, d), lambda i, j: (i, j, 0)),
            pl.BlockSpec((8, ts), lambda i, j: (0, i * nt + j)),
            pl.BlockSpec((CLASS_ROWS, META_LANES), lambda i, j: (i * nt + j, 0)),
            pl.BlockSpec(memory_space=pl.ANY),
        ],
        out_shape=[
            jax.ShapeDtypeStruct((b, s, d), F32),
            jax.ShapeDtypeStruct((8, b * s), F32),
            jax.ShapeDtypeStruct((b * nt * CLASS_ROWS, META_LANES), F32),
            jax.ShapeDtypeStruct((n_rows, w), F32),
        ],
        scratch_shapes=[pltpu.VMEM((CLASS_ROWS, 1), F32), pltpu.VMEM((CLASS_ROWS, 1), F32),
                        pltpu.VMEM((1, 1), F32), pltpu.VMEM((2, ts, w), F32),
                        pltpu.VMEM((8, ts), jnp.int32), pltpu.SMEM((1, ts), jnp.int32),
                        pltpu.SemaphoreType.DMA((2,)), pltpu.SemaphoreType.DMA((1,))],
        input_output_aliases={9: 3},
        compiler_params=pltpu.CompilerParams(dimension_semantics=("arbitrary", "arbitrary"),
                                             vmem_limit_bytes=VMEM_LIMIT),
    )(x, kv, row(g_x), w_xq.astype(BF16), w_xo.astype(BF16), row(g_f), wr_hi, wr_lo, b_router[:, None], hs_zero)


def _tile_tables(alloc, n_tiles):
    opened = alloc[:, 0] > 0.0
    tile_of = alloc[:, 1].astype(jnp.int32)
    cls_of = jnp.arange(alloc.shape[0], dtype=jnp.int32) % CLASS_ROWS
    tile_ids = jnp.arange(n_tiles, dtype=jnp.int32)
    hit = opened[None, :] & (tile_of[None, :] == tile_ids[:, None])
    tile_cls = jnp.sum(jnp.where(hit, cls_of[None, :], 0), axis=1)
    used = jnp.any(hit, axis=1)
    n_used = jnp.sum(used.astype(jnp.int32))
    visit = jnp.asarray(PAIR_VISIT_RANK, jnp.int32)
    key = jnp.where(used, len(PAIRS) * (tile_cls // len(PAIRS)) + visit[tile_cls % len(PAIRS)], N_CLASSES)
    ahead = (key[None, :] < key[:, None]) | ((key[None, :] == key[:, None]) & (tile_ids[None, :] < tile_ids[:, None]))
    rank = jnp.sum(ahead.astype(jnp.int32), axis=1)
    order = jnp.sum(jnp.where(rank[None, :] == tile_ids[:, None], tile_ids[None, :], 0), axis=1)
    order_in = order[jnp.minimum(tile_ids, n_used - 1)]
    step_cls = tile_cls[order_in]
    pair = step_cls % len(PAIRS)
    base = EXPERTS_PER_GROUP * (step_cls // len(PAIRS))
    slot_a = jnp.asarray(PAIR_SLOT_A, jnp.int32)[pair]
    slot_b = jnp.asarray(PAIR_SLOT_B, jnp.int32)[pair]
    i32 = lambda v: v.astype(jnp.int32)
    return (i32(order_in), i32(order), i32(base + slot_a), i32(base + slot_b), i32(slot_a > slot_b),
            i32(n_used.reshape(1)))


def _moe_kernel(oin_ref, oout_ref, ea_ref, eb_ref, swap_ref, nused_ref, rows_ref,
                wg_a, wg_b, wu_a, wu_b, wd_a, wd_b, y_ref):
    tm, d = y_ref.shape
    i = pl.program_id(0)
    used = i < nused_ref[0]

    @pl.when(used)
    def _():
        hb = rows_ref[:, 0:d].astype(BF16)
        w_lower, w_higher = rows_ref[:, d + 1:d + 2], rows_ref[:, d + 2:d + 3]
        a_is_higher = swap_ref[i] == 1
        w_a = jnp.where(a_is_higher, w_higher, w_lower)
        w_b = jnp.where(a_is_higher, w_lower, w_higher)
        y = None
        for w_slot, wg, wu, wd in ((w_a, wg_a, wu_a, wd_a), (w_b, wg_b, wu_b, wd_b)):
            gate = _dot(hb, wg[0].astype(BF16))
            act = (gate * _sigmoid(gate)) * _dot(hb, wu[0].astype(BF16))
            part = w_slot * _dot(act.astype(BF16), wd[0].astype(BF16))
            y = part if y is None else y + part
        y_ref[...] = y

    @pl.when(jnp.logical_not(used))
    def _():
        y_ref[...] = jnp.zeros_like(y_ref)


def _moe(hs, order_in, order_out, e_a, e_b, swap, n_used, w_gate, w_up, w_down, n_tiles, tm):
    w = hs.shape[1]
    d = w - META_LANES
    f = w_gate.shape[2]
    a3 = lambda i, oin, oout, ea, eb, sw, nu: (ea[i], 0, 0)
    b3 = lambda i, oin, oout, ea, eb, sw, nu: (eb[i], 0, 0)
    grid_spec = pltpu.PrefetchScalarGridSpec(
        num_scalar_prefetch=6,
        grid=(n_tiles,),
        in_specs=[
            pl.BlockSpec((tm, w), lambda i, oin, oout, ea, eb, sw, nu: (oin[i], 0)),
            pl.BlockSpec((1, d, f), a3), pl.BlockSpec((1, d, f), b3),
            pl.BlockSpec((1, d, f), a3), pl.BlockSpec((1, d, f), b3),
            pl.BlockSpec((1, f, d), a3), pl.BlockSpec((1, f, d), b3),
        ],
        out_specs=pl.BlockSpec((tm, d), lambda i, oin, oout, ea, eb, sw, nu: (oout[i], 0)),
    )
    return pl.pallas_call(
        _moe_kernel,
        name="moe_experts",
        grid_spec=grid_spec,
        out_shape=jax.ShapeDtypeStruct((n_tiles * tm, d), F32),
        compiler_params=pltpu.CompilerParams(dimension_semantics=("arbitrary",), vmem_limit_bytes=VMEM_LIMIT),
    )(order_in, order_out, e_a, e_b, swap, n_used, hs, w_gate, w_gate, w_up, w_up, w_down, w_down)


def _combine_kernel(final_norm, pos_ref, x_ref, y_hbm, g_ref, o_ref, rows_ref, sem):
    tc = x_ref.shape[0]
    i = pl.program_id(0)
    n = pl.num_programs(0)

    def issue(tile, slot):
        base = tile * tc
        for r in range(tc):
            pltpu.make_async_copy(y_hbm.at[pl.ds(pos_ref[base + r], 1)], rows_ref.at[slot, pl.ds(r, 1)],
                                  sem.at[slot]).start()

    @pl.when(i == 0)
    def _():
        issue(0, 0)

    for slot in range(2):
        @pl.when(i % 2 == slot)
        def _():
            @pl.when(i + 1 < n)
            def _():
                issue(i + 1, 1 - slot)

            pltpu.make_async_copy(y_hbm.at[pl.ds(0, tc)], rows_ref.at[slot], sem.at[slot]).wait()
            x = x_ref[...] + rows_ref[slot]
            if final_norm:
                x = _rmsnorm(x, g_ref[...])
            o_ref[...] = x


def _combine(x, y_sorted, pos, g, final_norm):
    t, d = x.shape
    tc = COMBINE_TILE
    grid_spec = pltpu.PrefetchScalarGridSpec(
        num_scalar_prefetch=1,
        grid=(t // tc,),
        in_specs=[pl.BlockSpec((tc, d), lambda i, p: (i, 0)), pl.BlockSpec(memory_space=pl.ANY),
                  pl.BlockSpec((1, d), lambda i, p: (0, 0))],
        out_specs=pl.BlockSpec((tc, d), lambda i, p: (i, 0)),
        scratch_shapes=[pltpu.VMEM((2, tc, d), F32), pltpu.SemaphoreType.DMA((2,))],
    )
    return pl.pallas_call(
        functools.partial(_combine_kernel, final_norm),
        name="combine",
        grid_spec=grid_spec,
        out_shape=jax.ShapeDtypeStruct((t, d), F32),
        compiler_params=pltpu.CompilerParams(dimension_semantics=("arbitrary",), vmem_limit_bytes=VMEM_LIMIT),
    )(pos, x, y_sorted, g[None, :])


def kernel(x, mem, positions, norm_mix_g, w_in, b_branch_gate, ret_norm_g, w_ret_out, conv_w, conv_b, conv_ln_g, conv_ln_b, w_conv_out, w_mix_out, norm_xattn_g, norm_mem_g, w_xq, w_xkv, w_xo, norm_ffn_g, w_router, b_router, w_exp_gate, w_exp_up, w_exp_down, norm_final_g):
    b, s, d = x.shape
    depth = w_in.shape[0]
    t = b * s
    assert s % SEQ_TILE == 0 and s % MIXER_TILE == 0 and MIXER_TILE % RET_CHUNK == 0
    assert t % MOE_TILE == 0 and t % COMBINE_TILE == 0
    tm = MOE_TILE
    n_tiles = t // tm + N_CLASSES
    cos, sin = _rope_tables(positions)
    n_exp, _, f = w_exp_gate.shape[1:]
    wg_all = w_exp_gate.reshape(depth * n_exp, d, f)
    wu_all = w_exp_up.reshape(depth * n_exp, d, f)
    wd_all = w_exp_down.reshape(depth * n_exp, f, d)
    moe = None
    for l in range(depth):
        x, hs_zero = _mixer(x, cos, sin, norm_mix_g[l], w_in[l], b_branch_gate[l], ret_norm_g[l], w_ret_out[l],
                            conv_w[l], conv_b[l], conv_ln_g[l], conv_ln_b[l], w_conv_out[l], w_mix_out[l],
                            (n_tiles + 1) * tm, moe)
        kv = _mem_kv(mem, norm_mem_g[l], w_xkv[l].astype(BF16))
        x, meta, alloc, hs = _xattn_route(x, kv, norm_xattn_g[l], w_xq[l], w_xo[l], norm_ffn_g[l],
                                          w_router, b_router, hs_zero)
        pos = meta[3].astype(jnp.int32)
        order_in, order_out, e_a, e_b, swap, n_used = _tile_tables(alloc, n_tiles)
        y_sorted = _moe(hs, order_in, order_out, e_a + l * n_exp, e_b + l * n_exp, swap, n_used,
                        wg_all, wu_all, wd_all, n_tiles, tm)
        moe = (y_sorted, pos)
    return _combine(x.reshape(t, d), y_sorted, pos, norm_final_g, True).reshape(b, s, d)
```

```python
import functools

import jax
import jax.numpy as jnp
import numpy as np
from jax import lax
from jax.experimental import pallas as pl
from jax.experimental.pallas import tpu as pltpu

F32 = jnp.float32
BF16 = jnp.bfloat16

RET_HEADS = 4
HEAD_DIM = 128
RET_WIDTH = RET_HEADS * HEAD_DIM
RET_CHUNK = 128
CONV_CH = 512
CONV_WIDTH = 31
CONV_BLOCK = 128
CONV_HALO = 32
XATTN_HEADS = 4
XATTN_WIDTH = XATTN_HEADS * HEAD_DIM
N_EXPERTS = 16
N_GROUPS = 4
EXPERTS_PER_GROUP = 4
ROPE_BASE = 10000.0
EPS = 1e-6
QK_SCALE = HEAD_DIM ** -0.5

PAIRS = ((0, 1), (0, 2), (0, 3), (1, 2), (1, 3), (2, 3))
N_CLASSES = N_GROUPS * len(PAIRS)
PAIR_VISIT_RANK = (0, 3, 2, 5, 1, 4)
PAIR_SLOT_A = (0, 2, 3, 2, 3, 2)
PAIR_SLOT_B = (1, 0, 0, 1, 1, 3)
CLASS_ROWS = 32
META_LANES = 128

SEQ_TILE = 256
MIXER_TILE = 256
MOE_TILE = 256
COMBINE_TILE = 512
VMEM_LIMIT = 56 * 1024 * 1024


def _rmsnorm(x, g):
    return x * lax.rsqrt(jnp.mean(x * x, axis=-1, keepdims=True) + EPS) * g


def _sigmoid(x):
    return 1.0 / (1.0 + jnp.exp(-x))


def _dot(a, b):
    return jnp.dot(a, b, preferred_element_type=F32)


def _dot_nt(a, b):
    return lax.dot_general(a, b, (((1,), (1,)), ((), ())), preferred_element_type=F32)


def _dot_tn(a, b):
    return lax.dot_general(a, b, (((0,), (0,)), ((), ())), preferred_element_type=F32)


def _const_spec(shape):
    n = len(shape)
    return pl.BlockSpec(shape, lambda *_: (0,) * n, pipeline_mode=pl.Buffered(1))


def _rope_kernel(pos_ref, invf_ref, cos_ref, sin_ref):
    ang = pos_ref[...] * invf_ref[...]
    lane = lax.broadcasted_iota(jnp.int32, ang.shape, 1)
    s = jnp.sin(ang)
    cos_ref[...] = jnp.cos(ang)
    sin_ref[...] = jnp.where(lane < HEAD_DIM // 2, -s, s)


def _rope_tables(positions):
    s = positions.shape[0]
    half = HEAD_DIM // 2
    inv_freq = ROPE_BASE ** (-jnp.arange(half, dtype=F32) / half)
    invf = jnp.concatenate([inv_freq, inv_freq])[None, :]
    pos = positions.astype(F32)[:, None]
    ts = min(s, 1024)
    return pl.pallas_call(
        _rope_kernel,
        name="rope_tables",
        grid=(s // ts,),
        in_specs=[pl.BlockSpec((ts, 1), lambda i: (i, 0)), pl.BlockSpec((1, HEAD_DIM), lambda i: (0, 0))],
        out_specs=[pl.BlockSpec((ts, HEAD_DIM), lambda i: (i, 0))] * 2,
        out_shape=[jax.ShapeDtypeStruct((s, HEAD_DIM), F32)] * 2,
    )(pos, invf)


def _memkv_kernel(mem_ref, g_ref, w_ref, kv_ref):
    m = _rmsnorm(mem_ref[0], g_ref[...]).astype(BF16)
    kv_ref[0] = _dot(m, w_ref[...]).astype(BF16)


def _mem_kv(mem, g, w_xkv):
    b, m, d = mem.shape
    return pl.pallas_call(
        _memkv_kernel,
        name="mem_kv",
        grid=(b,),
        in_specs=[pl.BlockSpec((1, m, d), lambda i: (i, 0, 0)), pl.BlockSpec((1, d), lambda i: (0, 0)),
                  pl.BlockSpec((d, 2 * XATTN_WIDTH), lambda i: (0, 0))],
        out_specs=pl.BlockSpec((1, m, 2 * XATTN_WIDTH), lambda i: (i, 0, 0)),
        out_shape=jax.ShapeDtypeStruct((b, m, 2 * XATTN_WIDTH), BF16),
        compiler_params=pltpu.CompilerParams(vmem_limit_bytes=VMEM_LIMIT),
    )(mem, g[None, :], w_xkv)


def _retention_tables():
    h = np.arange(RET_HEADS, dtype=np.float64)
    log_gamma = np.log(1.0 - 2.0 ** (-5.0 - h))
    idx = np.arange(RET_CHUNK, dtype=np.float64)
    diff = idx[:, None] - idx[None, :]
    decay = np.where(diff[None] >= 0.0, np.exp(np.maximum(diff, 0.0)[None] * log_gamma[:, None, None]), 0.0)
    zeta = np.exp((RET_CHUNK - 1.0 - idx)[None, :] * log_gamma[:, None])
    xi = np.exp((idx + 1.0)[None, :] * log_gamma[:, None])
    chunk_decay = np.exp(RET_CHUNK * log_gamma)
    bc = lambda t: np.broadcast_to(t[:, :, None], (RET_HEADS, RET_CHUNK, HEAD_DIM))
    return (jnp.asarray(decay, F32), jnp.asarray(bc(zeta), F32), jnp.asarray(bc(xi), F32),
            tuple(float(c) for c in chunk_decay))


def _gather_rows(idx_ref, tile, n, src_hbm, dst_ref, slot, sem):
    base = tile * n
    for r in range(n):
        pltpu.make_async_copy(src_hbm.at[pl.ds(idx_ref[base + r], 1)], dst_ref.at[slot, pl.ds(r, 1)],
                              sem.at[slot]).start()


def _wait_gathered(n, src_hbm, dst_ref, slot, sem):
    pltpu.make_async_copy(src_hbm.at[pl.ds(0, n)], dst_ref.at[slot], sem.at[slot]).wait()


def _mixer_kernel(chunk_decay, add_moe, *refs):
    if add_moe:
        pos_ref, refs = refs[0], refs[1:]
    (x_ref, cos_ref, sin_ref, g_ref, win_ref, bg_ref, rg_ref, wro_ref, cw_ref, cb_ref, lng_ref, lnb_ref,
     wco_ref, wmo_ref, decay_ref, zeta_ref, xi_ref) = refs[:17]
    refs = refs[17:]
    if add_moe:
        y_hbm, refs = refs[0], refs[1:]
    o_ref, hs_hbm, state_ref, zext_ref, phase_ref, u_ref, zeros_ref, zsem = refs[:8]
    ts = x_ref.shape[1]
    d = x_ref.shape[2]
    lin = pl.program_id(0) * pl.num_programs(1) + pl.program_id(1)
    total = pl.num_programs(0) * pl.num_programs(1)
    slot = lin % 2

    @pl.when(pl.program_id(1) == 0)
    def _():
        state_ref[...] = jnp.zeros_like(state_ref)
        zext_ref[0:CONV_HALO, :] = jnp.zeros((CONV_HALO, CONV_CH), F32)

    chunk = zeros_ref.shape[0]

    def zero_chunk(k):
        start = jnp.minimum(k * chunk, hs_hbm.shape[0] - chunk)
        return pltpu.make_async_copy(zeros_ref, hs_hbm.at[pl.ds(start, chunk)], zsem.at[0])

    @pl.when(lin == 0)
    def _():
        zeros_ref[...] = jnp.zeros_like(zeros_ref)

    @pl.when(lin >= 1)
    def _():
        zero_chunk(lin - 1).wait()

    zero_chunk(lin).start()

    x = x_ref[0]
    if add_moe:
        rows_ref, sem = refs[8:10]

        @pl.when(lin == 0)
        def _():
            _gather_rows(pos_ref, 0, ts, y_hbm, rows_ref, 0, sem)

        _wait_gathered(ts, y_hbm, rows_ref, slot, sem)
        x = x + rows_ref[slot]
        _gather_rows(pos_ref, jnp.minimum(lin + 1, total - 1), ts, y_hbm, rows_ref, 1 - slot, sem)
    h = _rmsnorm(x, g_ref[...]).astype(BF16)

    def proj(a, b):
        return _dot(h, win_ref[:, a:b])

    o_q, o_k, o_v, o_g = RET_WIDTH, 2 * RET_WIDTH, 3 * RET_WIDTH, 4 * RET_WIDTH
    o_c = o_g + 2 * CONV_CH

    ca, cg = proj(o_g, o_g + CONV_CH), proj(o_g + CONV_CH, o_c)
    zext_ref[CONV_HALO:CONV_HALO + ts, :] = ca * _sigmoid(cg)
    first = CONV_HALO - (CONV_WIDTH - 1)
    other_cols = [(c0, c0 + 256) for c0 in range(0, o_g, 256)] + [(c0, c0 + 256) for c0 in range(o_c, o_c + 2 * d, 256)]
    for k in range(len(other_cols)):
        u_ref[:, 256 * k:256 * (k + 1)] = proj(*other_cols[k])
    blocks = []
    for cb in range(CONV_CH // CONV_BLOCK):
        cs = slice(cb * CONV_BLOCK, (cb + 1) * CONV_BLOCK)
        acc = None
        for b in range(8):
            rows = ts if b == 0 else ts + 8
            part = None
            for a in range(CONV_HALO // 8 + 1):
                w = 8 * a + b - first
                if 0 <= w < CONV_WIDTH:
                    term = zext_ref[8 * a:8 * a + rows, cs] * cw_ref[w:w + 1, cs]
                    part = term if part is None else part + term
            if b == 0:
                acc = part
            else:
                phase_ref[b - 1, :, cs] = part
                acc = acc + phase_ref[b - 1, b:b + ts, cs]
        blocks.append(acc)
    acc = jnp.concatenate(blocks, axis=1) + cb_ref[...]
    zext_ref[0:CONV_HALO, :] = zext_ref[ts:ts + CONV_HALO, :]
    mu = jnp.mean(acc, axis=-1, keepdims=True)
    cen = acc - mu
    var = jnp.mean(cen * cen, axis=-1, keepdims=True)
    zf = cen * lax.rsqrt(var + EPS) * lng_ref[...] + lnb_ref[...]
    y_conv = _dot((zf * _sigmoid(zf)).astype(BF16), wco_ref[...])
    uq, uk, uv, sg = (u_ref[:, c0:c0 + RET_WIDTH] for c0 in (0, o_q, o_k, o_v))
    gate_pre = u_ref[:, o_g:o_g + 2 * d]

    cos, sin = cos_ref[...], sin_ref[...]
    heads = []
    for hd in range(RET_HEADS):
        sl = slice(hd * HEAD_DIM, (hd + 1) * HEAD_DIM)
        qh = uq[:, sl]
        qh = qh * cos + pltpu.roll(qh, HEAD_DIM // 2, 1) * sin
        kh = uk[:, sl]
        kh = (kh * cos + pltpu.roll(kh, HEAD_DIM // 2, 1) * sin) * QK_SCALE
        vh = uv[:, sl]
        rows = []
        for c in range(ts // RET_CHUNK):
            r = slice(c * RET_CHUNK, (c + 1) * RET_CHUNK)
            qc = qh[r].astype(BF16)
            kc = kh[r]
            vc = vh[r].astype(BF16)
            sc = _dot_nt(qc, kc.astype(BF16)) * decay_ref[hd]
            inner = _dot(sc.astype(BF16), vc)
            st = state_ref[hd]
            cross = _dot(qc, st.astype(BF16)) * xi_ref[hd]
            kz = (kc * zeta_ref[hd]).astype(BF16)
            state_ref[hd] = chunk_decay[hd] * st + _dot_tn(kz, vc)
            rows.append(inner + cross)
        rh = jnp.concatenate(rows, axis=0)
        heads.append(rh * lax.rsqrt(jnp.mean(rh * rh, axis=-1, keepdims=True) + EPS))
    r = jnp.concatenate(heads, axis=1) * rg_ref[...]
    r = (sg * _sigmoid(sg)) * r
    y_ret = _dot(r.astype(BF16), wro_ref[...])

    gate = _sigmoid(gate_pre + bg_ref[...])
    merged = (gate[:, 0:d] * y_ret + gate[:, d:2 * d] * y_conv).astype(BF16)
    o_ref[0] = x + _dot(merged, wmo_ref[...])

    @pl.when(lin == total - 1)
    def _():
        zero_chunk(lin).wait()
        if add_moe:
            _wait_gathered(ts, y_hbm, rows_ref, 1 - slot, sem)


def _mixer(x, cos, sin, g, w_in, b_gate, ret_g, w_ret_out, conv_w, conv_b, ln_g, ln_b, w_conv_out, w_mix_out,
           n_rows, moe=None):
    b, s, d = x.shape
    ts = MIXER_TILE
    steps = b * (s // ts)
    per_step = -(-n_rows // steps)
    zero_rows = min(n_rows, -(-per_step // 8) * 8)
    decay, zeta, xi, chunk_decay = _retention_tables()
    in_cols = w_in.shape[1]
    cw = jnp.zeros((CONV_HALO, CONV_CH), F32).at[:CONV_WIDTH].set(conv_w)
    row = lambda v: v[None, :]
    tab = (RET_HEADS, RET_CHUNK, HEAD_DIM)
    add_moe = moe is not None
    in_specs = [
        pl.BlockSpec((1, ts, d), lambda i, j, *_: (i, j, 0)),
        pl.BlockSpec((ts, HEAD_DIM), lambda i, j, *_: (j, 0)),
        pl.BlockSpec((ts, HEAD_DIM), lambda i, j, *_: (j, 0)),
        _const_spec((1, d)), _const_spec((d, in_cols)), _const_spec((1, 2 * d)),
        _const_spec((1, RET_WIDTH)), _const_spec((RET_WIDTH, d)),
        _const_spec((CONV_HALO, CONV_CH)), _const_spec((1, CONV_CH)), _const_spec((1, CONV_CH)),
        _const_spec((1, CONV_CH)), _const_spec((CONV_CH, d)), _const_spec((d, d)),
        _const_spec((RET_HEADS, RET_CHUNK, RET_CHUNK)), _const_spec(tab), _const_spec(tab),
    ]
    scratch = [pltpu.VMEM((RET_HEADS, HEAD_DIM, HEAD_DIM), F32),
               pltpu.VMEM((CONV_HALO + ts, CONV_CH), F32),
               pltpu.VMEM((7, ts + 8, CONV_CH), F32),
               pltpu.VMEM((ts, in_cols - 2 * CONV_CH), F32),
               pltpu.VMEM((zero_rows, d + META_LANES), F32), pltpu.SemaphoreType.DMA((1,))]
    args = [x, cos, sin, row(g), w_in.astype(BF16), row(b_gate), row(ret_g), w_ret_out.astype(BF16), cw,
            row(conv_b), row(ln_g), row(ln_b), w_conv_out.astype(BF16), w_mix_out.astype(BF16), decay, zeta, xi]
    if add_moe:
        y_sorted, pos = moe
        in_specs.append(pl.BlockSpec(memory_space=pl.ANY))
        scratch += [pltpu.VMEM((2, ts, d), F32), pltpu.SemaphoreType.DMA((2,))]
        args = [pos] + args + [y_sorted]
    grid_spec = pltpu.PrefetchScalarGridSpec(
        num_scalar_prefetch=1 if add_moe else 0,
        grid=(b, s // ts),
        in_specs=in_specs,
        out_specs=[pl.BlockSpec((1, ts, d), lambda i, j, *_: (i, j, 0)), pl.BlockSpec(memory_space=pl.ANY)],
        scratch_shapes=scratch,
    )
    return pl.pallas_call(
        functools.partial(_mixer_kernel, chunk_decay, add_moe),
        name="mixer",
        grid_spec=grid_spec,
        out_shape=[jax.ShapeDtypeStruct((b, s, d), F32), jax.ShapeDtypeStruct((n_rows, d + META_LANES), F32)],
        compiler_params=pltpu.CompilerParams(dimension_semantics=("arbitrary", "arbitrary"),
                                             vmem_limit_bytes=VMEM_LIMIT),
    )(*args)


def _route(logits, bias):
    scores = _sigmoid(logits)
    sel = scores + bias
    one, zero = jnp.float32(1.0), jnp.float32(0.0)
    top, gscore = [], []
    for g in range(N_GROUPS):
        a = [sel[EXPERTS_PER_GROUP * g + i:EXPERTS_PER_GROUP * g + i + 1, :] for i in range(EXPERTS_PER_GROUP)]
        tg, sg = [], None
        for i in range(EXPERTS_PER_GROUP):
            rank = None
            for j in range(EXPERTS_PER_GROUP):
                if j == i:
                    continue
                ahead = (a[j] >= a[i]) if j < i else (a[j] > a[i])
                ahead = jnp.where(ahead, one, zero)
                rank = ahead if rank is None else rank + ahead
            in_top = rank < 2.0
            tg.append(in_top)
            contrib = jnp.where(in_top, a[i], zero)
            sg = contrib if sg is None else sg + contrib
        top.append(tg)
        gscore.append(sg)
    cls = jnp.zeros_like(gscore[0])
    w_lo = jnp.zeros_like(cls)
    w_hi = jnp.zeros_like(cls)
    for g in range(N_GROUPS):
        behind = None
        for g2 in range(N_GROUPS):
            if g2 == g:
                continue
            ahead = (gscore[g2] >= gscore[g]) if g2 < g else (gscore[g2] > gscore[g])
            behind = ahead if behind is None else (behind | ahead)
        best = jnp.logical_not(behind)
        for p, (i, j) in enumerate(PAIRS):
            active = best & top[g][i] & top[g][j]
            e_lo, e_hi = EXPERTS_PER_GROUP * g + i, EXPERTS_PER_GROUP * g + j
            cls = jnp.where(active, jnp.float32(len(PAIRS) * g + p), cls)
            w_lo = jnp.where(active, scores[e_lo:e_lo + 1, :], w_lo)
            w_hi = jnp.where(active, scores[e_hi:e_hi + 1, :], w_hi)
    den = w_lo + w_hi
    return cls, w_lo / den, w_hi / den


def _xattn_route_kernel(x_ref, kv_ref, gx_ref, wq_ref, wo_ref, gf_ref, wrh_ref, wrl_ref, br_ref,
                        init_hbm, o_ref, meta_ref, alloc_ref, hs_hbm,
                        fill_ref, cur_ref, next_ref, rows_ref, posv_ref, pos_smem, sem, sem_pos):
    del init_hbm
    ts = x_ref.shape[1]
    d = x_ref.shape[2]
    tm = float(MOE_TILE)
    lin = pl.program_id(0) * pl.num_programs(1) + pl.program_id(1)
    total = pl.num_programs(0) * pl.num_programs(1)
    slot = lin % 2

    def wait_rows(sl):
        pltpu.make_async_copy(rows_ref.at[sl], hs_hbm.at[pl.ds(0, ts)], sem.at[sl]).wait()

    @pl.when(lin == 0)
    def _():
        fill_ref[...] = jnp.full(fill_ref.shape, tm, F32)
        cur_ref[...] = jnp.zeros_like(cur_ref)
        next_ref[...] = jnp.zeros_like(next_ref)
        rows_ref[1] = jnp.zeros(rows_ref.shape[1:], F32)
        for r in range(ts):
            pos_smem[0, r] = hs_hbm.shape[0] - ts + r

    def pos_to_smem():
        return pltpu.make_async_copy(posv_ref.at[pl.ds(0, 1)], pos_smem, sem_pos.at[0])

    def scatter_rows(sl):
        for r in range(ts):
            pltpu.make_async_copy(rows_ref.at[sl, pl.ds(r, 1)], hs_hbm.at[pl.ds(pos_smem[0, r], 1)],
                                  sem.at[sl]).start()

    @pl.when(lin >= 1)
    def _():
        pos_to_smem().wait()

    scatter_rows(1 - slot)

    x = x_ref[0]
    h = _rmsnorm(x, gx_ref[...]).astype(BF16)
    q = _dot(h, wq_ref[...])
    kv = kv_ref[0]
    heads = []
    for hd in range(XATTN_HEADS):
        sl = slice(hd * HEAD_DIM, (hd + 1) * HEAD_DIM)
        vs = slice(XATTN_WIDTH + hd * HEAD_DIM, XATTN_WIDTH + (hd + 1) * HEAD_DIM)
        sc = _dot_nt(q[:, sl].astype(BF16), kv[:, sl]) * QK_SCALE
        sc = sc - jnp.max(sc, axis=-1, keepdims=True)
        p = jnp.exp(sc)
        p = p / jnp.sum(p, axis=-1, keepdims=True)
        heads.append(_dot(p.astype(BF16), kv[:, vs]))
    att = jnp.concatenate(heads, axis=1).astype(BF16)
    x2 = x + _dot(att, wo_ref[...])
    o_ref[0] = x2

    hf = _rmsnorm(x2, gf_ref[...])
    h_hi = hf.astype(BF16)
    h_lo = (hf - h_hi.astype(F32)).astype(BF16)
    lg = _dot(h_hi, wrh_ref[...]) + (_dot(h_lo, wrh_ref[...]) + _dot(h_hi, wrl_ref[...]))
    logits = lg.T[0:N_EXPERTS, :]
    cls, w_lo, w_hi = _route(logits, br_ref[...])

    one, zero = jnp.float32(1.0), jnp.float32(0.0)
    crow = lax.broadcasted_iota(jnp.int32, (CLASS_ROWS, ts), 0).astype(F32)
    onehot = jnp.where(crow == cls, one, zero)
    before = lax.broadcasted_iota(jnp.int32, (ts, ts), 0) < lax.broadcasted_iota(jnp.int32, (ts, ts), 1)
    prefix = _dot(onehot.astype(BF16), jnp.where(before, one, zero).astype(BF16))
    count = jnp.sum(onehot, axis=1, keepdims=True)
    fill, cur, nxt = fill_ref[...], cur_ref[...], next_ref[...]
    need = jnp.where(fill + count > tm, one, zero)
    lower = (lax.broadcasted_iota(jnp.int32, (CLASS_ROWS, CLASS_ROWS), 1)
             < lax.broadcasted_iota(jnp.int32, (CLASS_ROWS, CLASS_ROWS), 0))
    opened_before = _dot(jnp.where(lower, one, zero).astype(BF16),
                         jnp.broadcast_to(need, (CLASS_ROWS, META_LANES)).astype(BF16))[:, 0:1]
    new_tile = nxt + opened_before
    per_token = lambda v: jnp.sum(onehot * v, axis=0, keepdims=True)
    slot_in_cur = per_token(fill) + per_token(prefix)
    pos = jnp.where(slot_in_cur >= tm, per_token(new_tile) * tm + (slot_in_cur - tm),
                    per_token(cur) * tm + slot_in_cur)
    fill_ref[...] = jnp.where(need > zero, fill + count - tm, fill + count)
    cur_ref[...] = jnp.where(need > zero, new_tile, cur)
    next_ref[...] = nxt + jnp.sum(need, axis=0, keepdims=True)
    lane = lax.broadcasted_iota(jnp.int32, (CLASS_ROWS, META_LANES), 1)
    alloc_ref[...] = jnp.where(lane == 0, need, jnp.where(lane == 1, new_tile, zero))

    meta = jnp.concatenate([cls, w_lo, w_hi, pos, jnp.zeros((META_LANES - 4, ts), F32)], axis=0)
    meta_ref[...] = meta[0:8, :]

    @pl.when(lin >= 1)
    def _():
        wait_rows(slot)

    rows_ref[slot, :, 0:d] = hf
    rows_ref[slot, :, d:d + META_LANES] = meta.T
    posv_ref[...] = jnp.broadcast_to(pos.astype(jnp.int32), posv_ref.shape)
    pos_to_smem().start()

    @pl.when(lin == total - 1)
    def _():
        pos_to_smem().wait()
        scatter_rows(slot)
        wait_rows(slot)
        wait_rows(1 - slot)


def _xattn_route(x, kv, g_x, w_xq, w_xo, g_f, w_router, b_router, hs_zero):
    n_rows = hs_zero.shape[0]
    b, s, d = x.shape
    ts = SEQ_TILE
    assert ts <= MOE_TILE
    m = kv.shape[1]
    nt = s // ts
    w = d + META_LANES
    row = lambda v: v[None, :]
    wr = jnp.zeros((d, META_LANES), F32).at[:, :N_EXPERTS].set(w_router)
    wr_hi = wr.astype(BF16)
    wr_lo = (wr - wr_hi.astype(F32)).astype(BF16)
    return pl.pallas_call(
        _xattn_route_kernel,
        name="xattn_route",
        grid=(b, nt),
        in_specs=[
            pl.BlockSpec((1, ts, d), lambda i, j: (i, j, 0)),
            pl.BlockSpec((1, m, 2 * XATTN_WIDTH), lambda i, j: (i, 0, 0)),
            _const_spec((1, d)), _const_spec((d, XATTN_WIDTH)), _const_spec((XATTN_WIDTH, d)),
            _const_spec((1, d)), _const_spec((d, META_LANES)), _const_spec((d, META_LANES)),
            _const_spec((N_EXPERTS, 1)),
            pl.BlockSpec(memory_space=pl.ANY),
        ],
        out_specs=[
            pl.BlockSpec((1, ts, d), lambda i, j: (i, j, 0)),
            pl.BlockSpec((8, ts), lambda i, j: (0, i * nt + j)),
            pl.BlockSpec((CLASS_ROWS, META_LANES), lambda i, j: (i * nt + j, 0)),
            pl.BlockSpec(memory_space=pl.ANY),
        ],
        out_shape=[
            jax.ShapeDtypeStruct((b, s, d), F32),
            jax.ShapeDtypeStruct((8, b * s), F32),
            jax.ShapeDtypeStruct((b * nt * CLASS_ROWS, META_LANES), F32),
            jax.ShapeDtypeStruct((n_rows, w), F32),
        ],
        scratch_shapes=[pltpu.VMEM((CLASS_ROWS, 1), F32), pltpu.VMEM((CLASS_ROWS, 1), F32),
                        pltpu.VMEM((1, 1), F32), pltpu.VMEM((2, ts, w), F32),
                        pltpu.VMEM((8, ts), jnp.int32), pltpu.SMEM((1, ts), jnp.int32),
                        pltpu.SemaphoreType.DMA((2,)), pltpu.SemaphoreType.DMA((1,))],
        input_output_aliases={9: 3},
        compiler_params=pltpu.CompilerParams(dimension_semantics=("arbitrary", "arbitrary"),
                                             vmem_limit_bytes=VMEM_LIMIT),
    )(x, kv, row(g_x), w_xq.astype(BF16), w_xo.astype(BF16), row(g_f), wr_hi, wr_lo, b_router[:, None], hs_zero)


def _tile_tables(alloc, n_tiles):
    opened = alloc[:, 0] > 0.0
    tile_of = alloc[:, 1].astype(jnp.int32)
    cls_of = jnp.arange(alloc.shape[0], dtype=jnp.int32) % CLASS_ROWS
    tile_ids = jnp.arange(n_tiles, dtype=jnp.int32)
    hit = opened[None, :] & (tile_of[None, :] == tile_ids[:, None])
    tile_cls = jnp.sum(jnp.where(hit, cls_of[None, :], 0), axis=1)
    used = jnp.any(hit, axis=1)
    n_used = jnp.sum(used.astype(jnp.int32))
    visit = jnp.asarray(PAIR_VISIT_RANK, jnp.int32)
    key = jnp.where(used, len(PAIRS) * (tile_cls // len(PAIRS)) + visit[tile_cls % len(PAIRS)], N_CLASSES)
    ahead = (key[None, :] < key[:, None]) | ((key[None, :] == key[:, None]) & (tile_ids[None, :] < tile_ids[:, None]))
    rank = jnp.sum(ahead.astype(jnp.int32), axis=1)
    order = jnp.sum(jnp.where(rank[None, :] == tile_ids[:, None], tile_ids[None, :], 0), axis=1)
    order_in = order[jnp.minimum(tile_ids, n_used - 1)]
    step_cls = tile_cls[order_in]
    pair = step_cls % len(PAIRS)
    base = EXPERTS_PER_GROUP * (step_cls // len(PAIRS))
    slot_a = jnp.asarray(PAIR_SLOT_A, jnp.int32)[pair]
    slot_b = jnp.asarray(PAIR_SLOT_B, jnp.int32)[pair]
    i32 = lambda v: v.astype(jnp.int32)
    return (i32(order_in), i32(order), i32(base + slot_a), i32(base + slot_b), i32(slot_a > slot_b),
            i32(n_used.reshape(1)))


def _moe_kernel(oin_ref, oout_ref, ea_ref, eb_ref, swap_ref, nused_ref, rows_ref,
                wg_a, wg_b, wu_a, wu_b, wd_a, wd_b, y_ref):
    tm, d = y_ref.shape
    i = pl.program_id(0)
    used = i < nused_ref[0]

    @pl.when(used)
    def _():
        hb = rows_ref[:, 0:d].astype(BF16)
        w_lower, w_higher = rows_ref[:, d + 1:d + 2], rows_ref[:, d + 2:d + 3]
        a_is_higher = swap_ref[i] == 1
        w_a = jnp.where(a_is_higher, w_higher, w_lower)
        w_b = jnp.where(a_is_higher, w_lower, w_higher)
        y = None
        for w_slot, wg, wu, wd in ((w_a, wg_a, wu_a, wd_a), (w_b, wg_b, wu_b, wd_b)):
            gate = _dot(hb, wg[0].astype(BF16))
            act = (gate * _sigmoid(gate)) * _dot(hb, wu[0].astype(BF16))
            part = w_slot * _dot(act.astype(BF16), wd[0].astype(BF16))
            y = part if y is None else y + part
        y_ref[...] = y

    @pl.when(jnp.logical_not(used))
    def _():
        y_ref[...] = jnp.zeros_like(y_ref)


def _moe(hs, order_in, order_out, e_a, e_b, swap, n_used, w_gate, w_up, w_down, n_tiles, tm):
    w = hs.shape[1]
    d = w - META_LANES
    f = w_gate.shape[2]
    a3 = lambda i, oin, oout, ea, eb, sw, nu: (ea[i], 0, 0)
    b3 = lambda i, oin, oout, ea, eb, sw, nu: (eb[i], 0, 0)
    grid_spec = pltpu.PrefetchScalarGridSpec(
        num_scalar_prefetch=6,
        grid=(n_tiles,),
        in_specs=[
            pl.BlockSpec((tm, w), lambda i, oin, oout, ea, eb, sw, nu: (oin[i], 0)),
            pl.BlockSpec((1, d, f), a3), pl.BlockSpec((1, d, f), b3),
            pl.BlockSpec((1, d, f), a3), pl.BlockSpec((1, d, f), b3),
            pl.BlockSpec((1, f, d), a3), pl.BlockSpec((1, f, d), b3),
        ],
        out_specs=pl.BlockSpec((tm, d), lambda i, oin, oout, ea, eb, sw, nu: (oout[i], 0)),
    )
    return pl.pallas_call(
        _moe_kernel,
        name="moe_experts",
        grid_spec=grid_spec,
        out_shape=jax.ShapeDtypeStruct((n_tiles * tm, d), F32),
        compiler_params=pltpu.CompilerParams(dimension_semantics=("arbitrary",), vmem_limit_bytes=VMEM_LIMIT),
    )(order_in, order_out, e_a, e_b, swap, n_used, hs, w_gate, w_gate, w_up, w_up, w_down, w_down)


def _combine_kernel(final_norm, pos_ref, x_ref, y_hbm, g_ref, o_ref, rows_ref, sem):
    tc = x_ref.shape[0]
    i = pl.program_id(0)
    n = pl.num_programs(0)

    def issue(tile, slot):
        base = tile * tc
        for r in range(tc):
            pltpu.make_async_copy(y_hbm.at[pl.ds(pos_ref[base + r], 1)], rows_ref.at[slot, pl.ds(r, 1)],
                                  sem.at[slot]).start()

    @pl.when(i == 0)
    def _():
        issue(0, 0)

    for slot in range(2):
        @pl.when(i % 2 == slot)
        def _():
            @pl.when(i + 1 < n)
            def _():
                issue(i + 1, 1 - slot)

            pltpu.make_async_copy(y_hbm.at[pl.ds(0, tc)], rows_ref.at[slot], sem.at[slot]).wait()
            x = x_ref[...] + rows_ref[slot]
            if final_norm:
                x = _rmsnorm(x, g_ref[...])
            o_ref[...] = x


def _combine(x, y_sorted, pos, g, final_norm):
    t, d = x.shape
    tc = COMBINE_TILE
    grid_spec = pltpu.PrefetchScalarGridSpec(
        num_scalar_prefetch=1,
        grid=(t // tc,),
        in_specs=[pl.BlockSpec((tc, d), lambda i, p: (i, 0)), pl.BlockSpec(memory_space=pl.ANY),
                  pl.BlockSpec((1, d), lambda i, p: (0, 0))],
        out_specs=pl.BlockSpec((tc, d), lambda i, p: (i, 0)),
        scratch_shapes=[pltpu.VMEM((2, tc, d), F32), pltpu.SemaphoreType.DMA((2,))],
    )
    return pl.pallas_call(
        functools.partial(_combine_kernel, final_norm),
        name="combine",
        grid_spec=grid_spec,
        out_shape=jax.ShapeDtypeStruct((t, d), F32),
        compiler_params=pltpu.CompilerParams(dimension_semantics=("arbitrary",), vmem_limit_bytes=VMEM_LIMIT),
    )(pos, x, y_sorted, g[None, :])


def kernel(x, mem, positions, norm_mix_g, w_in, b_branch_gate, ret_norm_g, w_ret_out, conv_w, conv_b, conv_ln_g, conv_ln_b, w_conv_out, w_mix_out, norm_xattn_g, norm_mem_g, w_xq, w_xkv, w_xo, norm_ffn_g, w_router, b_router, w_exp_gate, w_exp_up, w_exp_down, norm_final_g):
    b, s, d = x.shape
    depth = w_in.shape[0]
    t = b * s
    assert s % SEQ_TILE == 0 and s % MIXER_TILE == 0 and MIXER_TILE % RET_CHUNK == 0
    assert t % MOE_TILE == 0 and t % COMBINE_TILE == 0
    tm = MOE_TILE
    n_tiles = t // tm + N_CLASSES
    cos, sin = _rope_tables(positions)
    n_exp, _, f = w_exp_gate.shape[1:]
    wg_all = w_exp_gate.reshape(depth * n_exp, d, f)
    wu_all = w_exp_up.reshape(depth * n_exp, d, f)
    wd_all = w_exp_down.reshape(depth * n_exp, f, d)
    moe = None
    for l in range(depth):
        x, hs_zero = _mixer(x, cos, sin, norm_mix_g[l], w_in[l], b_branch_gate[l], ret_norm_g[l], w_ret_out[l],
                            conv_w[l], conv_b[l], conv_ln_g[l], conv_ln_b[l], w_conv_out[l], w_mix_out[l],
                            (n_tiles + 1) * tm, moe)
        kv = _mem_kv(mem, norm_mem_g[l], w_xkv[l].astype(BF16))
        x, meta, alloc, hs = _xattn_route(x, kv, norm_xattn_g[l], w_xq[l], w_xo[l], norm_ffn_g[l],
                                          w_router, b_router, hs_zero)
        pos = meta[3].astype(jnp.int32)
        order_in, order_out, e_a, e_b, swap, n_used = _tile_tables(alloc, n_tiles)
        y_sorted = _moe(hs, order_in, order_out, e_a + l * n_exp, e_b + l * n_exp, swap, n_used,
                        wg_all, wu_all, wd_all, n_tiles, tm)
        moe = (y_sorted, pos)
    return _combine(x.reshape(t, d), y_sorted, pos, norm_final_g, True).reshape(b, s, d)
```
